```python
import jax, jax.numpy as jnp
from jax import lax
import numpy as np

D_MODEL = 1024
BATCH = 4
SEQ = 4096
DEPTH = 2

CHUNK = 64
N_META = 16
EPS = 1e-6
D_FF = 4 * D_MODEL

RET_HEADS = 4
RET_QK_DIM = D_MODEL // 8
RET_V_DIM = D_MODEL // 4
RET_QK = RET_HEADS * RET_QK_DIM
RET_V = RET_HEADS * RET_V_DIM
ROPE_BASE = 10000.0

LRU_WIDTH = D_MODEL
LRU_BLOCKS = 16
LRU_BLOCK_DIM = LRU_WIDTH // LRU_BLOCKS
LRU_C = 8.0
CONV_WIDTH = 4

S5_WIDTH = D_MODEL // 2
S5_GROUP = 16
S5_GROUPS = S5_WIDTH // S5_GROUP
S5_STATE = 64

HG_HEADS = 4
HG_DK = 128
HG_DV = 128
HG_WIDTH = HG_HEADS * HG_DK
HG_V = HG_HEADS * HG_DV

N_EVEN = (DEPTH + 1) // 2
N_ODD = DEPTH // 2
IN_AB = 2 * RET_QK + 2 * RET_V + 2 * LRU_WIDTH
OUT_AB = RET_V + LRU_WIDTH
IN_CD = S5_WIDTH + 3 * HG_WIDTH + HG_V
OUT_CD = S5_WIDTH + HG_V

kernel_name = "hybrid_retention_rglru_s5_hgrn2_block"

F32 = jnp.float32


def rms_norm(x, g):
    xf = x.astype(F32)
    y = xf * lax.rsqrt(jnp.mean(xf * xf, axis=-1, keepdims=True) + EPS)
    return (y * g.astype(F32)).astype(x.dtype)


def head_norm(o, g, center):
    if center:
        o = o - jnp.mean(o, axis=-1, keepdims=True)
    o = o * lax.rsqrt(jnp.mean(o * o, axis=-1, keepdims=True) + EPS)
    return o.reshape(o.shape[0], o.shape[1], -1) * g.astype(F32)


def split_cols(z, sizes):
    offs = np.cumsum([0] + list(sizes))
    return [z[..., int(offs[n]):int(offs[n + 1])] for n in range(len(sizes))]


def to_chunks(t):
    pad = CHUNK - N_META
    t = jnp.pad(t, [(0, 0), (pad, 0)] + [(0, 0)] * (t.ndim - 2))
    return t.reshape(t.shape[0], -1, CHUNK, *t.shape[2:])


def from_chunks(t):
    t = t.reshape(t.shape[0], -1, *t.shape[3:])
    return t[:, CHUNK - N_META:]


def rotary(x, pos):
    half = x.shape[-1] // 2
    inv = ROPE_BASE ** (-jnp.arange(half, dtype=F32) / half)
    ang = pos.astype(F32)[:, None] * inv[None, :]
    cos = jnp.cos(ang)[None, :, None, :]
    sin = jnp.sin(ang)[None, :, None, :]
    x1, x2 = x[..., :half], x[..., half:]
    return jnp.concatenate([x1 * cos - x2 * sin, x1 * sin + x2 * cos], axis=-1)


def retention(q, k, v, gate, gn_g):
    Bn, L, H, dk = q.shape
    pos = jnp.arange(L)
    q = rotary(q.astype(F32), pos) * (dk ** -0.5)
    k = rotary(k.astype(F32), pos)
    v = v.astype(F32).reshape(Bn, L, H, RET_V_DIM)
    log_g = jnp.log1p(-(2.0 ** (-5.0 - jnp.arange(H, dtype=F32))))
    qc, kc, vc = to_chunks(q), to_chunks(k), to_chunks(v)
    idx = jnp.arange(CHUNK, dtype=F32)
    intra_decay = jnp.exp(jnp.abs(idx[:, None] - idx[None, :])[None] * log_g[:, None, None])
    scores = jnp.einsum('bnihd,bnjhd->bnhij', qc, kc) * intra_decay
    o_intra = jnp.einsum('bnhij,bnjhe->bnihe', scores, vc)
    k_dec = kc * jnp.exp((CHUNK - 1 - idx)[:, None] * log_g[None, :])[..., None]
    kv = jnp.einsum('bnjhd,bnjhe->nbhde', k_dec, vc)
    chunk_decay = jnp.exp(CHUNK * log_g)[None, :, None, None]

    def step(state, kv_c):
        return chunk_decay * state + kv_c, state

    _, prev = lax.scan(step, jnp.zeros_like(kv[0]), kv)
    q_dec = qc * jnp.exp((idx + 1.0)[:, None] * log_g[None, :])[..., None]
    o_inter = jnp.einsum('bnihd,nbhde->bnihe', q_dec, prev)
    o = from_chunks(o_intra + o_inter)
    return head_norm(o, gn_g, True) * jax.nn.silu(gate.astype(F32))


def rg_lru(xb, w_a, b_a, w_i, b_i, lam, conv_w, conv_b):
    Bn, L, W = xb.shape
    xc = lax.conv_general_dilated(
        xb, conv_w[:, None, :].astype(xb.dtype), window_strides=(1,),
        padding=[(CONV_WIDTH - 1, 0)], dimension_numbers=('NWC', 'WIO', 'NWC'),
        feature_group_count=W).astype(F32) + conv_b.astype(F32)
    xg = xc.reshape(Bn, L, LRU_BLOCKS, LRU_BLOCK_DIM)
    r = jax.nn.sigmoid(jnp.einsum('blhi,hij->blhj', xg, w_a.astype(F32)).reshape(Bn, L, W) + b_a)
    i = jax.nn.sigmoid(jnp.einsum('blhi,hij->blhj', xg, w_i.astype(F32)).reshape(Bn, L, W) + b_i)
    log_a = -LRU_C * r * jax.nn.softplus(-lam.astype(F32))
    a = jnp.exp(log_a)
    b = jnp.sqrt(-jnp.expm1(2.0 * log_a)) * (i * xc)

    def comb(lhs, rhs):
        a1, b1 = lhs
        a2, b2 = rhs
        return a1 * a2, a2 * b1 + b2

    _, h = lax.associative_scan(comb, (a, b), axis=1)
    return h


def s5(u, a_re_log, a_im, b_re, b_im, c_re, c_im, d, log_dt, glu_w, glu_b):
    Bn, L, _ = u.shape
    ug = u.astype(F32).reshape(Bn, L, S5_GROUPS, S5_GROUP)
    dt = jnp.exp(log_dt.astype(F32))[:, None]
    a_re = -jnp.exp(a_re_log.astype(F32))
    a_im = a_im.astype(F32)
    mag = jnp.exp(dt * a_re)
    ab_re = mag * jnp.cos(dt * a_im)
    ab_im = mag * jnp.sin(dt * a_im)
    n_re, n_im = ab_re - 1.0, ab_im
    den = a_re * a_re + a_im * a_im
    z_re = (n_re * a_re + n_im * a_im) / den
    z_im = (n_im * a_re - n_re * a_im) / den
    b_re = b_re.astype(F32)
    b_im = b_im.astype(F32)
    bb_re = z_re[..., None] * b_re - z_im[..., None] * b_im
    bb_im = z_re[..., None] * b_im + z_im[..., None] * b_re
    bu_re = jnp.einsum('blgc,gpc->blgp', ug, bb_re)
    bu_im = jnp.einsum('blgc,gpc->blgp', ug, bb_im)
    ar = jnp.broadcast_to(ab_re, bu_re.shape)
    ai = jnp.broadcast_to(ab_im, bu_re.shape)

    def comb(lhs, rhs):
        ar1, ai1, br1, bi1 = lhs
        ar2, ai2, br2, bi2 = rhs
        return (ar1 * ar2 - ai1 * ai2, ar1 * ai2 + ai1 * ar2,
                ar2 * br1 - ai2 * bi1 + br2, ar2 * bi1 + ai2 * br1 + bi2)

    _, _, h_re, h_im = lax.associative_scan(comb, (ar, ai, bu_re, bu_im), axis=1)
    y = (jnp.einsum('blgp,gcp->blgc', h_re, c_re.astype(F32))
         - jnp.einsum('blgp,gcp->blgc', h_im, c_im.astype(F32))
         + d.astype(F32) * ug)
    y = jax.nn.gelu(y.reshape(Bn, L, S5_WIDTH))
    return y * jax.nn.sigmoid(y @ glu_w.astype(F32) + glu_b.astype(F32))


def hgrn2(q, f_pre, i, g, lb, gn_g):
    Bn, L, H, dk = q.shape
    lb = lb.astype(F32).reshape(H, dk)
    f = lb + (1.0 - lb) * jax.nn.sigmoid(f_pre.astype(F32))
    log_f = jnp.log(f)
    k = 1.0 - f
    qc = to_chunks(q.astype(F32))
    kc = to_chunks(k)
    ic = to_chunks(i.astype(F32))
    cum = jnp.cumsum(to_chunks(log_f), axis=2)
    xs = tuple(jnp.moveaxis(t, 1, 0) for t in (qc, kc, ic, cum))
    mask = jnp.tril(jnp.ones((CHUNK, CHUNK), dtype=bool))[None, :, :, None, None]

    def step(S, inp):
        qb, kb, ib, cb = inp
        diff = cb[:, :, None] - cb[:, None, :]
        w = jnp.exp(jnp.where(mask, diff, -jnp.inf))
        att = jnp.einsum('bthd,btshd->bhts', qb, w * kb[:, None])
        o_intra = jnp.einsum('bhts,bshe->bthe', att, ib)
        o_inter = jnp.einsum('bthd,bhde->bthe', qb * jnp.exp(cb), S)
        total = cb[:, -1]
        k_dec = kb * jnp.exp(total[:, None] - cb)
        S_new = jnp.exp(total)[..., None] * S + jnp.einsum('bshd,bshe->bhde', k_dec, ib)
        return S_new, o_intra + o_inter

    S0 = jnp.zeros((Bn, H, dk, HG_DV), F32)
    _, o = lax.scan(step, S0, xs)
    o = from_chunks(jnp.moveaxis(o, 0, 1))
    return head_norm(o, gn_g, False) * jax.nn.silu(g.astype(F32))


def mixer_ab(h, w_in, w_out, ret_gn, rg_wa, rg_ba, rg_wi, rg_bi, rg_lam, rg_conv_w, rg_conv_b):
    Bn, L, _ = h.shape
    z = h @ w_in
    q, k, v, gate, bx, bg = split_cols(z, [RET_QK, RET_QK, RET_V, RET_V, LRU_WIDTH, LRU_WIDTH])
    ya = retention(q.reshape(Bn, L, RET_HEADS, RET_QK_DIM), k.reshape(Bn, L, RET_HEADS, RET_QK_DIM),
                   v, gate, ret_gn)
    yb = jax.nn.gelu(bg.astype(F32)) * rg_lru(bx, rg_wa, rg_ba, rg_wi, rg_bi, rg_lam, rg_conv_w, rg_conv_b)
    return jnp.concatenate([ya, yb], axis=-1) @ w_out.astype(F32)


def mixer_cd(h, w_in, w_out, a_re_log, a_im, b_re, b_im, c_re, c_im, d, log_dt, glu_w, glu_b, hg_gn, lb):
    Bn, L, _ = h.shape
    z = h @ w_in
    u, q, f, i, g = split_cols(z, [S5_WIDTH, HG_WIDTH, HG_WIDTH, HG_V, HG_V])
    yc = s5(u, a_re_log, a_im, b_re, b_im, c_re, c_im, d, log_dt, glu_w, glu_b)
    yd = hgrn2(q.reshape(Bn, L, HG_HEADS, HG_DK), f.reshape(Bn, L, HG_HEADS, HG_DK),
               i.reshape(Bn, L, HG_HEADS, HG_DV), g, lb, hg_gn)
    return jnp.concatenate([yc, yd], axis=-1) @ w_out.astype(F32)


def sq_relu_mlp(h, w1, w2):
    a = jax.nn.relu(h @ w1)
    return (a * a) @ w2


def setup_inputs(seed: int = 0) -> dict:
    key = jax.random.key(seed)
    ks = iter(jax.random.split(key, 40))
    nrm = lambda shape, s: jax.random.normal(next(ks), shape, F32) * s
    u_lru = jax.random.uniform(next(ks), (N_EVEN, LRU_WIDTH), F32, 0.9, 0.999)
    a_lru = u_lru ** (1.0 / LRU_C)
    return {
        "x": nrm((BATCH, SEQ, D_MODEL), 1.0),
        "meta": nrm((N_META, D_MODEL), 1.0),
        "w_in_ab": nrm((N_EVEN, D_MODEL, IN_AB), D_MODEL ** -0.5),
        "w_out_ab": nrm((N_EVEN, OUT_AB, D_MODEL), OUT_AB ** -0.5),
        "ret_gn": 1.0 + nrm((N_EVEN, RET_V), 0.02),
        "rg_wa": nrm((N_EVEN, LRU_BLOCKS, LRU_BLOCK_DIM, LRU_BLOCK_DIM), LRU_BLOCK_DIM ** -0.5),
        "rg_ba": nrm((N_EVEN, LRU_WIDTH), 0.02),
        "rg_wi": nrm((N_EVEN, LRU_BLOCKS, LRU_BLOCK_DIM, LRU_BLOCK_DIM), LRU_BLOCK_DIM ** -0.5),
        "rg_bi": nrm((N_EVEN, LRU_WIDTH), 0.02),
        "rg_lam": jnp.log(a_lru) - jnp.log1p(-a_lru),
        "rg_conv_w": nrm((N_EVEN, CONV_WIDTH, LRU_WIDTH), CONV_WIDTH ** -0.5),
        "rg_conv_b": nrm((N_EVEN, LRU_WIDTH), 0.02),
        "w_in_cd": nrm((N_ODD, D_MODEL, IN_CD), D_MODEL ** -0.5),
        "w_out_cd": nrm((N_ODD, OUT_CD, D_MODEL), OUT_CD ** -0.5),
        "s5_a_re_log": jnp.log(0.5) + nrm((N_ODD, S5_GROUPS, S5_STATE), 0.02),
        "s5_a_im": jnp.broadcast_to(jnp.pi * jnp.arange(S5_STATE, dtype=F32), (N_ODD, S5_GROUPS, S5_STATE))
                   + nrm((N_ODD, S5_GROUPS, S5_STATE), 0.02),
        "s5_b_re": nrm((N_ODD, S5_GROUPS, S5_STATE, S5_GROUP), (2 * S5_GROUP) ** -0.5),
        "s5_b_im": nrm((N_ODD, S5_GROUPS, S5_STATE, S5_GROUP), (2 * S5_GROUP) ** -0.5),
        "s5_c_re": nrm((N_ODD, S5_GROUPS, S5_GROUP, S5_STATE), (2 * S5_STATE) ** -0.5),
        "s5_c_im": nrm((N_ODD, S5_GROUPS, S5_GROUP, S5_STATE), (2 * S5_STATE) ** -0.5),
        "s5_d": nrm((N_ODD, S5_GROUPS, S5_GROUP), 1.0),
        "s5_log_dt": jax.random.uniform(next(ks), (N_ODD, S5_GROUPS), F32, np.log(1e-3), np.log(1e-1)),
        "s5_glu_w": nrm((N_ODD, S5_WIDTH, S5_WIDTH), S5_WIDTH ** -0.5),
        "s5_glu_b": nrm((N_ODD, S5_WIDTH), 0.02),
        "hg_gn": 1.0 + nrm((N_ODD, HG_V), 0.02),
        "hg_lb_logits": nrm((DEPTH, HG_WIDTH), 0.5),
        "norm_g": 1.0 + nrm((DEPTH, 4, D_MODEL), 0.02),
        "mlp_w1": nrm((DEPTH, D_MODEL, D_FF), D_MODEL ** -0.5),
        "mlp_w2": nrm((DEPTH, D_FF, D_MODEL), D_FF ** -0.5),
    }


def reference(x, meta, w_in_ab, w_out_ab, ret_gn, rg_wa, rg_ba, rg_wi, rg_bi, rg_lam, rg_conv_w, rg_conv_b,
              w_in_cd, w_out_cd, s5_a_re_log, s5_a_im, s5_b_re, s5_b_im, s5_c_re, s5_c_im, s5_d, s5_log_dt,
              s5_glu_w, s5_glu_b, hg_gn, hg_lb_logits, norm_g, mlp_w1, mlp_w2):
    Bn = x.shape[0]
    h = jnp.concatenate([jnp.broadcast_to(meta[None].astype(x.dtype), (Bn, N_META, D_MODEL)), x], axis=1)
    p = jax.nn.softmax(hg_lb_logits.astype(F32), axis=0)
    lb_all = jnp.cumsum(p, axis=0) - p
    for l in range(DEPTH):
        j = l // 2
        hn = rms_norm(h, norm_g[l, 0])
        if l % 2 == 0:
            m = mixer_ab(hn, w_in_ab[j], w_out_ab[j], ret_gn[j], rg_wa[j], rg_ba[j], rg_wi[j], rg_bi[j],
                         rg_lam[j], rg_conv_w[j], rg_conv_b[j])
        else:
            m = mixer_cd(hn, w_in_cd[j], w_out_cd[j], s5_a_re_log[j], s5_a_im[j], s5_b_re[j], s5_b_im[j],
                         s5_c_re[j], s5_c_im[j], s5_d[j], s5_log_dt[j], s5_glu_w[j], s5_glu_b[j],
                         hg_gn[j], lb_all[l])
        h = h + rms_norm(m, norm_g[l, 1])
        hn = rms_norm(h, norm_g[l, 2])
        h = h + rms_norm(sq_relu_mlp(hn, mlp_w1[l], mlp_w2[l]), norm_g[l, 3])
    return h[:, N_META:]
```

```python
import functools
import math

import jax
import jax.numpy as jnp
from jax import lax
from jax.experimental import pallas as pl
from jax.experimental.pallas import tpu as pltpu

F32 = jnp.float32
BF16 = jnp.bfloat16

CHUNK = 64
N_META = 16
PAD = CHUNK - N_META
EPS = 1e-6

RET_HEADS = 4
RET_DK = 128
RET_DV = 256
ROPE_BASE = 10000.0
LRU_C = 8.0
CONV_WIDTH = 4
LRU_PAIR = 128
S5_GROUP = 16
S5_STATE = 64
S5_LANES = 512
HG_HEADS = 4
HG_DK = 128
HG_SUB = 16
SUBLANES = 8
VMEM_LIMIT_BYTES = 56 * 1024 * 1024


def _largest_divisor(n, cap):
    return max(d for d in range(1, cap + 1) if n % d == 0)


def _gelu_tanh(x):
    return 0.5 * x * (1.0 + jnp.tanh(0.7978845608028654 * (x + 0.044715 * x * x * x)))


def _dot(a, b):
    return jnp.dot(a, b, preferred_element_type=F32)


def _dot_nt(a, b):
    return lax.dot_general(a, b, (((1,), (1,)), ((), ())), preferred_element_type=F32)


def _norm_matmul_kernel(x_ref, g_ref, w_ref, o_ref, hn_ref, *, relu2):
    @pl.when(pl.program_id(1) == 0)
    def _():
        x = x_ref[...]
        ms = jnp.mean(x * x, axis=-1, keepdims=True)
        hn_ref[...] = (x * lax.rsqrt(ms + EPS) * g_ref[...]).astype(BF16)

    acc = _dot(hn_ref[...], w_ref[...])
    if relu2:
        acc = jnp.maximum(acc, 0.0)
        acc = acc * acc
    o_ref[...] = acc.astype(o_ref.dtype)


def _norm_matmul(h, g, w, *, tm, tn, relu2, out_dtype):
    m, d = h.shape
    n = w.shape[1]
    return pl.pallas_call(
        functools.partial(_norm_matmul_kernel, relu2=relu2),
        grid=(m // tm, n // tn),
        in_specs=[
            pl.BlockSpec((tm, d), lambda i, j: (i, 0)),
            pl.BlockSpec((1, d), lambda i, j: (0, 0)),
            pl.BlockSpec((d, tn), lambda i, j: (0, j)),
        ],
        out_specs=pl.BlockSpec((tm, tn), lambda i, j: (i, j)),
        out_shape=jax.ShapeDtypeStruct((m, n), out_dtype),
        scratch_shapes=[pltpu.VMEM((tm, d), BF16)],
        compiler_params=pltpu.CompilerParams(
            dimension_semantics=("arbitrary", "arbitrary"), vmem_limit_bytes=VMEM_LIMIT_BYTES),
        name="norm_matmul",
    )(h, g.reshape(1, d), w)


def _matmul_norm_res_kernel(a_ref, w_ref, g_ref, h_ref, o_ref, acc_ref, *, tiles_per_batch):
    k = pl.program_id(1)

    @pl.when(k == 0)
    def _():
        acc_ref[...] = jnp.zeros_like(acc_ref)

    acc_ref[...] += _dot(a_ref[...].astype(BF16), w_ref[...])

    @pl.when(k == pl.num_programs(1) - 1)
    def _():
        m = acc_ref[...]
        ms = jnp.mean(m * m, axis=-1, keepdims=True)
        out = h_ref[...] + m * lax.rsqrt(ms + EPS) * g_ref[...]
        first_tile = lax.rem(pl.program_id(0), tiles_per_batch) == 0
        row = lax.broadcasted_iota(jnp.int32, (out.shape[0], 1), 0)
        keep = jnp.logical_or(row >= PAD, jnp.logical_not(first_tile))
        o_ref[...] = jnp.where(keep, out, 0.0)


def _matmul_norm_res(a, w, g, h, *, tm, tk, rows_per_batch):
    m, kdim = a.shape
    d = w.shape[1]
    return pl.pallas_call(
        functools.partial(_matmul_norm_res_kernel, tiles_per_batch=rows_per_batch // tm),
        grid=(m // tm, kdim // tk),
        in_specs=[
            pl.BlockSpec((tm, tk), lambda i, k: (i, k)),
            pl.BlockSpec((tk, d), lambda i, k: (k, 0)),
            pl.BlockSpec((1, d), lambda i, k: (0, 0)),
            pl.BlockSpec((tm, d), lambda i, k: (i, 0)),
        ],
        out_specs=pl.BlockSpec((tm, d), lambda i, k: (i, 0)),
        out_shape=jax.ShapeDtypeStruct((m, d), F32),
        scratch_shapes=[pltpu.VMEM((tm, d), F32)],
        compiler_params=pltpu.CompilerParams(
            dimension_semantics=("arbitrary", "arbitrary"), vmem_limit_bytes=VMEM_LIMIT_BYTES),
        name="matmul_norm_res",
    )(a, w, g.reshape(1, d), h)


def _mixer_ab_kernel(q_ref, k_ref, v_ref, gate_ref, bx_ref, bg_ref, gn_ref, wg_ref, ba_ref, bi_ref,
                     lam_ref, cw_ref, cb_ref, y_ref,
                     s_ref, cos_ref, sin_ref, xcar_ref, hcar_ref, a_buf, b_buf, *, chunks):
    j = pl.program_id(1)
    rows = chunks * CHUNK
    width = bx_ref.shape[1]

    @pl.when(j == 0)
    def _():
        s_ref[...] = jnp.zeros_like(s_ref)
        xcar_ref[...] = jnp.zeros_like(xcar_ref)
        hcar_ref[...] = jnp.zeros_like(hcar_ref)

    row = lax.broadcasted_iota(jnp.int32, (rows, 1), 0)
    pos = (j * rows + row - PAD).astype(F32)
    lane = lax.broadcasted_iota(jnp.int32, (1, RET_DK), 1)
    half = RET_DK // 2
    freq = jnp.exp((lane & (half - 1)).astype(F32) * (-math.log(ROPE_BASE) / half))
    ang = pos * freq
    cos_ref[...] = jnp.cos(ang)
    sin_ref[...] = jnp.where(lane < half, -1.0, 1.0) * jnp.sin(ang)

    idx = lax.broadcasted_iota(jnp.int32, (CHUNK, 1), 0).astype(F32)
    ti = lax.broadcasted_iota(jnp.int32, (CHUNK, CHUNK), 0)
    si = lax.broadcasted_iota(jnp.int32, (CHUNK, CHUNK), 1)
    dist = jnp.abs(ti - si).astype(F32)

    def chunk_body(c, carry):
        r0 = pl.multiple_of(c * CHUNK, CHUNK)
        rs = pl.ds(r0, CHUNK)
        cosv = cos_ref[rs, :]
        sinv = sin_ref[rs, :]
        for h in range(RET_HEADS):
            log_g = math.log1p(-(2.0 ** (-5.0 - h)))
            qs = slice(h * RET_DK, (h + 1) * RET_DK)
            vs = slice(h * RET_DV, (h + 1) * RET_DV)
            qh = q_ref[rs, qs]
            kh = k_ref[rs, qs]
            qr = (qh * cosv + pltpu.roll(qh, half, 1) * sinv) * (RET_DK ** -0.5)
            kr = kh * cosv + pltpu.roll(kh, half, 1) * sinv
            vh = v_ref[rs, vs].astype(BF16)
            scores = _dot_nt(qr.astype(BF16), kr.astype(BF16)) * jnp.exp(dist * log_g)
            o = _dot(scores.astype(BF16), vh)
            q_dec = qr * jnp.exp((idx + 1.0) * log_g)
            o = o + _dot(q_dec.astype(BF16), s_ref[h].astype(BF16))
            k_dec = kr * jnp.exp((CHUNK - 1.0 - idx) * log_g)
            kv = _dot(k_dec.T.astype(BF16), vh)
            s_ref[h] = math.exp(CHUNK * log_g) * s_ref[h] + kv
            oc = o - jnp.mean(o, axis=-1, keepdims=True)
            var = jnp.mean(oc * oc, axis=-1, keepdims=True)
            gt = gate_ref[rs, vs]
            y_ref[rs, vs] = oc * lax.rsqrt(var + EPS) * gn_ref[:, vs] * (gt * jax.nn.sigmoid(gt))
        return carry

    lax.fori_loop(0, chunks, chunk_body, 0)

    xb = bx_ref[...]
    xe = jnp.concatenate([xcar_ref[...], xb], axis=0)
    xc = cb_ref[...] + xb * cw_ref[CONV_WIDTH - 1:CONV_WIDTH, :]
    for s in range(1, CONV_WIDTH):
        xc = xc + pltpu.roll(xe, s, 0)[SUBLANES:, :] * cw_ref[CONV_WIDTH - 1 - s:CONV_WIDTH - s, :]
    xcar_ref[...] = xb[rows - SUBLANES:, :]

    valid = jnp.logical_or(row >= PAD, j > 0)
    xcb = xc.astype(BF16)
    for p in range(width // LRU_PAIR):
        cs = slice(p * LRU_PAIR, (p + 1) * LRU_PAIR)
        g2 = _dot(xcb[:, cs], wg_ref[p])
        r = jax.nn.sigmoid(g2[:, :LRU_PAIR] + ba_ref[:, cs])
        i = jax.nn.sigmoid(g2[:, LRU_PAIR:] + bi_ref[:, cs])
        lam = lam_ref[:, cs]
        softplus_neg_lam = jnp.maximum(-lam, 0.0) + jnp.log1p(jnp.exp(-jnp.abs(lam)))
        log_a = -LRU_C * r * softplus_neg_lam
        a = jnp.exp(log_a)
        b = jnp.sqrt(1.0 - a * a) * (i * xc[:, cs])
        a_buf[:, cs] = a
        b_buf[:, cs] = jnp.where(valid, b, 0.0)

    a3 = a_buf[...].reshape(rows // SUBLANES, SUBLANES, width)
    b3 = b_buf[...].reshape(rows // SUBLANES, SUBLANES, width)
    sub = lax.broadcasted_iota(jnp.int32, (1, SUBLANES, 1), 1)
    for s in (1, 2, 4):
        m = sub >= s
        a_sh = jnp.where(m, pltpu.roll(a3, s, 1), 1.0)
        b_sh = jnp.where(m, pltpu.roll(b3, s, 1), 0.0)
        b3 = b3 + a3 * b_sh
        a3 = a3 * a_sh
    a_buf[...] = a3.reshape(rows, width)
    b_buf[...] = b3.reshape(rows, width)

    def group_body(gidx, hprev):
        g0 = pl.multiple_of(gidx * SUBLANES, SUBLANES)
        gs = pl.ds(g0, SUBLANES)
        hr = b_buf[gs, :] + a_buf[gs, :] * hprev
        b_buf[gs, :] = hr
        return hr[SUBLANES - 1:SUBLANES, :]

    hlast = lax.fori_loop(0, rows // SUBLANES, group_body, hcar_ref[0:1, :])
    hcar_ref[0:1, :] = hlast
    y_ref[:, RET_HEADS * RET_DV:] = _gelu_tanh(bg_ref[...]) * b_buf[...]


def _mixer_ab(z, ret_gn, wg, ba, bi, lam, conv_w, conv_b, *, batch, chunks):
    m = z.shape[0]
    rows = chunks * CHUNK
    nb = m // batch // rows
    qk = RET_HEADS * RET_DK
    vw = RET_HEADS * RET_DV
    lw = lam.shape[-1]

    def zspec(width, col):
        return pl.BlockSpec((rows, width), lambda b, j: (b * nb + j, col))

    def pspec(shape):
        return pl.BlockSpec(shape, lambda b, j: (0,) * len(shape))

    assert vw == 2 * qk and lw == vw
    return pl.pallas_call(
        functools.partial(_mixer_ab_kernel, chunks=chunks),
        grid=(batch, nb),
        in_specs=[
            zspec(qk, 0), zspec(qk, 1), zspec(vw, 1), zspec(vw, 2), zspec(lw, 3), zspec(lw, 4),
            pspec((1, vw)), pspec(wg.shape), pspec((1, lw)), pspec((1, lw)), pspec((1, lw)),
            pspec((CONV_WIDTH, lw)), pspec((1, lw)),
        ],
        out_specs=pl.BlockSpec((rows, vw + lw), lambda b, j: (b * nb + j, 0)),
        out_shape=jax.ShapeDtypeStruct((m, vw + lw), F32),
        scratch_shapes=[
            pltpu.VMEM((RET_HEADS, RET_DK, RET_DV), F32),
            pltpu.VMEM((rows, RET_DK), F32),
            pltpu.VMEM((rows, RET_DK), F32),
            pltpu.VMEM((SUBLANES, lw), F32),
            pltpu.VMEM((SUBLANES, lw), F32),
            pltpu.VMEM((rows, lw), F32),
            pltpu.VMEM((rows, lw), F32),
        ],
        compiler_params=pltpu.CompilerParams(
            dimension_semantics=("arbitrary", "arbitrary"), vmem_limit_bytes=VMEM_LIMIT_BYTES),
        name="mixer_ab",
    )(z, z, z, z, z, z, ret_gn.reshape(1, vw), wg, ba.reshape(1, lw), bi.reshape(1, lw),
      lam.reshape(1, lw), conv_w, conv_b.reshape(1, lw))


_S5_TAB_ROWS = 8


def _mixer_cd_kernel(u_ref, q_ref, f_ref, i_ref, g_ref,
                     arl_ref, aim_ref, ldt_ref, bre_ref, bim_ref, cre_ref, cim_ref, d_ref,
                     gw_ref, gb_ref, gn_ref, lbl_ref, y_ref,
                     bm_ref, tab_ref, hcar_ref, wre_ref, wim_ref, ybuf_ref, st_ref, *, chunks, layer):
    b = pl.program_id(0)
    j = pl.program_id(1)
    rows = chunks * CHUNK
    ngroups = rows // SUBLANES
    ncol = bre_ref.shape[0]
    cin = bre_ref.shape[1]

    @pl.when(jnp.logical_and(b == 0, j == 0))
    def _():
        dt = jnp.exp(ldt_ref[...])
        a_re = -jnp.exp(arl_ref[...])
        a_im = aim_ref[...]
        subl = lax.broadcasted_iota(jnp.int32, (SUBLANES, 1), 0)

        def power(n):
            mag = jnp.exp(n * dt * a_re)
            return mag * jnp.cos(n * dt * a_im), mag * jnp.sin(n * dt * a_im)

        ab_re, ab_im = power(1.0)
        den = a_re * a_re + a_im * a_im
        z_re = ((ab_re - 1.0) * a_re + ab_im * a_im) / den
        z_im = (ab_im * a_re - (ab_re - 1.0) * a_im) / den
        steps = []
        for s in (1, 2, 4):
            p_re, p_im = power(float(s))
            steps.append((jnp.where(subl >= s, p_re, 0.0), jnp.where(subl >= s, p_im, 0.0)))
        nrow = (subl + 1).astype(F32)
        mag = jnp.exp(nrow * dt * a_re)
        f_re = mag * jnp.cos(nrow * dt * a_im)
        f_im = mag * jnp.sin(nrow * dt * a_im)
        for jb in range(ncol):
            ls = slice(jb * S5_LANES, (jb + 1) * S5_LANES)
            for si, (m_re, m_im) in enumerate(steps):
                tab_ref[jb, 2 * si] = m_re[:, ls]
                tab_ref[jb, 2 * si + 1] = m_im[:, ls]
            tab_ref[jb, 6] = f_re[:, ls]
            tab_ref[jb, 7] = f_im[:, ls]
            zr = z_re[:, ls]
            zi = z_im[:, ls]
            bb_re = zr * bre_ref[jb] - zi * bim_ref[jb]
            bb_im = zr * bim_ref[jb] + zi * bre_ref[jb]
            bm_ref[jb] = jnp.concatenate([bb_re, bb_im], axis=1).astype(BF16)

    @pl.when(j == 0)
    def _():
        hcar_ref[...] = jnp.zeros_like(hcar_ref)
        st_ref[...] = jnp.zeros_like(st_ref)

    u = u_ref[...]
    ub = u.astype(BF16)
    for jb in range(ncol):
        cs = slice(jb * cin, (jb + 1) * cin)
        bu = _dot(ub[:, cs], bm_ref[jb])
        hre = bu[:, :S5_LANES].reshape(ngroups, SUBLANES, S5_LANES)
        him = bu[:, S5_LANES:].reshape(ngroups, SUBLANES, S5_LANES)
        for si, s in enumerate((1, 2, 4)):
            m_re = tab_ref[jb, 2 * si]
            m_im = tab_ref[jb, 2 * si + 1]
            s_re = pltpu.roll(hre, s, 1)
            s_im = pltpu.roll(him, s, 1)
            hre, him = hre + m_re * s_re - m_im * s_im, him + m_re * s_im + m_im * s_re
        wre_ref[...] = hre.reshape(rows, S5_LANES)
        wim_ref[...] = him.reshape(rows, S5_LANES)
        f_re = tab_ref[jb, 6]
        f_im = tab_ref[jb, 7]

        def group_body(gidx, carry):
            c_re, c_im = carry
            g0 = pl.multiple_of(gidx * SUBLANES, SUBLANES)
            gs = pl.ds(g0, SUBLANES)
            hr = wre_ref[gs, :] + f_re * c_re - f_im * c_im
            hi = wim_ref[gs, :] + f_re * c_im + f_im * c_re
            wre_ref[gs, :] = hr
            wim_ref[gs, :] = hi
            return hr[SUBLANES - 1:SUBLANES, :], hi[SUBLANES - 1:SUBLANES, :]

        c_re, c_im = lax.fori_loop(0, ngroups, group_body, (hcar_ref[jb, 0:1, :], hcar_ref[jb, 1:2, :]))
        hcar_ref[jb, 0:1, :] = c_re
        hcar_ref[jb, 1:2, :] = c_im
        y = (_dot(wre_ref[...].astype(BF16), cre_ref[jb]) - _dot(wim_ref[...].astype(BF16), cim_ref[jb])
             + d_ref[:, cs] * u[:, cs])
        ybuf_ref[:, cs] = _gelu_tanh(y)
    yg = ybuf_ref[...]
    s5w = yg.shape[1]
    y_ref[:, :s5w] = yg * jax.nn.sigmoid(_dot(yg.astype(BF16), gw_ref[...]) + gb_ref[...])

    logits = lbl_ref[...]
    pexp = jnp.exp(logits - jnp.max(logits, axis=0, keepdims=True))
    psm = pexp / jnp.sum(pexp, axis=0, keepdims=True)
    lb = jnp.zeros_like(psm[0:1, :])
    for l in range(layer):
        lb = lb + psm[l:l + 1, :]

    hw = HG_HEADS * HG_DK
    ti = lax.broadcasted_iota(jnp.int32, (CHUNK, CHUNK), 0)
    si = lax.broadcasted_iota(jnp.int32, (CHUNK, CHUNK), 1)
    tril = (ti >= si).astype(BF16)
    trow = lax.broadcasted_iota(jnp.int32, (HG_SUB, 1), 0)
    hi_ = lax.broadcasted_iota(jnp.int32, (2 * HG_DK, 2 * HG_DK), 0) < HG_DK
    hj_ = lax.broadcasted_iota(jnp.int32, (2 * HG_DK, 2 * HG_DK), 1) < HG_DK
    ones_bd = (hi_ == hj_).astype(BF16)
    nsub = CHUNK // HG_SUB

    def chunk_body(c, carry):
        r0 = pl.multiple_of(c * CHUNK, CHUNK)
        rs = pl.ds(r0, CHUNK)
        f = lb + (1.0 - lb) * jax.nn.sigmoid(f_ref[rs, :])
        logf = jnp.log(f)
        kk = 1.0 - f
        lf_hi = logf.astype(BF16)
        lf_lo = (logf - lf_hi.astype(F32)).astype(BF16)
        cum = _dot(tril, lf_hi) + _dot(tril, lf_lo)
        total = cum[CHUNK - 1:CHUNK, :]
        q = q_ref[rs, :]
        iv = i_ref[rs, :]
        ivb = iv.astype(BF16)
        q_in = (q * jnp.exp(cum)).astype(BF16)
        k_dec = kk * jnp.exp(total - cum)
        dec = jnp.exp(total)

        o_parts = []
        for h in range(HG_HEADS):
            hs = slice(h * HG_DK, (h + 1) * HG_DK)
            st = st_ref[h]
            o_parts.append(_dot_nt(q_in[:, hs], st.astype(BF16)))
            st_ref[h] = dec[:, hs] * st + _dot(iv[:, hs].T.astype(BF16), k_dec[:, hs].astype(BF16))
        o = jnp.concatenate(o_parts, axis=1)

        blocks = []
        for blk in range(nsub):
            b0 = blk * HG_SUB
            bs = slice(b0, b0 + HG_SUB)
            q_blk = q[bs, :]
            cum_blk = cum[bs, :]
            acc = o[bs, :]
            if blk > 0:
                c_ref_row = cum[b0 - 1:b0, :]
                q_t = (q_blk * jnp.exp(cum_blk - c_ref_row)).astype(BF16)
                k_t = (kk[:b0, :] * jnp.exp(c_ref_row - cum[:b0, :])).astype(BF16)
                parts = []
                for h in range(HG_HEADS):
                    hs = slice(h * HG_DK, (h + 1) * HG_DK)
                    att = _dot_nt(q_t[:, hs], k_t[:, hs])
                    parts.append(_dot(att.astype(BF16), ivb[:b0, hs]))
                acc = acc + jnp.concatenate(parts, axis=1)
            ws = []
            for s in range(HG_SUB):
                diff = jnp.where(trow >= s, cum_blk - cum[b0 + s:b0 + s + 1, :], -jnp.inf)
                ws.append((q_blk * kk[b0 + s:b0 + s + 1, :] * jnp.exp(diff)).astype(BF16))
            w_all = jnp.concatenate(ws, axis=0)
            att_parts = [_dot(w_all[:, p * 2 * HG_DK:(p + 1) * 2 * HG_DK], ones_bd)
                         for p in range(hw // (2 * HG_DK))]
            att_all = jnp.concatenate(att_parts, axis=1)
            for s in range(HG_SUB):
                acc = acc + att_all[s * HG_SUB:(s + 1) * HG_SUB, :] * iv[b0 + s:b0 + s + 1, :]
            blocks.append(acc)
        o = jnp.concatenate(blocks, axis=0)

        gt = g_ref[rs, :]
        gate = gt * jax.nn.sigmoid(gt)
        for h in range(HG_HEADS):
            hs = slice(h * HG_DK, (h + 1) * HG_DK)
            oh = o[:, hs]
            ms = jnp.mean(oh * oh, axis=-1, keepdims=True)
            y_ref[rs, s5w + h * HG_DK:s5w + (h + 1) * HG_DK] = (
                oh * lax.rsqrt(ms + EPS) * gn_ref[:, hs] * gate[:, hs])
        return carry

    lax.fori_loop(0, chunks, chunk_body, 0)


def _block_diag(t):
    n, g, r, c = t.shape
    out = jnp.zeros((n, g, r, g, c), t.dtype)
    for gi in range(g):
        out = out.at[:, gi, :, gi, :].set(t[:, gi])
    return out.reshape(n, g * r, g * c)


def _mixer_cd(z, a_re_log, a_im, b_re, b_im, c_re, c_im, d, log_dt, glu_w, glu_b, hg_gn, lb_logits,
              *, batch, chunks, layer):
    m = z.shape[0]
    rows = chunks * CHUNK
    nb = m // batch // rows
    groups, state = a_re_log.shape
    s5w = groups * S5_GROUP
    nstate = groups * state
    gpc = S5_LANES // state
    ncol = groups // gpc
    cin = gpc * S5_GROUP
    hw = HG_HEADS * HG_DK

    bre = _block_diag(jnp.transpose(b_re.reshape(ncol, gpc, state, S5_GROUP), (0, 1, 3, 2)))
    bim = _block_diag(jnp.transpose(b_im.reshape(ncol, gpc, state, S5_GROUP), (0, 1, 3, 2)))
    cre = _block_diag(jnp.transpose(c_re.reshape(ncol, gpc, S5_GROUP, state), (0, 1, 3, 2))).astype(BF16)
    cim = _block_diag(jnp.transpose(c_im.reshape(ncol, gpc, S5_GROUP, state), (0, 1, 3, 2))).astype(BF16)
    ldt = jnp.repeat(log_dt, state).reshape(1, nstate)

    def zspec(col):
        return pl.BlockSpec((rows, s5w), lambda b, j: (b * nb + j, col))

    def pspec(shape):
        return pl.BlockSpec(shape, lambda b, j: (0,) * len(shape))

    assert s5w == hw
    return pl.pallas_call(
        functools.partial(_mixer_cd_kernel, chunks=chunks, layer=layer),
        grid=(batch, nb),
        in_specs=[
            zspec(0), zspec(1), zspec(2), zspec(3), zspec(4),
            pspec((1, nstate)), pspec((1, nstate)), pspec((1, nstate)),
            pspec(bre.shape), pspec(bim.shape), pspec(cre.shape), pspec(cim.shape), pspec((1, s5w)),
            pspec((s5w, s5w)), pspec((1, s5w)), pspec((1, hw)), pspec(lb_logits.shape),
        ],
        out_specs=pl.BlockSpec((rows, s5w + hw), lambda b, j: (b * nb + j, 0)),
        out_shape=jax.ShapeDtypeStruct((m, s5w + hw), F32),
        scratch_shapes=[
            pltpu.VMEM((ncol, cin, 2 * S5_LANES), BF16),
            pltpu.VMEM((ncol, _S5_TAB_ROWS, SUBLANES, S5_LANES), F32),
            pltpu.VMEM((ncol, SUBLANES, S5_LANES), F32),
            pltpu.VMEM((rows, S5_LANES), F32),
            pltpu.VMEM((rows, S5_LANES), F32),
            pltpu.VMEM((rows, s5w), F32),
            pltpu.VMEM((HG_HEADS, HG_DK, HG_DK), F32),
        ],
        compiler_params=pltpu.CompilerParams(
            dimension_semantics=("arbitrary", "arbitrary"), vmem_limit_bytes=VMEM_LIMIT_BYTES),
        name="mixer_cd",
    )(z, z, z, z, z, a_re_log.reshape(1, nstate), a_im.reshape(1, nstate), ldt, bre, bim, cre, cim,
      d.reshape(1, s5w), glu_w.astype(BF16), glu_b.reshape(1, s5w), hg_gn.reshape(1, hw), lb_logits)


def _pack_lru_gates(wa, wi):
    nblk, bd, _ = wa.shape
    per = LRU_PAIR // bd
    wa_bd = _block_diag(wa.reshape(nblk // per, per, bd, bd))
    wi_bd = _block_diag(wi.reshape(nblk // per, per, bd, bd))
    return jnp.concatenate([wa_bd, wi_bd], axis=2).astype(BF16)


def _tiles(rows_per_batch, n):
    tm = _largest_divisor(rows_per_batch, 1040)
    tn = max(t for t in (256, 512, 1024, 1280) if n % t == 0)
    return tm, tn


def kernel(x, meta, w_in_ab, w_out_ab, ret_gn, rg_wa, rg_ba, rg_wi, rg_bi, rg_lam, rg_conv_w, rg_conv_b,
           w_in_cd, w_out_cd, s5_a_re_log, s5_a_im, s5_b_re, s5_b_im, s5_c_re, s5_c_im, s5_d, s5_log_dt,
           s5_glu_w, s5_glu_b, hg_gn, hg_lb_logits, norm_g, mlp_w1, mlp_w2):
    batch, seq, d = x.shape
    depth = norm_g.shape[0]
    lp = PAD + N_META + seq
    assert lp % CHUNK == 0
    chunks = _largest_divisor(lp // CHUNK, 5)
    m = batch * lp

    h = jnp.concatenate([jnp.zeros((batch, PAD, d), x.dtype),
                         jnp.broadcast_to(meta[None].astype(x.dtype), (batch, N_META, d)), x], axis=1)
    h = h.reshape(m, d)

    for l in range(depth):
        jdx = l // 2
        if l % 2 == 0:
            w_in, w_out = w_in_ab[jdx], w_out_ab[jdx]
        else:
            w_in, w_out = w_in_cd[jdx], w_out_cd[jdx]
        tm, tn = _tiles(lp, w_in.shape[1])
        z = _norm_matmul(h, norm_g[l, 0], w_in.astype(BF16), tm=tm, tn=tn, relu2=False, out_dtype=F32)
        if l % 2 == 0:
            y = _mixer_ab(z, ret_gn[jdx], _pack_lru_gates(rg_wa[jdx], rg_wi[jdx]), rg_ba[jdx], rg_bi[jdx],
                          rg_lam[jdx], rg_conv_w[jdx], rg_conv_b[jdx], batch=batch, chunks=chunks)
        else:
            y = _mixer_cd(z, s5_a_re_log[jdx], s5_a_im[jdx], s5_b_re[jdx], s5_b_im[jdx], s5_c_re[jdx],
                          s5_c_im[jdx], s5_d[jdx], s5_log_dt[jdx], s5_glu_w[jdx], s5_glu_b[jdx], hg_gn[jdx],
                          hg_lb_logits, batch=batch, chunks=chunks, layer=l)
        tk = max(t for t in (256, 512, 1024) if w_out.shape[0] % t == 0)
        h = _matmul_norm_res(y, w_out.astype(BF16), norm_g[l, 1], h, tm=tm, tk=tk, rows_per_batch=lp)
        tm, tn = _tiles(lp, mlp_w1.shape[2])
        a = _norm_matmul(h, norm_g[l, 2], mlp_w1[l].astype(BF16), tm=tm, tn=tn, relu2=True, out_dtype=BF16)
        h = _matmul_norm_res(a, mlp_w2[l].astype(BF16), norm_g[l, 3], h, tm=tm, tk=1024, rows_per_batch=lp)

    return h.reshape(batch, lp, d)[:, PAD + N_META:]
```

```python
import functools
import math

import jax
import jax.numpy as jnp
from jax import lax
from jax.experimental import pallas as pl
from jax.experimental.pallas import tpu as pltpu

F32 = jnp.float32
BF16 = jnp.bfloat16

CHUNK = 64
N_META = 16
PAD = CHUNK - N_META
EPS = 1e-6

RET_HEADS = 4
RET_DK = 128
RET_DV = 256
ROPE_BASE = 10000.0
LRU_C = 8.0
CONV_WIDTH = 4
S5_GROUP = 16
S5_STATE = 64
S5_LANES = 512
HG_HEADS = 4
HG_DK = 128
LANES = 128
SUBLANES = 8
SLAB_PITCH = CHUNK + SUBLANES
VMEM_LIMIT_BYTES = 56 * 1024 * 1024


def _largest_divisor(n, cap):
    return max(d for d in range(1, cap + 1) if n % d == 0)


def _gelu_tanh(x):
    return 0.5 * x * (1.0 + jnp.tanh(0.7978845608028654 * (x + 0.044715 * x * x * x)))


def _dot(a, b):
    return jnp.dot(a, b, preferred_element_type=F32)


def _dot_nt(a, b):
    return lax.dot_general(a, b, (((1,), (1,)), ((), ())), preferred_element_type=F32)


def _norm_matmul_kernel(x_ref, g_ref, w_ref, o_ref, hn_ref, *, relu2):
    @pl.when(pl.program_id(1) == 0)
    def _():
        x = x_ref[...]
        ms = jnp.mean(x * x, axis=-1, keepdims=True)
        hn_ref[...] = (x * lax.rsqrt(ms + EPS) * g_ref[...]).astype(BF16)

    acc = _dot(hn_ref[...], w_ref[...])
    if relu2:
        acc = jnp.maximum(acc, 0.0)
        acc = acc * acc
    o_ref[...] = acc.astype(o_ref.dtype)


def _norm_matmul(h, g, w, *, tm, tn, relu2, out_dtype):
    m, d = h.shape
    n = w.shape[1]
    return pl.pallas_call(
        functools.partial(_norm_matmul_kernel, relu2=relu2),
        grid=(m // tm, n // tn),
        in_specs=[
            pl.BlockSpec((tm, d), lambda i, j: (i, 0)),
            pl.BlockSpec((1, d), lambda i, j: (0, 0)),
            pl.BlockSpec((d, tn), lambda i, j: (0, j)),
        ],
        out_specs=pl.BlockSpec((tm, tn), lambda i, j: (i, j)),
        out_shape=jax.ShapeDtypeStruct((m, n), out_dtype),
        scratch_shapes=[pltpu.VMEM((tm, d), BF16)],
        compiler_params=pltpu.CompilerParams(
            dimension_semantics=("arbitrary", "arbitrary"), vmem_limit_bytes=VMEM_LIMIT_BYTES),
        name="norm_matmul",
    )(h, g.reshape(1, d), w)


def _matmul_norm_res_kernel(a_ref, w_ref, g_ref, h_ref, o_ref, acc_ref, *, tiles_per_batch):
    k = pl.program_id(1)

    @pl.when(k == 0)
    def _():
        acc_ref[...] = jnp.zeros_like(acc_ref)

    acc_ref[...] += _dot(a_ref[...].astype(BF16), w_ref[...])

    @pl.when(k == pl.num_programs(1) - 1)
    def _():
        m = acc_ref[...]
        ms = jnp.mean(m * m, axis=-1, keepdims=True)
        out = h_ref[...] + m * lax.rsqrt(ms + EPS) * g_ref[...]
        first_tile = lax.rem(pl.program_id(0), tiles_per_batch) == 0
        row = lax.broadcasted_iota(jnp.int32, (out.shape[0], 1), 0)
        keep = jnp.logical_or(row >= PAD, jnp.logical_not(first_tile))
        o_ref[...] = jnp.where(keep, out, 0.0)


def _matmul_norm_res(a, w, g, h, *, tm, tk, rows_per_batch):
    m, kdim = a.shape
    d = w.shape[1]
    return pl.pallas_call(
        functools.partial(_matmul_norm_res_kernel, tiles_per_batch=rows_per_batch // tm),
        grid=(m // tm, kdim // tk),
        in_specs=[
            pl.BlockSpec((tm, tk), lambda i, k: (i, k)),
            pl.BlockSpec((tk, d), lambda i, k: (k, 0)),
            pl.BlockSpec((1, d), lambda i, k: (0, 0)),
            pl.BlockSpec((tm, d), lambda i, k: (i, 0)),
        ],
        out_specs=pl.BlockSpec((tm, d), lambda i, k: (i, 0)),
        out_shape=jax.ShapeDtypeStruct((m, d), F32),
        scratch_shapes=[pltpu.VMEM((tm, d), F32)],
        compiler_params=pltpu.CompilerParams(
            dimension_semantics=("arbitrary", "arbitrary"), vmem_limit_bytes=VMEM_LIMIT_BYTES),
        name="matmul_norm_res",
    )(a, w, g.reshape(1, d), h)


def _mixer_ab_kernel(q_ref, k_ref, v_ref, gate_ref, bx_ref, bg_ref, gn_ref, wg_ref, ba_ref, bi_ref,
                     lam_ref, cw_ref, cb_ref, y_ref, s_ref, xcar_ref, hcar_ref, a_slab, b_slab):
    c = pl.program_id(0)
    nbatch, _, width = bx_ref.shape
    ngrp = width // LANES

    @pl.when(c == 0)
    def _():
        s_ref[...] = jnp.zeros_like(s_ref)
        xcar_ref[...] = jnp.zeros_like(xcar_ref)
        hcar_ref[...] = jnp.zeros_like(hcar_ref)

    row = lax.broadcasted_iota(jnp.int32, (CHUNK, 1), 0)
    idx = row.astype(F32)
    pos = (c * CHUNK + row - PAD).astype(F32)
    lane = lax.broadcasted_iota(jnp.int32, (1, RET_DK), 1)
    half = RET_DK // 2
    freq = jnp.exp((lane & (half - 1)).astype(F32) * (-math.log(ROPE_BASE) / half))
    ang = pos * freq
    cosv = jnp.cos(ang)
    sinv = jnp.where(lane < half, -1.0, 1.0) * jnp.sin(ang)
    ti = lax.broadcasted_iota(jnp.int32, (CHUNK, CHUNK), 0)
    si = lax.broadcasted_iota(jnp.int32, (CHUNK, CHUNK), 1)
    dist = jnp.abs(ti - si).astype(F32)

    def ret_body(b, carry):
        for h in range(RET_HEADS):
            log_g = math.log1p(-(2.0 ** (-5.0 - h)))
            qs = slice(h * RET_DK, (h + 1) * RET_DK)
            vs = slice(h * RET_DV, (h + 1) * RET_DV)
            qh = q_ref[b, :, qs]
            kh = k_ref[b, :, qs]
            qr = (qh * cosv + pltpu.roll(qh, half, 1) * sinv) * (RET_DK ** -0.5)
            kr = kh * cosv + pltpu.roll(kh, half, 1) * sinv
            vh = v_ref[b, :, vs].astype(BF16)
            scores = _dot_nt(qr.astype(BF16), kr.astype(BF16)) * jnp.exp(dist * log_g)
            o = _dot(scores.astype(BF16), vh)
            q_dec = qr * jnp.exp((idx + 1.0) * log_g)
            o = o + _dot(q_dec.astype(BF16), s_ref[b, h].astype(BF16))
            k_dec = kr * jnp.exp((CHUNK - 1.0 - idx) * log_g)
            kv = _dot(k_dec.T.astype(BF16), vh)
            s_ref[b, h] = math.exp(CHUNK * log_g) * s_ref[b, h] + kv
            oc = o - jnp.mean(o, axis=-1, keepdims=True)
            var = jnp.mean(oc * oc, axis=-1, keepdims=True)
            gt = gate_ref[b, :, vs]
            y_ref[b, :, vs] = oc * lax.rsqrt(var + EPS) * gn_ref[:, vs] * (gt * jax.nn.sigmoid(gt))
        return carry

    lax.fori_loop(0, nbatch, ret_body, 0)

    xb = bx_ref[...]
    xe = jnp.concatenate([xcar_ref[...], xb], axis=1)
    xc = cb_ref[...] + xb * cw_ref[CONV_WIDTH - 1:CONV_WIDTH, :]
    for s in range(1, CONV_WIDTH):
        xc = xc + pltpu.roll(xe, s, 1)[:, SUBLANES:, :] * cw_ref[CONV_WIDTH - 1 - s:CONV_WIDTH - s, :]
    xcar_ref[...] = xb[:, CHUNK - SUBLANES:, :]

    xc2 = xc.reshape(nbatch * CHUNK, width)
    xcb = xc2.astype(BF16)
    valid = jnp.logical_or(row >= PAD, c > 0)
    for p in range(ngrp):
        cs = slice(p * LANES, (p + 1) * LANES)
        g2 = _dot(xcb[:, cs], wg_ref[p])
        r = jax.nn.sigmoid(g2[:, :LANES] + ba_ref[:, cs])
        i = jax.nn.sigmoid(g2[:, LANES:] + bi_ref[:, cs])
        lam = lam_ref[:, cs]
        softplus_neg_lam = jnp.maximum(-lam, 0.0) + jnp.log1p(jnp.exp(-jnp.abs(lam)))
        a = jnp.exp(-LRU_C * r * softplus_neg_lam)
        bb = jnp.sqrt(1.0 - a * a) * (i * xc2[:, cs])
        for b in range(nbatch):
            rs = slice(b * CHUNK, (b + 1) * CHUNK)
            a_slab[b, p * SLAB_PITCH:p * SLAB_PITCH + CHUNK, :] = a[rs]
            b_slab[b, p * SLAB_PITCH:p * SLAB_PITCH + CHUNK, :] = jnp.where(valid, bb[rs], 0.0)

    def step(t, hs):
        out = []
        for b in range(nbatch):
            ts = pl.ds(t, ngrp, stride=SLAB_PITCH)
            h = a_slab[b, ts, :] * hs[b] + b_slab[b, ts, :]
            b_slab[b, ts, :] = h
            out.append(h)
        return tuple(out)

    hs = lax.fori_loop(0, CHUNK, step, tuple(hcar_ref[b] for b in range(nbatch)), unroll=8)
    for b in range(nbatch):
        hcar_ref[b] = hs[b]
        hfull = jnp.concatenate(
            [b_slab[b, p * SLAB_PITCH:p * SLAB_PITCH + CHUNK, :] for p in range(ngrp)], axis=1)
        y_ref[b, :, RET_HEADS * RET_DV:] = _gelu_tanh(bg_ref[b]) * hfull


def _mixer_ab(z3, ret_gn, wg, ba, bi, lam, conv_w, conv_b):
    batch, lp, _ = z3.shape
    qk = RET_HEADS * RET_DK
    vw = RET_HEADS * RET_DV
    lw = lam.shape[-1]
    assert vw == 2 * qk and lw == vw and lw == SUBLANES * LANES

    def zspec(width, col):
        return pl.BlockSpec((batch, CHUNK, width), lambda c: (0, c, col))

    def pspec(shape):
        return pl.BlockSpec(shape, lambda c: (0,) * len(shape))

    return pl.pallas_call(
        _mixer_ab_kernel,
        grid=(lp // CHUNK,),
        in_specs=[
            zspec(qk, 0), zspec(qk, 1), zspec(vw, 1), zspec(vw, 2), zspec(lw, 3), zspec(lw, 4),
            pspec((1, vw)), pspec(wg.shape), pspec((1, lw)), pspec((1, lw)), pspec((1, lw)),
            pspec((CONV_WIDTH, lw)), pspec((1, lw)),
        ],
        out_specs=pl.BlockSpec((batch, CHUNK, vw + lw), lambda c: (0, c, 0)),
        out_shape=jax.ShapeDtypeStruct((batch, lp, vw + lw), F32),
        scratch_shapes=[
            pltpu.VMEM((batch, RET_HEADS, RET_DK, RET_DV), F32),
            pltpu.VMEM((batch, SUBLANES, lw), F32),
            pltpu.VMEM((batch, SUBLANES, LANES), F32),
            pltpu.VMEM((batch, SUBLANES * SLAB_PITCH, LANES), F32),
            pltpu.VMEM((batch, SUBLANES * SLAB_PITCH, LANES), F32),
        ],
        compiler_params=pltpu.CompilerParams(
            dimension_semantics=("arbitrary",), vmem_limit_bytes=VMEM_LIMIT_BYTES),
        name="mixer_ab",
    )(z3, z3, z3, z3, z3, z3, ret_gn.reshape(1, vw), wg, ba.reshape(1, lw), bi.reshape(1, lw),
      lam.reshape(1, lw), conv_w, conv_b.reshape(1, lw))


_HG_LEVELS = (8, 16, 32)


def _mixer_cd_kernel(u_ref, q_ref, f_ref, i_ref, g_ref,
                     arl_ref, aim_ref, ldt_ref, arl_t_ref, aim_t_ref, ldt_t_ref,
                     bre_ref, bim_ref, cre_ref, cim_ref, d_ref, gw_ref, gb_ref, gn_ref, lbl_ref, y_ref,
                     bm_ref, ab_ref, hcar_ref, bu_slab, sel_ref, st_ref, *, layer):
    c = pl.program_id(0)
    nbatch, _, s5w = u_ref.shape
    ncol = bre_ref.shape[0]
    cin = bre_ref.shape[1]
    nslab = ab_ref.shape[1]
    nhalf = nslab // SUBLANES
    gpc = S5_LANES // LANES
    hw = HG_HEADS * HG_DK
    pair = 2 * HG_DK

    @pl.when(c == 0)
    def _():
        def disc(ldt, arl, aim):
            dt = jnp.exp(ldt)
            a_re = -jnp.exp(arl)
            mag = jnp.exp(dt * a_re)
            return a_re, aim, mag * jnp.cos(dt * aim), mag * jnp.sin(dt * aim)

        _, _, t_re, t_im = disc(ldt_t_ref[...], arl_t_ref[...], aim_t_ref[...])
        ab_ref[0] = t_re
        ab_ref[1] = t_im
        a_re, a_im, ab_re, ab_im = disc(ldt_ref[...], arl_ref[...], aim_ref[...])
        den = a_re * a_re + a_im * a_im
        z_re = ((ab_re - 1.0) * a_re + ab_im * a_im) / den
        z_im = (ab_im * a_re - (ab_re - 1.0) * a_im) / den
        for jb in range(ncol):
            ls = slice(jb * S5_LANES, (jb + 1) * S5_LANES)
            zr = z_re[:, ls]
            zi = z_im[:, ls]
            bb_re = zr * bre_ref[jb] - zi * bim_ref[jb]
            bb_im = zr * bim_ref[jb] + zi * bre_ref[jb]
            bm_ref[jb] = jnp.concatenate([bb_re, bb_im], axis=1).astype(BF16)
        ri = lax.broadcasted_iota(jnp.int32, (pair, 2 * CHUNK), 0)
        ci = lax.broadcasted_iota(jnp.int32, (pair, 2 * CHUNK), 1)
        same_head = (ri >= HG_DK) == (ci >= CHUNK)
        for s in range(SUBLANES):
            sel_ref[s] = jnp.logical_and(same_head, (ci & (SUBLANES - 1)) == s).astype(BF16)
        hcar_ref[...] = jnp.zeros_like(hcar_ref)
        st_ref[...] = jnp.zeros_like(st_ref)

    u2 = u_ref[...].reshape(nbatch * CHUNK, s5w)
    ub = u2.astype(BF16)
    for jb in range(ncol):
        bu = _dot(ub[:, jb * cin:(jb + 1) * cin], bm_ref[jb])
        for part in range(2):
            for gi in range(gpc):
                slab = part * nslab + jb * gpc + gi
                col = part * S5_LANES + gi * LANES
                for b in range(nbatch):
                    bu_slab[b, slab * SLAB_PITCH:slab * SLAB_PITCH + CHUNK, :] = (
                        bu[b * CHUNK:(b + 1) * CHUNK, col:col + LANES])

    a_re = [ab_ref[0, hf * SUBLANES:(hf + 1) * SUBLANES, :] for hf in range(nhalf)]
    a_im = [ab_ref[1, hf * SUBLANES:(hf + 1) * SUBLANES, :] for hf in range(nhalf)]

    def step(t, hs):
        out = []
        for b in range(nbatch):
            for hf in range(nhalf):
                h_re, h_im = hs[2 * (b * nhalf + hf)], hs[2 * (b * nhalf + hf) + 1]
                ts_re = pl.ds(hf * SUBLANES * SLAB_PITCH + t, SUBLANES, stride=SLAB_PITCH)
                ts_im = pl.ds((nslab + hf * SUBLANES) * SLAB_PITCH + t, SUBLANES, stride=SLAB_PITCH)
                n_re = a_re[hf] * h_re - a_im[hf] * h_im + bu_slab[b, ts_re, :]
                n_im = a_re[hf] * h_im + a_im[hf] * h_re + bu_slab[b, ts_im, :]
                bu_slab[b, ts_re, :] = n_re
                bu_slab[b, ts_im, :] = n_im
                out += [n_re, n_im]
        return tuple(out)

    init = []
    for b in range(nbatch):
        for hf in range(nhalf):
            init += [hcar_ref[b, 0, hf], hcar_ref[b, 1, hf]]
    hs = lax.fori_loop(0, CHUNK, step, tuple(init), unroll=4)
    for b in range(nbatch):
        for hf in range(nhalf):
            hcar_ref[b, 0, hf] = hs[2 * (b * nhalf + hf)]
            hcar_ref[b, 1, hf] = hs[2 * (b * nhalf + hf) + 1]

    def states(part, jb):
        return jnp.concatenate(
            [jnp.concatenate(
                [bu_slab[b, (part * nslab + jb * gpc + gi) * SLAB_PITCH:
                         (part * nslab + jb * gpc + gi) * SLAB_PITCH + CHUNK, :] for gi in range(gpc)], axis=1)
             for b in range(nbatch)], axis=0).astype(BF16)

    ys = []
    for jb in range(ncol):
        cs = slice(jb * cin, (jb + 1) * cin)
        y = _dot(states(0, jb), cre_ref[jb]) - _dot(states(1, jb), cim_ref[jb]) + d_ref[:, cs] * u2[:, cs]
        ys.append(_gelu_tanh(y))
    yg = jnp.concatenate(ys, axis=1)
    yc = yg * jax.nn.sigmoid(_dot(yg.astype(BF16), gw_ref[...]) + gb_ref[...])
    y_ref[:, :, :s5w] = yc.reshape(nbatch, CHUNK, s5w)

    logits = lbl_ref[...]
    pexp = jnp.exp(logits - jnp.max(logits, axis=0, keepdims=True))
    psm = pexp / jnp.sum(pexp, axis=0, keepdims=True)
    lb = jnp.zeros_like(psm[0:1, :])
    for l in range(layer):
        lb = lb + psm[l:l + 1, :]

    ti = lax.broadcasted_iota(jnp.int32, (CHUNK, CHUNK), 0)
    si = lax.broadcasted_iota(jnp.int32, (CHUNK, CHUNK), 1)
    tril = (ti >= si).astype(BF16)
    tp = lax.broadcasted_iota(jnp.int32, (CHUNK, 2 * CHUNK), 0)
    sp = lax.broadcasted_iota(jnp.int32, (CHUNK, 2 * CHUNK), 1) & (CHUNK - 1)
    diag_mask = jnp.logical_and((tp >> 3) == (sp >> 3), sp <= tp)
    nvr = CHUNK // SUBLANES

    def both_heads(x, p):
        xa = x[:, p * pair:p * pair + HG_DK]
        xb_ = x[:, p * pair + HG_DK:(p + 1) * pair]
        zero = jnp.zeros_like(xa)
        return jnp.concatenate([jnp.concatenate([xa, zero], axis=1),
                                jnp.concatenate([zero, xb_], axis=1)], axis=0)

    def hg_body(b, carry):
        f = lb + (1.0 - lb) * jax.nn.sigmoid(f_ref[b])
        logf = jnp.log(f)
        kk = 1.0 - f
        lf_hi = logf.astype(BF16)
        lf_lo = (logf - lf_hi.astype(F32)).astype(BF16)
        cum = _dot(tril, lf_hi) + _dot(tril, lf_lo)
        total = cum[CHUNK - 1:CHUNK, :]
        q = q_ref[b]
        iv = i_ref[b]
        ivb = iv.astype(BF16)
        q_in = (q * jnp.exp(cum)).astype(BF16)
        k_dec = kk * jnp.exp(total - cum)
        dec = jnp.exp(total)

        att = [jnp.zeros((CHUNK, 2 * CHUNK), F32) for _ in range(HG_HEADS // 2)]
        ends = [cum[v * SUBLANES + SUBLANES - 1:(v + 1) * SUBLANES, :] for v in range(nvr)]
        zeros8 = jnp.zeros((SUBLANES, hw), F32)
        for n in _HG_LEVELS:
            per = n // SUBLANES
            qparts, kparts = [], []
            for v in range(nvr):
                blk = v // per
                vs = slice(v * SUBLANES, (v + 1) * SUBLANES)
                if blk % 2 == 1:
                    qparts.append(q[vs] * jnp.exp(cum[vs] - ends[blk * per - 1]))
                    kparts.append(zeros8)
                else:
                    qparts.append(zeros8)
                    kparts.append(kk[vs] * jnp.exp(ends[blk * per + per - 1] - cum[vs]))
            q_t = jnp.concatenate(qparts, axis=0).astype(BF16)
            k_t = jnp.concatenate(kparts, axis=0).astype(BF16)
            shift = n.bit_length()
            for p in range(HG_HEADS // 2):
                a_n = _dot_nt(q_t[:, p * pair:(p + 1) * pair], both_heads(k_t, p))
                if 2 * n < CHUNK:
                    a_n = jnp.where((tp >> shift) == (sp >> shift), a_n, 0.0)
                att[p] = att[p] + a_n
        q3 = q.reshape(nvr, SUBLANES, hw)
        k3 = kk.reshape(nvr, SUBLANES, hw)
        c3 = cum.reshape(nvr, SUBLANES, hw)
        dsum = [jnp.zeros((CHUNK, 2 * CHUNK), F32) for _ in range(HG_HEADS // 2)]
        for s in range(SUBLANES):
            w = q3 * k3[:, s:s + 1, :] * jnp.exp(jnp.minimum(c3 - c3[:, s:s + 1, :], 0.0))
            wb = w.reshape(CHUNK, hw).astype(BF16)
            for p in range(HG_HEADS // 2):
                dsum[p] = dsum[p] + _dot(wb[:, p * pair:(p + 1) * pair], sel_ref[s])
        for p in range(HG_HEADS // 2):
            a_all = (att[p] + jnp.where(diag_mask, dsum[p], 0.0)).astype(BF16)
            o_pair = _dot(a_all, both_heads(ivb, p))
            for hh in range(2):
                h = 2 * p + hh
                hs_ = slice(h * HG_DK, (h + 1) * HG_DK)
                st = st_ref[b, h]
                oh = o_pair[:, hh * HG_DK:(hh + 1) * HG_DK] + _dot_nt(q_in[:, hs_], st.astype(BF16))
                st_ref[b, h] = dec[:, hs_] * st + _dot(iv[:, hs_].T.astype(BF16), k_dec[:, hs_].astype(BF16))
                ms = jnp.mean(oh * oh, axis=-1, keepdims=True)
                gt = g_ref[b, :, hs_]
                y_ref[b, :, s5w + h * HG_DK:s5w + (h + 1) * HG_DK] = (
                    oh * lax.rsqrt(ms + EPS) * gn_ref[:, hs_] * (gt * jax.nn.sigmoid(gt)))
        return carry

    lax.fori_loop(0, nbatch, hg_body, 0)


def _block_diag(t):
    n, g, r, c = t.shape
    out = jnp.zeros((n, g, r, g, c), t.dtype)
    for gi in range(g):
        out = out.at[:, gi, :, gi, :].set(t[:, gi])
    return out.reshape(n, g * r, g * c)


def _mixer_cd(z3, a_re_log, a_im, b_re, b_im, c_re, c_im, d, log_dt, glu_w, glu_b, hg_gn, lb_logits,
              *, layer):
    batch, lp, _ = z3.shape
    groups, state = a_re_log.shape
    s5w = groups * S5_GROUP
    nstate = groups * state
    gpc = S5_LANES // state
    ncol = groups // gpc
    cin = gpc * S5_GROUP
    hw = HG_HEADS * HG_DK
    nslab = nstate // LANES
    assert s5w == hw and nslab % SUBLANES == 0

    bre = _block_diag(jnp.transpose(b_re.reshape(ncol, gpc, state, S5_GROUP), (0, 1, 3, 2)))
    bim = _block_diag(jnp.transpose(b_im.reshape(ncol, gpc, state, S5_GROUP), (0, 1, 3, 2)))
    cre = _block_diag(jnp.transpose(c_re.reshape(ncol, gpc, S5_GROUP, state), (0, 1, 3, 2))).astype(BF16)
    cim = _block_diag(jnp.transpose(c_im.reshape(ncol, gpc, S5_GROUP, state), (0, 1, 3, 2))).astype(BF16)
    ldt = jnp.repeat(log_dt, state)

    def zspec(col):
        return pl.BlockSpec((batch, CHUNK, s5w), lambda c: (0, c, col))

    def pspec(shape):
        return pl.BlockSpec(shape, lambda c: (0,) * len(shape))

    return pl.pallas_call(
        functools.partial(_mixer_cd_kernel, layer=layer),
        grid=(lp // CHUNK,),
        in_specs=[
            zspec(0), zspec(1), zspec(2), zspec(3), zspec(4),
            pspec((1, nstate)), pspec((1, nstate)), pspec((1, nstate)),
            pspec((nslab, LANES)), pspec((nslab, LANES)), pspec((nslab, LANES)),
            pspec(bre.shape), pspec(bim.shape), pspec(cre.shape), pspec(cim.shape), pspec((1, s5w)),
            pspec((s5w, s5w)), pspec((1, s5w)), pspec((1, hw)), pspec(lb_logits.shape),
        ],
        out_specs=pl.BlockSpec((batch, CHUNK, s5w + hw), lambda c: (0, c, 0)),
        out_shape=jax.ShapeDtypeStruct((batch, lp, s5w + hw), F32),
        scratch_shapes=[
            pltpu.VMEM((ncol, cin, 2 * S5_LANES), BF16),
            pltpu.VMEM((2, nslab, LANES), F32),
            pltpu.VMEM((batch, 2, nslab // SUBLANES, SUBLANES, LANES), F32),
            pltpu.VMEM((batch, 2 * nslab * SLAB_PITCH, LANES), F32),
            pltpu.VMEM((SUBLANES, 2 * HG_DK, 2 * CHUNK), BF16),
            pltpu.VMEM((batch, HG_HEADS, HG_DK, HG_DK), F32),
        ],
        compiler_params=pltpu.CompilerParams(
            dimension_semantics=("arbitrary",), vmem_limit_bytes=VMEM_LIMIT_BYTES),
        name="mixer_cd",
    )(z3, z3, z3, z3, z3,
      a_re_log.reshape(1, nstate), a_im.reshape(1, nstate), ldt.reshape(1, nstate),
      a_re_log.reshape(nslab, LANES), a_im.reshape(nslab, LANES), ldt.reshape(nslab, LANES),
      bre, bim, cre, cim, d.reshape(1, s5w), glu_w.astype(BF16), glu_b.reshape(1, s5w),
      hg_gn.reshape(1, hw), lb_logits)


def _pack_lru_gates(wa, wi):
    nblk, bd, _ = wa.shape
    per = LANES // bd
    wa_bd = _block_diag(wa.reshape(nblk // per, per, bd, bd))
    wi_bd = _block_diag(wi.reshape(nblk // per, per, bd, bd))
    return jnp.concatenate([wa_bd, wi_bd], axis=2).astype(BF16)


def _tiles(rows_per_batch, n):
    tm = _largest_divisor(rows_per_batch, 1040)
    tn = max(t for t in (256, 512, 1024, 1280) if n % t == 0)
    return tm, tn


def kernel(x, meta, w_in_ab, w_out_ab, ret_gn, rg_wa, rg_ba, rg_wi, rg_bi, rg_lam, rg_conv_w, rg_conv_b,
           w_in_cd, w_out_cd, s5_a_re_log, s5_a_im, s5_b_re, s5_b_im, s5_c_re, s5_c_im, s5_d, s5_log_dt,
           s5_glu_w, s5_glu_b, hg_gn, hg_lb_logits, norm_g, mlp_w1, mlp_w2):
    batch, seq, d = x.shape
    depth = norm_g.shape[0]
    lp = PAD + N_META + seq
    assert lp % CHUNK == 0
    m = batch * lp

    h = jnp.concatenate([jnp.zeros((batch, PAD, d), x.dtype),
                         jnp.broadcast_to(meta[None].astype(x.dtype), (batch, N_META, d)), x], axis=1)
    h = h.reshape(m, d)

    for l in range(depth):
        jdx = l // 2
        if l % 2 == 0:
            w_in, w_out = w_in_ab[jdx], w_out_ab[jdx]
        else:
            w_in, w_out = w_in_cd[jdx], w_out_cd[jdx]
        tm, tn = _tiles(lp, w_in.shape[1])
        z = _norm_matmul(h, norm_g[l, 0], w_in.astype(BF16), tm=tm, tn=tn, relu2=False, out_dtype=F32)
        z3 = z.reshape(batch, lp, -1)
        if l % 2 == 0:
            y = _mixer_ab(z3, ret_gn[jdx], _pack_lru_gates(rg_wa[jdx], rg_wi[jdx]), rg_ba[jdx], rg_bi[jdx],
                          rg_lam[jdx], rg_conv_w[jdx], rg_conv_b[jdx])
        else:
            y = _mixer_cd(z3, s5_a_re_log[jdx], s5_a_im[jdx], s5_b_re[jdx], s5_b_im[jdx], s5_c_re[jdx],
                          s5_c_im[jdx], s5_d[jdx], s5_log_dt[jdx], s5_glu_w[jdx], s5_glu_b[jdx], hg_gn[jdx],
                          hg_lb_logits, layer=l)
        y = y.reshape(m, -1)
        tk = max(t for t in (256, 512, 1024) if w_out.shape[0] % t == 0)
        h = _matmul_norm_res(y, w_out.astype(BF16), norm_g[l, 1], h, tm=tm, tk=tk, rows_per_batch=lp)
        tm, tn = _tiles(lp, mlp_w1.shape[2])
        a = _norm_matmul(h, norm_g[l, 2], mlp_w1[l].astype(BF16), tm=tm, tn=tn, relu2=True, out_dtype=BF16)
        h = _matmul_norm_res(a, mlp_w2[l].astype(BF16), norm_g[l, 3], h, tm=tm, tk=1024, rows_per_batch=lp)

    return h.reshape(batch, lp, d)[:, PAD + N_META:]
```

```python
import functools
import math

import jax
import jax.numpy as jnp
from jax import lax
from jax.experimental import pallas as pl
from jax.experimental.pallas import tpu as pltpu

F32 = jnp.float32
BF16 = jnp.bfloat16

CHUNK = 64
N_META = 16
PAD = CHUNK - N_META
EPS = 1e-6

RET_HEADS = 4
RET_DK = 128
RET_DV = 256
ROPE_BASE = 10000.0
LRU_C = 8.0
CONV_WIDTH = 4
S5_GROUP = 16
S5_STATE = 64
S5_LANES = 512
HG_HEADS = 4
HG_DK = 128
LANES = 128
SUBLANES = 8
SLAB_PITCH = CHUNK + SUBLANES
VMEM_LIMIT_BYTES = 56 * 1024 * 1024


def _largest_divisor(n, cap):
    return max(d for d in range(1, cap + 1) if n % d == 0)


def _gelu_tanh(x):
    return 0.5 * x * (1.0 + jnp.tanh(0.7978845608028654 * (x + 0.044715 * x * x * x)))


def _dot(a, b):
    return jnp.dot(a, b, preferred_element_type=F32)


def _dot_nt(a, b):
    return lax.dot_general(a, b, (((1,), (1,)), ((), ())), preferred_element_type=F32)


def _norm_matmul_kernel(x_ref, g_ref, w_ref, o_ref, hn_ref, *, relu2):
    @pl.when(pl.program_id(1) == 0)
    def _():
        x = x_ref[...]
        ms = jnp.mean(x * x, axis=-1, keepdims=True)
        hn_ref[...] = (x * lax.rsqrt(ms + EPS) * g_ref[...]).astype(BF16)

    acc = _dot(hn_ref[...], w_ref[...])
    if relu2:
        acc = jnp.maximum(acc, 0.0)
        acc = acc * acc
    o_ref[...] = acc.astype(o_ref.dtype)


def _norm_matmul(h, g, w, *, tm, tn, relu2, out_dtype):
    m, d = h.shape
    n = w.shape[1]
    return pl.pallas_call(
        functools.partial(_norm_matmul_kernel, relu2=relu2),
        grid=(m // tm, n // tn),
        in_specs=[
            pl.BlockSpec((tm, d), lambda i, j: (i, 0)),
            pl.BlockSpec((1, d), lambda i, j: (0, 0)),
            pl.BlockSpec((d, tn), lambda i, j: (0, j)),
        ],
        out_specs=pl.BlockSpec((tm, tn), lambda i, j: (i, j)),
        out_shape=jax.ShapeDtypeStruct((m, n), out_dtype),
        scratch_shapes=[pltpu.VMEM((tm, d), BF16)],
        compiler_params=pltpu.CompilerParams(
            dimension_semantics=("arbitrary", "arbitrary"), vmem_limit_bytes=VMEM_LIMIT_BYTES),
        name="norm_matmul",
    )(h, g.reshape(1, d), w)


def _matmul_norm_res_kernel(a_ref, w_ref, g_ref, h_ref, o_ref, acc_ref, *, tiles_per_batch):
    k = pl.program_id(1)

    @pl.when(k == 0)
    def _():
        acc_ref[...] = jnp.zeros_like(acc_ref)

    acc_ref[...] += _dot(a_ref[...].astype(BF16), w_ref[...])

    @pl.when(k == pl.num_programs(1) - 1)
    def _():
        m = acc_ref[...]
        ms = jnp.mean(m * m, axis=-1, keepdims=True)
        o_ref[...] = _keep_rows(h_ref[...] + m * lax.rsqrt(ms + EPS) * g_ref[...], tiles_per_batch)


def _matmul_norm_res(a, w, g, h, *, tm, tk, rows_per_batch):
    m, kdim = a.shape
    d = w.shape[1]
    return pl.pallas_call(
        functools.partial(_matmul_norm_res_kernel, tiles_per_batch=rows_per_batch // tm),
        grid=(m // tm, kdim // tk),
        in_specs=[
            pl.BlockSpec((tm, tk), lambda i, k: (i, k)),
            pl.BlockSpec((tk, d), lambda i, k: (k, 0)),
            pl.BlockSpec((1, d), lambda i, k: (0, 0)),
            pl.BlockSpec((tm, d), lambda i, k: (i, 0)),
        ],
        out_specs=pl.BlockSpec((tm, d), lambda i, k: (i, 0)),
        out_shape=jax.ShapeDtypeStruct((m, d), F32),
        scratch_shapes=[pltpu.VMEM((tm, d), F32)],
        compiler_params=pltpu.CompilerParams(
            dimension_semantics=("arbitrary", "arbitrary"), vmem_limit_bytes=VMEM_LIMIT_BYTES),
        name="matmul_norm_res",
    )(a, w, g.reshape(1, d), h)


def _keep_rows(out, tiles_per_batch):
    first_tile = lax.rem(pl.program_id(0), tiles_per_batch) == 0
    row = lax.broadcasted_iota(jnp.int32, (out.shape[0], 1), 0)
    keep = jnp.logical_or(row >= PAD, jnp.logical_not(first_tile))
    return jnp.where(keep, out, 0.0)


def _mlp_kernel(h_ref, g_in_ref, w1_ref, w2_ref, g_out_ref, o_ref, *, tiles_per_batch, tn):
    x = h_ref[...]
    ms = jnp.mean(x * x, axis=-1, keepdims=True)
    hn = (x * lax.rsqrt(ms + EPS) * g_in_ref[...]).astype(BF16)
    acc = jnp.zeros(x.shape, F32)
    for j in range(w1_ref.shape[1] // tn):
        a = jnp.maximum(_dot(hn, w1_ref[:, j * tn:(j + 1) * tn]), 0.0)
        acc = acc + _dot((a * a).astype(BF16), w2_ref[j * tn:(j + 1) * tn, :])
    ms = jnp.mean(acc * acc, axis=-1, keepdims=True)
    o_ref[...] = _keep_rows(x + acc * lax.rsqrt(ms + EPS) * g_out_ref[...], tiles_per_batch)


def _mlp(h, g_in, w1, w2, g_out, *, tm, tn, rows_per_batch):
    m, d = h.shape
    f = w1.shape[1]
    resident = pl.Buffered(1)
    return pl.pallas_call(
        functools.partial(_mlp_kernel, tiles_per_batch=rows_per_batch // tm, tn=tn),
        grid=(m // tm,),
        in_specs=[
            pl.BlockSpec((tm, d), lambda i: (i, 0)),
            pl.BlockSpec((1, d), lambda i: (0, 0)),
            pl.BlockSpec((d, f), lambda i: (0, 0), pipeline_mode=resident),
            pl.BlockSpec((f, d), lambda i: (0, 0), pipeline_mode=resident),
            pl.BlockSpec((1, d), lambda i: (0, 0)),
        ],
        out_specs=pl.BlockSpec((tm, d), lambda i: (i, 0)),
        out_shape=jax.ShapeDtypeStruct((m, d), F32),
        compiler_params=pltpu.CompilerParams(
            dimension_semantics=("arbitrary",), vmem_limit_bytes=VMEM_LIMIT_BYTES),
        name="mlp",
    )(h, g_in.reshape(1, d), w1, w2, g_out.reshape(1, d))


def _mixer_ab_kernel(q_ref, k_ref, v_ref, gate_ref, bx_ref, bg_ref, gn_ref, wg_ref, ba_ref, bi_ref,
                     lam_ref, cw_ref, cb_ref, y_ref, s_ref, xcar_ref, hcar_ref, a_slab, b_slab):
    c = pl.program_id(0)
    nbatch, _, width = bx_ref.shape
    ngrp = width // LANES

    @pl.when(c == 0)
    def _():
        s_ref[...] = jnp.zeros_like(s_ref)
        xcar_ref[...] = jnp.zeros_like(xcar_ref)
        hcar_ref[...] = jnp.zeros_like(hcar_ref)

    row = lax.broadcasted_iota(jnp.int32, (CHUNK, 1), 0)
    idx = row.astype(F32)
    pos = (c * CHUNK + row - PAD).astype(F32)
    lane = lax.broadcasted_iota(jnp.int32, (1, RET_DK), 1)
    half = RET_DK // 2
    freq = jnp.exp((lane & (half - 1)).astype(F32) * (-math.log(ROPE_BASE) / half))
    ang = pos * freq
    cosv = jnp.cos(ang)
    sinv = jnp.where(lane < half, -1.0, 1.0) * jnp.sin(ang)
    ti = lax.broadcasted_iota(jnp.int32, (CHUNK, CHUNK), 0)
    si = lax.broadcasted_iota(jnp.int32, (CHUNK, CHUNK), 1)
    dist = jnp.abs(ti - si).astype(F32)

    def ret_body(b, carry):
        for h in range(RET_HEADS):
            log_g = math.log1p(-(2.0 ** (-5.0 - h)))
            qs = slice(h * RET_DK, (h + 1) * RET_DK)
            vs = slice(h * RET_DV, (h + 1) * RET_DV)
            qh = q_ref[b, :, qs]
            kh = k_ref[b, :, qs]
            qr = (qh * cosv + pltpu.roll(qh, half, 1) * sinv) * (RET_DK ** -0.5)
            kr = kh * cosv + pltpu.roll(kh, half, 1) * sinv
            vh = v_ref[b, :, vs].astype(BF16)
            scores = _dot_nt(qr.astype(BF16), kr.astype(BF16)) * jnp.exp(dist * log_g)
            o = _dot(scores.astype(BF16), vh)
            q_dec = qr * jnp.exp((idx + 1.0) * log_g)
            o = o + _dot(q_dec.astype(BF16), s_ref[b, h].astype(BF16))
            k_dec = kr * jnp.exp((CHUNK - 1.0 - idx) * log_g)
            kv = _dot(k_dec.T.astype(BF16), vh)
            s_ref[b, h] = math.exp(CHUNK * log_g) * s_ref[b, h] + kv
            oc = o - jnp.mean(o, axis=-1, keepdims=True)
            var = jnp.mean(oc * oc, axis=-1, keepdims=True)
            gt = gate_ref[b, :, vs]
            y_ref[b, :, vs] = oc * lax.rsqrt(var + EPS) * gn_ref[:, vs] * (gt * jax.nn.sigmoid(gt))
        return carry

    lax.fori_loop(0, nbatch, ret_body, 0)

    xb = bx_ref[...]
    xe = jnp.concatenate([xcar_ref[...], xb], axis=1)
    xc = cb_ref[...] + xb * cw_ref[CONV_WIDTH - 1:CONV_WIDTH, :]
    for s in range(1, CONV_WIDTH):
        xc = xc + pltpu.roll(xe, s, 1)[:, SUBLANES:, :] * cw_ref[CONV_WIDTH - 1 - s:CONV_WIDTH - s, :]
    xcar_ref[...] = xb[:, CHUNK - SUBLANES:, :]

    xc2 = xc.reshape(nbatch * CHUNK, width)
    xcb = xc2.astype(BF16)
    valid = jnp.logical_or(row >= PAD, c > 0)
    for p in range(ngrp):
        cs = slice(p * LANES, (p + 1) * LANES)
        g2 = _dot(xcb[:, cs], wg_ref[p])
        r = jax.nn.sigmoid(g2[:, :LANES] + ba_ref[:, cs])
        i = jax.nn.sigmoid(g2[:, LANES:] + bi_ref[:, cs])
        lam = lam_ref[:, cs]
        softplus_neg_lam = jnp.maximum(-lam, 0.0) + jnp.log1p(jnp.exp(-jnp.abs(lam)))
        a = jnp.exp(-LRU_C * r * softplus_neg_lam)
        bb = jnp.sqrt(1.0 - a * a) * (i * xc2[:, cs])
        for b in range(nbatch):
            rs = slice(b * CHUNK, (b + 1) * CHUNK)
            a_slab[b, p * SLAB_PITCH:p * SLAB_PITCH + CHUNK, :] = a[rs]
            b_slab[b, p * SLAB_PITCH:p * SLAB_PITCH + CHUNK, :] = jnp.where(valid, bb[rs], 0.0)

    def step(t, hs):
        out = []
        for b in range(nbatch):
            ts = pl.ds(t, ngrp, stride=SLAB_PITCH)
            h = a_slab[b, ts, :] * hs[b] + b_slab[b, ts, :]
            b_slab[b, ts, :] = h
            out.append(h)
        return tuple(out)

    hs = lax.fori_loop(0, CHUNK, step, tuple(hcar_ref[b] for b in range(nbatch)), unroll=8)
    for b in range(nbatch):
        hcar_ref[b] = hs[b]
        hfull = jnp.concatenate(
            [b_slab[b, p * SLAB_PITCH:p * SLAB_PITCH + CHUNK, :] for p in range(ngrp)], axis=1)
        y_ref[b, :, RET_HEADS * RET_DV:] = _gelu_tanh(bg_ref[b]) * hfull


def _mixer_ab(z3, ret_gn, wg, ba, bi, lam, conv_w, conv_b):
    batch, lp, _ = z3.shape
    qk = RET_HEADS * RET_DK
    vw = RET_HEADS * RET_DV
    lw = lam.shape[-1]
    assert vw == 2 * qk and lw == vw and lw == SUBLANES * LANES

    def zspec(width, col):
        return pl.BlockSpec((batch, CHUNK, width), lambda c: (0, c, col))

    def pspec(shape):
        return pl.BlockSpec(shape, lambda c: (0,) * len(shape))

    return pl.pallas_call(
        _mixer_ab_kernel,
        grid=(lp // CHUNK,),
        in_specs=[
            zspec(qk, 0), zspec(qk, 1), zspec(vw, 1), zspec(vw, 2), zspec(lw, 3), zspec(lw, 4),
            pspec((1, vw)), pspec(wg.shape), pspec((1, lw)), pspec((1, lw)), pspec((1, lw)),
            pspec((CONV_WIDTH, lw)), pspec((1, lw)),
        ],
        out_specs=pl.BlockSpec((batch, CHUNK, vw + lw), lambda c: (0, c, 0)),
        out_shape=jax.ShapeDtypeStruct((batch, lp, vw + lw), F32),
        scratch_shapes=[
            pltpu.VMEM((batch, RET_HEADS, RET_DK, RET_DV), F32),
            pltpu.VMEM((batch, SUBLANES, lw), F32),
            pltpu.VMEM((batch, SUBLANES, LANES), F32),
            pltpu.VMEM((batch, SUBLANES * SLAB_PITCH, LANES), F32),
            pltpu.VMEM((batch, SUBLANES * SLAB_PITCH, LANES), F32),
        ],
        compiler_params=pltpu.CompilerParams(
            dimension_semantics=("arbitrary",), vmem_limit_bytes=VMEM_LIMIT_BYTES),
        name="mixer_ab",
    )(z3, z3, z3, z3, z3, z3, ret_gn.reshape(1, vw), wg, ba.reshape(1, lw), bi.reshape(1, lw),
      lam.reshape(1, lw), conv_w, conv_b.reshape(1, lw))


_HG_LEVELS = (8, 16, 32)


def _mixer_cd_kernel(u_ref, q_ref, f_ref, i_ref, g_ref,
                     arl_ref, aim_ref, ldt_ref, arl_t_ref, aim_t_ref, ldt_t_ref,
                     bre_ref, bim_ref, cre_ref, cim_ref, d_ref, gw_ref, gb_ref, gn_ref, lbl_ref, y_ref,
                     bm_ref, ab_ref, hcar_ref, bu_slab, sel_ref, st_ref, *, layer):
    c = pl.program_id(0)
    nbatch, _, s5w = u_ref.shape
    ncol = bre_ref.shape[0]
    cin = bre_ref.shape[1]
    nslab = ab_ref.shape[1]
    nhalf = nslab // SUBLANES
    gpc = S5_LANES // LANES
    hw = HG_HEADS * HG_DK
    pair = 2 * HG_DK

    @pl.when(c == 0)
    def _():
        def disc(ldt, arl, aim):
            dt = jnp.exp(ldt)
            a_re = -jnp.exp(arl)
            mag = jnp.exp(dt * a_re)
            return a_re, aim, mag * jnp.cos(dt * aim), mag * jnp.sin(dt * aim)

        _, _, t_re, t_im = disc(ldt_t_ref[...], arl_t_ref[...], aim_t_ref[...])
        ab_ref[0] = t_re
        ab_ref[1] = t_im
        a_re, a_im, ab_re, ab_im = disc(ldt_ref[...], arl_ref[...], aim_ref[...])
        den = a_re * a_re + a_im * a_im
        z_re = ((ab_re - 1.0) * a_re + ab_im * a_im) / den
        z_im = (ab_im * a_re - (ab_re - 1.0) * a_im) / den
        for jb in range(ncol):
            ls = slice(jb * S5_LANES, (jb + 1) * S5_LANES)
            zr = z_re[:, ls]
            zi = z_im[:, ls]
            bb_re = zr * bre_ref[jb] - zi * bim_ref[jb]
            bb_im = zr * bim_ref[jb] + zi * bre_ref[jb]
            bm_ref[jb] = jnp.concatenate([bb_re, bb_im], axis=1).astype(BF16)
        ri = lax.broadcasted_iota(jnp.int32, (pair, 2 * CHUNK), 0)
        ci = lax.broadcasted_iota(jnp.int32, (pair, 2 * CHUNK), 1)
        same_head = (ri >= HG_DK) == (ci >= CHUNK)
        for s in range(SUBLANES):
            sel_ref[s] = jnp.logical_and(same_head, (ci & (SUBLANES - 1)) == s).astype(BF16)
        hcar_ref[...] = jnp.zeros_like(hcar_ref)
        st_ref[...] = jnp.zeros_like(st_ref)

    u2 = u_ref[...].reshape(nbatch * CHUNK, s5w)
    ub = u2.astype(BF16)
    for jb in range(ncol):
        bu = _dot(ub[:, jb * cin:(jb + 1) * cin], bm_ref[jb])
        for part in range(2):
            for gi in range(gpc):
                slab = part * nslab + jb * gpc + gi
                col = part * S5_LANES + gi * LANES
                for b in range(nbatch):
                    bu_slab[b, slab * SLAB_PITCH:slab * SLAB_PITCH + CHUNK, :] = (
                        bu[b * CHUNK:(b + 1) * CHUNK, col:col + LANES])

    a_re = [ab_ref[0, hf * SUBLANES:(hf + 1) * SUBLANES, :] for hf in range(nhalf)]
    a_im = [ab_ref[1, hf * SUBLANES:(hf + 1) * SUBLANES, :] for hf in range(nhalf)]

    def step(t, hs):
        out = []
        for b in range(nbatch):
            for hf in range(nhalf):
                h_re, h_im = hs[2 * (b * nhalf + hf)], hs[2 * (b * nhalf + hf) + 1]
                ts_re = pl.ds(hf * SUBLANES * SLAB_PITCH + t, SUBLANES, stride=SLAB_PITCH)
                ts_im = pl.ds((nslab + hf * SUBLANES) * SLAB_PITCH + t, SUBLANES, stride=SLAB_PITCH)
                n_re = a_re[hf] * h_re - a_im[hf] * h_im + bu_slab[b, ts_re, :]
                n_im = a_re[hf] * h_im + a_im[hf] * h_re + bu_slab[b, ts_im, :]
                bu_slab[b, ts_re, :] = n_re
                bu_slab[b, ts_im, :] = n_im
                out += [n_re, n_im]
        return tuple(out)

    init = []
    for b in range(nbatch):
        for hf in range(nhalf):
            init += [hcar_ref[b, 0, hf], hcar_ref[b, 1, hf]]
    hs = lax.fori_loop(0, CHUNK, step, tuple(init), unroll=4)
    for b in range(nbatch):
        for hf in range(nhalf):
            hcar_ref[b, 0, hf] = hs[2 * (b * nhalf + hf)]
            hcar_ref[b, 1, hf] = hs[2 * (b * nhalf + hf) + 1]

    def states(part, jb):
        return jnp.concatenate(
            [jnp.concatenate(
                [bu_slab[b, (part * nslab + jb * gpc + gi) * SLAB_PITCH:
                         (part * nslab + jb * gpc + gi) * SLAB_PITCH + CHUNK, :] for gi in range(gpc)], axis=1)
             for b in range(nbatch)], axis=0).astype(BF16)

    ys = []
    for jb in range(ncol):
        cs = slice(jb * cin, (jb + 1) * cin)
        y = _dot(states(0, jb), cre_ref[jb]) - _dot(states(1, jb), cim_ref[jb]) + d_ref[:, cs] * u2[:, cs]
        ys.append(_gelu_tanh(y))
    yg = jnp.concatenate(ys, axis=1)
    yc = yg * jax.nn.sigmoid(_dot(yg.astype(BF16), gw_ref[...]) + gb_ref[...])
    y_ref[:, :, :s5w] = yc.reshape(nbatch, CHUNK, s5w)

    logits = lbl_ref[...]
    pexp = jnp.exp(logits - jnp.max(logits, axis=0, keepdims=True))
    psm = pexp / jnp.sum(pexp, axis=0, keepdims=True)
    lb = jnp.zeros_like(psm[0:1, :])
    for l in range(layer):
        lb = lb + psm[l:l + 1, :]

    ti = lax.broadcasted_iota(jnp.int32, (CHUNK, CHUNK), 0)
    si = lax.broadcasted_iota(jnp.int32, (CHUNK, CHUNK), 1)
    tril = (ti >= si).astype(BF16)
    tp = lax.broadcasted_iota(jnp.int32, (CHUNK, 2 * CHUNK), 0)
    sp = lax.broadcasted_iota(jnp.int32, (CHUNK, 2 * CHUNK), 1) & (CHUNK - 1)
    diag_mask = jnp.logical_and((tp >> 3) == (sp >> 3), sp <= tp)
    nvr = CHUNK // SUBLANES

    def both_heads(x, p):
        xa = x[:, p * pair:p * pair + HG_DK]
        xb_ = x[:, p * pair + HG_DK:(p + 1) * pair]
        zero = jnp.zeros_like(xa)
        return jnp.concatenate([jnp.concatenate([xa, zero], axis=1),
                                jnp.concatenate([zero, xb_], axis=1)], axis=0)

    def hg_body(b, carry):
        f = lb + (1.0 - lb) * jax.nn.sigmoid(f_ref[b])
        logf = jnp.log(f)
        kk = 1.0 - f
        lf_hi = logf.astype(BF16)
        lf_lo = (logf - lf_hi.astype(F32)).astype(BF16)
        cum = _dot(tril, lf_hi) + _dot(tril, lf_lo)
        total = cum[CHUNK - 1:CHUNK, :]
        q = q_ref[b]
        iv = i_ref[b]
        ivb = iv.astype(BF16)
        q_in = (q * jnp.exp(cum)).astype(BF16)
        k_dec = kk * jnp.exp(total - cum)
        dec = jnp.exp(total)

        att = [jnp.zeros((CHUNK, 2 * CHUNK), F32) for _ in range(HG_HEADS // 2)]
        ends = [cum[v * SUBLANES + SUBLANES - 1:(v + 1) * SUBLANES, :] for v in range(nvr)]
        zeros8 = jnp.zeros((SUBLANES, hw), F32)
        for n in _HG_LEVELS:
            per = n // SUBLANES
            qparts, kparts = [], []
            for v in range(nvr):
                blk = v // per
                vs = slice(v * SUBLANES, (v + 1) * SUBLANES)
                if blk % 2 == 1:
                    qparts.append(q[vs] * jnp.exp(cum[vs] - ends[blk * per - 1]))
                    kparts.append(zeros8)
                else:
                    qparts.append(zeros8)
                    kparts.append(kk[vs] * jnp.exp(ends[blk * per + per - 1] - cum[vs]))
            q_t = jnp.concatenate(qparts, axis=0).astype(BF16)
            k_t = jnp.concatenate(kparts, axis=0).astype(BF16)
            shift = n.bit_length()
            for p in range(HG_HEADS // 2):
                a_n = _dot_nt(q_t[:, p * pair:(p + 1) * pair], both_heads(k_t, p))
                if 2 * n < CHUNK:
                    a_n = jnp.where((tp >> shift) == (sp >> shift), a_n, 0.0)
                att[p] = att[p] + a_n
        q3 = q.reshape(nvr, SUBLANES, hw)
        k3 = kk.reshape(nvr, SUBLANES, hw)
        c3 = cum.reshape(nvr, SUBLANES, hw)
        dsum = [jnp.zeros((CHUNK, 2 * CHUNK), F32) for _ in range(HG_HEADS // 2)]
        for s in range(SUBLANES):
            w = q3 * k3[:, s:s + 1, :] * jnp.exp(jnp.minimum(c3 - c3[:, s:s + 1, :], 0.0))
            wb = w.reshape(CHUNK, hw).astype(BF16)
            for p in range(HG_HEADS // 2):
                dsum[p] = dsum[p] + _dot(wb[:, p * pair:(p + 1) * pair], sel_ref[s])
        for p in range(HG_HEADS // 2):
            a_all = (att[p] + jnp.where(diag_mask, dsum[p], 0.0)).astype(BF16)
            o_pair = _dot(a_all, both_heads(ivb, p))
            for hh in range(2):
                h = 2 * p + hh
                hs_ = slice(h * HG_DK, (h + 1) * HG_DK)
                st = st_ref[b, h]
                oh = o_pair[:, hh * HG_DK:(hh + 1) * HG_DK] + _dot_nt(q_in[:, hs_], st.astype(BF16))
                st_ref[b, h] = dec[:, hs_] * st + _dot(iv[:, hs_].T.astype(BF16), k_dec[:, hs_].astype(BF16))
                ms = jnp.mean(oh * oh, axis=-1, keepdims=True)
                gt = g_ref[b, :, hs_]
                y_ref[b, :, s5w + h * HG_DK:s5w + (h + 1) * HG_DK] = (
                    oh * lax.rsqrt(ms + EPS) * gn_ref[:, hs_] * (gt * jax.nn.sigmoid(gt)))
        return carry

    lax.fori_loop(0, nbatch, hg_body, 0)


def _block_diag(t):
    n, g, r, c = t.shape
    out = jnp.zeros((n, g, r, g, c), t.dtype)
    for gi in range(g):
        out = out.at[:, gi, :, gi, :].set(t[:, gi])
    return out.reshape(n, g * r, g * c)


def _mixer_cd(z3, a_re_log, a_im, b_re, b_im, c_re, c_im, d, log_dt, glu_w, glu_b, hg_gn, lb_logits,
              *, layer):
    batch, lp, _ = z3.shape
    groups, state = a_re_log.shape
    s5w = groups * S5_GROUP
    nstate = groups * state
    gpc = S5_LANES // state
    ncol = groups // gpc
    cin = gpc * S5_GROUP
    hw = HG_HEADS * HG_DK
    nslab = nstate // LANES
    assert s5w == hw and nslab % SUBLANES == 0

    bre = _block_diag(jnp.transpose(b_re.reshape(ncol, gpc, state, S5_GROUP), (0, 1, 3, 2)))
    bim = _block_diag(jnp.transpose(b_im.reshape(ncol, gpc, state, S5_GROUP), (0, 1, 3, 2)))
    cre = _block_diag(jnp.transpose(c_re.reshape(ncol, gpc, S5_GROUP, state), (0, 1, 3, 2))).astype(BF16)
    cim = _block_diag(jnp.transpose(c_im.reshape(ncol, gpc, S5_GROUP, state), (0, 1, 3, 2))).astype(BF16)
    ldt = jnp.repeat(log_dt, state)

    def zspec(col):
        return pl.BlockSpec((batch, CHUNK, s5w), lambda c: (0, c, col))

    def pspec(shape):
        return pl.BlockSpec(shape, lambda c: (0,) * len(shape))

    return pl.pallas_call(
        functools.partial(_mixer_cd_kernel, layer=layer),
        grid=(lp // CHUNK,),
        in_specs=[
            zspec(0), zspec(1), zspec(2), zspec(3), zspec(4),
            pspec((1, nstate)), pspec((1, nstate)), pspec((1, nstate)),
            pspec((nslab, LANES)), pspec((nslab, LANES)), pspec((nslab, LANES)),
            pspec(bre.shape), pspec(bim.shape), pspec(cre.shape), pspec(cim.shape), pspec((1, s5w)),
            pspec((s5w, s5w)), pspec((1, s5w)), pspec((1, hw)), pspec(lb_logits.shape),
        ],
        out_specs=pl.BlockSpec((batch, CHUNK, s5w + hw), lambda c: (0, c, 0)),
        out_shape=jax.ShapeDtypeStruct((batch, lp, s5w + hw), F32),
        scratch_shapes=[
            pltpu.VMEM((ncol, cin, 2 * S5_LANES), BF16),
            pltpu.VMEM((2, nslab, LANES), F32),
            pltpu.VMEM((batch, 2, nslab // SUBLANES, SUBLANES, LANES), F32),
            pltpu.VMEM((batch, 2 * nslab * SLAB_PITCH, LANES), F32),
            pltpu.VMEM((SUBLANES, 2 * HG_DK, 2 * CHUNK), BF16),
            pltpu.VMEM((batch, HG_HEADS, HG_DK, HG_DK), F32),
        ],
        compiler_params=pltpu.CompilerParams(
            dimension_semantics=("arbitrary",), vmem_limit_bytes=VMEM_LIMIT_BYTES),
        name="mixer_cd",
    )(z3, z3, z3, z3, z3,
      a_re_log.reshape(1, nstate), a_im.reshape(1, nstate), ldt.reshape(1, nstate),
      a_re_log.reshape(nslab, LANES), a_im.reshape(nslab, LANES), ldt.reshape(nslab, LANES),
      bre, bim, cre, cim, d.reshape(1, s5w), glu_w.astype(BF16), glu_b.reshape(1, s5w),
      hg_gn.reshape(1, hw), lb_logits)


def _pack_lru_gates(wa, wi):
    nblk, bd, _ = wa.shape
    per = LANES // bd
    wa_bd = _block_diag(wa.reshape(nblk // per, per, bd, bd))
    wi_bd = _block_diag(wi.reshape(nblk // per, per, bd, bd))
    return jnp.concatenate([wa_bd, wi_bd], axis=2).astype(BF16)


def _tiles(rows_per_batch, n):
    tm = _largest_divisor(rows_per_batch, 1040)
    tn = max(t for t in (256, 512, 1024, 1280) if n % t == 0)
    return tm, tn


def kernel(x, meta, w_in_ab, w_out_ab, ret_gn, rg_wa, rg_ba, rg_wi, rg_bi, rg_lam, rg_conv_w, rg_conv_b,
           w_in_cd, w_out_cd, s5_a_re_log, s5_a_im, s5_b_re, s5_b_im, s5_c_re, s5_c_im, s5_d, s5_log_dt,
           s5_glu_w, s5_glu_b, hg_gn, hg_lb_logits, norm_g, mlp_w1, mlp_w2):
    batch, seq, d = x.shape
    depth = norm_g.shape[0]
    lp = PAD + N_META + seq
    assert lp % CHUNK == 0
    m = batch * lp

    h = jnp.concatenate([jnp.zeros((batch, PAD, d), x.dtype),
                         jnp.broadcast_to(meta[None].astype(x.dtype), (batch, N_META, d)), x], axis=1)
    h = h.reshape(m, d)

    for l in range(depth):
        jdx = l // 2
        if l % 2 == 0:
            w_in, w_out = w_in_ab[jdx], w_out_ab[jdx]
        else:
            w_in, w_out = w_in_cd[jdx], w_out_cd[jdx]
        tm, tn = _tiles(lp, w_in.shape[1])
        z = _norm_matmul(h, norm_g[l, 0], w_in.astype(BF16), tm=tm, tn=tn, relu2=False, out_dtype=F32)
        z3 = z.reshape(batch, lp, -1)
        if l % 2 == 0:
            y = _mixer_ab(z3, ret_gn[jdx], _pack_lru_gates(rg_wa[jdx], rg_wi[jdx]), rg_ba[jdx], rg_bi[jdx],
                          rg_lam[jdx], rg_conv_w[jdx], rg_conv_b[jdx])
        else:
            y = _mixer_cd(z3, s5_a_re_log[jdx], s5_a_im[jdx], s5_b_re[jdx], s5_b_im[jdx], s5_c_re[jdx],
                          s5_c_im[jdx], s5_d[jdx], s5_log_dt[jdx], s5_glu_w[jdx], s5_glu_b[jdx], hg_gn[jdx],
                          hg_lb_logits, layer=l)
        y = y.reshape(m, -1)
        tk = max(t for t in (256, 512, 1024) if w_out.shape[0] % t == 0)
        h = _matmul_norm_res(y, w_out.astype(BF16), norm_g[l, 1], h, tm=tm, tk=tk, rows_per_batch=lp)
        h = _mlp(h, norm_g[l, 2], mlp_w1[l].astype(BF16), mlp_w2[l].astype(BF16), norm_g[l, 3],
                 tm=_largest_divisor(lp, 520), tn=1024, rows_per_batch=lp)

    return h.reshape(batch, lp, d)[:, PAD + N_META:]
```

```python
import functools
import math

import jax
import jax.numpy as jnp
from jax import lax
from jax.experimental import pallas as pl
from jax.experimental.pallas import tpu as pltpu

F32 = jnp.float32
BF16 = jnp.bfloat16

CHUNK = 64
N_META = 16
PAD = CHUNK - N_META
EPS = 1e-6

RET_HEADS = 4
RET_DK = 128
RET_DV = 256
ROPE_BASE = 10000.0
LRU_C = 8.0
CONV_WIDTH = 4
S5_GROUP = 16
S5_STATE = 64
S5_LANES = 512
HG_HEADS = 4
HG_DK = 128
LANES = 128
SUBLANES = 8
SLAB_PITCH = CHUNK + SUBLANES
VMEM_LIMIT_BYTES = 56 * 1024 * 1024


def _largest_divisor(n, cap):
    return max(d for d in range(1, cap + 1) if n % d == 0)


def _gelu_tanh(x):
    return 0.5 * x * (1.0 + jnp.tanh(0.7978845608028654 * (x + 0.044715 * x * x * x)))


def _dot(a, b):
    return jnp.dot(a, b, preferred_element_type=F32)


def _dot_nt(a, b):
    return lax.dot_general(a, b, (((1,), (1,)), ((), ())), preferred_element_type=F32)


def _keep_rows(out, tiles_per_batch):
    first_tile = lax.rem(pl.program_id(0), tiles_per_batch) == 0
    row = lax.broadcasted_iota(jnp.int32, (out.shape[0], 1), 0)
    keep = jnp.logical_or(row >= PAD, jnp.logical_not(first_tile))
    return jnp.where(keep, out, 0.0)


def _mlp_kernel(h_ref, g_in_ref, w1_ref, w2_ref, g_out_ref, o_ref, *, tiles_per_batch, tn):
    x = h_ref[...]
    ms = jnp.mean(x * x, axis=-1, keepdims=True)
    hn = (x * lax.rsqrt(ms + EPS) * g_in_ref[...]).astype(BF16)
    acc = jnp.zeros(x.shape, F32)
    for j in range(w1_ref.shape[1] // tn):
        a = jnp.maximum(_dot(hn, w1_ref[:, j * tn:(j + 1) * tn]), 0.0)
        acc = acc + _dot((a * a).astype(BF16), w2_ref[j * tn:(j + 1) * tn, :])
    ms = jnp.mean(acc * acc, axis=-1, keepdims=True)
    o_ref[...] = _keep_rows(x + acc * lax.rsqrt(ms + EPS) * g_out_ref[...], tiles_per_batch)


def _mlp(h, g_in, w1, w2, g_out, *, tm, tn, rows_per_batch):
    m, d = h.shape
    f = w1.shape[1]
    resident = pl.Buffered(1)
    return pl.pallas_call(
        functools.partial(_mlp_kernel, tiles_per_batch=rows_per_batch // tm, tn=tn),
        grid=(m // tm,),
        in_specs=[
            pl.BlockSpec((tm, d), lambda i: (i, 0)),
            pl.BlockSpec((1, d), lambda i: (0, 0)),
            pl.BlockSpec((d, f), lambda i: (0, 0), pipeline_mode=resident),
            pl.BlockSpec((f, d), lambda i: (0, 0), pipeline_mode=resident),
            pl.BlockSpec((1, d), lambda i: (0, 0)),
        ],
        out_specs=pl.BlockSpec((tm, d), lambda i: (i, 0)),
        out_shape=jax.ShapeDtypeStruct((m, d), F32),
        compiler_params=pltpu.CompilerParams(
            dimension_semantics=("arbitrary",), vmem_limit_bytes=VMEM_LIMIT_BYTES),
        name="mlp",
    )(h, g_in.reshape(1, d), w1, w2, g_out.reshape(1, d))


def _layer_kernel(h_ref, g_in_ref, win_ref, wout_ref, g_out_ref, *rest, mixer, nparams, cols, tn):
    params = rest[:nparams]
    o_ref, z_ref, y_ref = rest[nparams:nparams + 3]
    scratch = rest[nparams + 3:]
    nbatch, ch, d = h_ref.shape
    x = h_ref[...].reshape(nbatch * ch, d)
    ms = jnp.mean(x * x, axis=-1, keepdims=True)
    hn = (x * lax.rsqrt(ms + EPS) * g_in_ref[...]).astype(BF16)
    for j in range(win_ref.shape[1] // tn):
        z_ref[:, :, j * tn:(j + 1) * tn] = _dot(hn, win_ref[:, j * tn:(j + 1) * tn]).reshape(nbatch, ch, tn)
    views = []
    off = 0
    for width in cols:
        views.append(z_ref.at[:, :, off:off + width])
        off += width
    mixer(*views, *params, y_ref, *scratch)
    m = _dot(y_ref[...].reshape(nbatch * ch, y_ref.shape[2]).astype(BF16), wout_ref[...])
    ms = jnp.mean(m * m, axis=-1, keepdims=True)
    out = (x + m * lax.rsqrt(ms + EPS) * g_out_ref[...]).reshape(nbatch, ch, d)
    row = lax.broadcasted_iota(jnp.int32, (1, ch, 1), 1)
    o_ref[...] = jnp.where(jnp.logical_or(row >= PAD, pl.program_id(0) > 0), out, 0.0)


def _layer(h3, g_in, w_in, w_out, g_out, params, *, mixer, cols, scratch, name):
    batch, lp, d = h3.shape
    nin = w_in.shape[1]
    nout = w_out.shape[0]
    tn = max(t for t in (512, 1024, 1280) if nin % t == 0)
    resident = pl.Buffered(1)

    def pspec(shape):
        return pl.BlockSpec(shape, lambda c: (0,) * len(shape))

    return pl.pallas_call(
        functools.partial(_layer_kernel, mixer=mixer, nparams=len(params), cols=cols, tn=tn),
        grid=(lp // CHUNK,),
        in_specs=[
            pl.BlockSpec((batch, CHUNK, d), lambda c: (0, c, 0)),
            pspec((1, d)),
            pl.BlockSpec((d, nin), lambda c: (0, 0), pipeline_mode=resident),
            pl.BlockSpec((nout, d), lambda c: (0, 0), pipeline_mode=resident),
            pspec((1, d)),
        ] + [pspec(p.shape) for p in params],
        out_specs=pl.BlockSpec((batch, CHUNK, d), lambda c: (0, c, 0)),
        out_shape=jax.ShapeDtypeStruct((batch, lp, d), F32),
        scratch_shapes=[pltpu.VMEM((batch, CHUNK, nin), F32), pltpu.VMEM((batch, CHUNK, nout), F32)] + scratch,
        compiler_params=pltpu.CompilerParams(
            dimension_semantics=("arbitrary",), vmem_limit_bytes=VMEM_LIMIT_BYTES),
        name=name,
    )(h3, g_in.reshape(1, d), w_in, w_out, g_out.reshape(1, d), *params)


def _mixer_ab_kernel(q_ref, k_ref, v_ref, gate_ref, bx_ref, bg_ref, gn_ref, wg_ref, ba_ref, bi_ref,
                     lam_ref, cw_ref, cb_ref, y_ref, s_ref, xcar_ref, hcar_ref, a_slab, b_slab):
    c = pl.program_id(0)
    nbatch, _, width = bx_ref.shape
    ngrp = width // LANES

    @pl.when(c == 0)
    def _():
        s_ref[...] = jnp.zeros_like(s_ref)
        xcar_ref[...] = jnp.zeros_like(xcar_ref)
        hcar_ref[...] = jnp.zeros_like(hcar_ref)

    row = lax.broadcasted_iota(jnp.int32, (CHUNK, 1), 0)
    idx = row.astype(F32)
    pos = (c * CHUNK + row - PAD).astype(F32)
    lane = lax.broadcasted_iota(jnp.int32, (1, RET_DK), 1)
    half = RET_DK // 2
    freq = jnp.exp((lane & (half - 1)).astype(F32) * (-math.log(ROPE_BASE) / half))
    ang = pos * freq
    cosv = jnp.cos(ang)
    sinv = jnp.where(lane < half, -1.0, 1.0) * jnp.sin(ang)
    ti = lax.broadcasted_iota(jnp.int32, (CHUNK, CHUNK), 0)
    si = lax.broadcasted_iota(jnp.int32, (CHUNK, CHUNK), 1)
    dist = jnp.abs(ti - si).astype(F32)

    def ret_body(b, carry):
        for h in range(RET_HEADS):
            log_g = math.log1p(-(2.0 ** (-5.0 - h)))
            qs = slice(h * RET_DK, (h + 1) * RET_DK)
            vs = slice(h * RET_DV, (h + 1) * RET_DV)
            qh = q_ref[b, :, qs]
            kh = k_ref[b, :, qs]
            qr = (qh * cosv + pltpu.roll(qh, half, 1) * sinv) * (RET_DK ** -0.5)
            kr = kh * cosv + pltpu.roll(kh, half, 1) * sinv
            vh = v_ref[b, :, vs].astype(BF16)
            scores = _dot_nt(qr.astype(BF16), kr.astype(BF16)) * jnp.exp(dist * log_g)
            o = _dot(scores.astype(BF16), vh)
            q_dec = qr * jnp.exp((idx + 1.0) * log_g)
            o = o + _dot(q_dec.astype(BF16), s_ref[b, h].astype(BF16))
            k_dec = kr * jnp.exp((CHUNK - 1.0 - idx) * log_g)
            kv = _dot(k_dec.T.astype(BF16), vh)
            s_ref[b, h] = math.exp(CHUNK * log_g) * s_ref[b, h] + kv
            oc = o - jnp.mean(o, axis=-1, keepdims=True)
            var = jnp.mean(oc * oc, axis=-1, keepdims=True)
            gt = gate_ref[b, :, vs]
            y_ref[b, :, vs] = oc * lax.rsqrt(var + EPS) * gn_ref[:, vs] * (gt * jax.nn.sigmoid(gt))
        return carry

    lax.fori_loop(0, nbatch, ret_body, 0)

    xb = bx_ref[...]
    xe = jnp.concatenate([xcar_ref[...], xb], axis=1)
    xc = cb_ref[...] + xb * cw_ref[CONV_WIDTH - 1:CONV_WIDTH, :]
    for s in range(1, CONV_WIDTH):
        xc = xc + pltpu.roll(xe, s, 1)[:, SUBLANES:, :] * cw_ref[CONV_WIDTH - 1 - s:CONV_WIDTH - s, :]
    xcar_ref[...] = xb[:, CHUNK - SUBLANES:, :]

    xc2 = xc.reshape(nbatch * CHUNK, width)
    xcb = xc2.astype(BF16)
    valid = jnp.logical_or(row >= PAD, c > 0)
    for p in range(ngrp):
        cs = slice(p * LANES, (p + 1) * LANES)
        g2 = _dot(xcb[:, cs], wg_ref[p])
        r = jax.nn.sigmoid(g2[:, :LANES] + ba_ref[:, cs])
        i = jax.nn.sigmoid(g2[:, LANES:] + bi_ref[:, cs])
        lam = lam_ref[:, cs]
        softplus_neg_lam = jnp.maximum(-lam, 0.0) + jnp.log1p(jnp.exp(-jnp.abs(lam)))
        a = jnp.exp(-LRU_C * r * softplus_neg_lam)
        bb = jnp.sqrt(1.0 - a * a) * (i * xc2[:, cs])
        for b in range(nbatch):
            rs = slice(b * CHUNK, (b + 1) * CHUNK)
            a_slab[b, p * SLAB_PITCH:p * SLAB_PITCH + CHUNK, :] = a[rs]
            b_slab[b, p * SLAB_PITCH:p * SLAB_PITCH + CHUNK, :] = jnp.where(valid, bb[rs], 0.0)

    def step(t, hs):
        out = []
        for b in range(nbatch):
            ts = pl.ds(t, ngrp, stride=SLAB_PITCH)
            h = a_slab[b, ts, :] * hs[b] + b_slab[b, ts, :]
            b_slab[b, ts, :] = h
            out.append(h)
        return tuple(out)

    hs = lax.fori_loop(0, CHUNK, step, tuple(hcar_ref[b] for b in range(nbatch)), unroll=8)
    for b in range(nbatch):
        hcar_ref[b] = hs[b]
        hfull = jnp.concatenate(
            [b_slab[b, p * SLAB_PITCH:p * SLAB_PITCH + CHUNK, :] for p in range(ngrp)], axis=1)
        y_ref[b, :, RET_HEADS * RET_DV:] = _gelu_tanh(bg_ref[b]) * hfull


def _layer_ab(h3, g_in, w_in, w_out, g_out, ret_gn, wg, ba, bi, lam, conv_w, conv_b):
    batch = h3.shape[0]
    qk = RET_HEADS * RET_DK
    vw = RET_HEADS * RET_DV
    lw = lam.shape[-1]
    assert lw == SUBLANES * LANES
    params = [ret_gn.reshape(1, vw), wg, ba.reshape(1, lw), bi.reshape(1, lw), lam.reshape(1, lw),
              conv_w, conv_b.reshape(1, lw)]
    scratch = [
        pltpu.VMEM((batch, RET_HEADS, RET_DK, RET_DV), F32),
        pltpu.VMEM((batch, SUBLANES, lw), F32),
        pltpu.VMEM((batch, SUBLANES, LANES), F32),
        pltpu.VMEM((batch, SUBLANES * SLAB_PITCH, LANES), F32),
        pltpu.VMEM((batch, SUBLANES * SLAB_PITCH, LANES), F32),
    ]
    return _layer(h3, g_in, w_in, w_out, g_out, params, mixer=_mixer_ab_kernel,
                  cols=(qk, qk, vw, vw, lw, lw), scratch=scratch, name="layer_ab")


_HG_LEVELS = (8, 16, 32)


def _mixer_cd_kernel(u_ref, q_ref, f_ref, i_ref, g_ref,
                     arl_ref, aim_ref, ldt_ref, arl_t_ref, aim_t_ref, ldt_t_ref,
                     bre_ref, bim_ref, cre_ref, cim_ref, d_ref, gw_ref, gb_ref, gn_ref, lbl_ref, y_ref,
                     bm_ref, ab_ref, hcar_ref, bu_slab, sel_ref, st_ref, *, layer):
    c = pl.program_id(0)
    nbatch, _, s5w = u_ref.shape
    ncol = bre_ref.shape[0]
    cin = bre_ref.shape[1]
    nslab = ab_ref.shape[1]
    nhalf = nslab // SUBLANES
    gpc = S5_LANES // LANES
    hw = HG_HEADS * HG_DK
    pair = 2 * HG_DK

    @pl.when(c == 0)
    def _():
        def disc(ldt, arl, aim):
            dt = jnp.exp(ldt)
            a_re = -jnp.exp(arl)
            mag = jnp.exp(dt * a_re)
            return a_re, aim, mag * jnp.cos(dt * aim), mag * jnp.sin(dt * aim)

        _, _, t_re, t_im = disc(ldt_t_ref[...], arl_t_ref[...], aim_t_ref[...])
        ab_ref[0] = t_re
        ab_ref[1] = t_im
        a_re, a_im, ab_re, ab_im = disc(ldt_ref[...], arl_ref[...], aim_ref[...])
        den = a_re * a_re + a_im * a_im
        z_re = ((ab_re - 1.0) * a_re + ab_im * a_im) / den
        z_im = (ab_im * a_re - (ab_re - 1.0) * a_im) / den
        for jb in range(ncol):
            ls = slice(jb * S5_LANES, (jb + 1) * S5_LANES)
            zr = z_re[:, ls]
            zi = z_im[:, ls]
            bb_re = zr * bre_ref[jb] - zi * bim_ref[jb]
            bb_im = zr * bim_ref[jb] + zi * bre_ref[jb]
            bm_ref[jb] = jnp.concatenate([bb_re, bb_im], axis=1).astype(BF16)
        ri = lax.broadcasted_iota(jnp.int32, (pair, 2 * CHUNK), 0)
        ci = lax.broadcasted_iota(jnp.int32, (pair, 2 * CHUNK), 1)
        same_head = (ri >= HG_DK) == (ci >= CHUNK)
        for s in range(SUBLANES):
            sel_ref[s] = jnp.logical_and(same_head, (ci & (SUBLANES - 1)) == s).astype(BF16)
        hcar_ref[...] = jnp.zeros_like(hcar_ref)
        st_ref[...] = jnp.zeros_like(st_ref)

    u2 = u_ref[...].reshape(nbatch * CHUNK, s5w)
    ub = u2.astype(BF16)
    for jb in range(ncol):
        bu = _dot(ub[:, jb * cin:(jb + 1) * cin], bm_ref[jb])
        for part in range(2):
            for gi in range(gpc):
                slab = part * nslab + jb * gpc + gi
                col = part * S5_LANES + gi * LANES
                for b in range(nbatch):
                    bu_slab[b, slab * SLAB_PITCH:slab * SLAB_PITCH + CHUNK, :] = (
                        bu[b * CHUNK:(b + 1) * CHUNK, col:col + LANES])

    a_re = [ab_ref[0, hf * SUBLANES:(hf + 1) * SUBLANES, :] for hf in range(nhalf)]
    a_im = [ab_ref[1, hf * SUBLANES:(hf + 1) * SUBLANES, :] for hf in range(nhalf)]

    def step(t, hs):
        out = []
        for b in range(nbatch):
            for hf in range(nhalf):
                h_re, h_im = hs[2 * (b * nhalf + hf)], hs[2 * (b * nhalf + hf) + 1]
                ts_re = pl.ds(hf * SUBLANES * SLAB_PITCH + t, SUBLANES, stride=SLAB_PITCH)
                ts_im = pl.ds((nslab + hf * SUBLANES) * SLAB_PITCH + t, SUBLANES, stride=SLAB_PITCH)
                n_re = a_re[hf] * h_re - a_im[hf] * h_im + bu_slab[b, ts_re, :]
                n_im = a_re[hf] * h_im + a_im[hf] * h_re + bu_slab[b, ts_im, :]
                bu_slab[b, ts_re, :] = n_re
                bu_slab[b, ts_im, :] = n_im
                out += [n_re, n_im]
        return tuple(out)

    init = []
    for b in range(nbatch):
        for hf in range(nhalf):
            init += [hcar_ref[b, 0, hf], hcar_ref[b, 1, hf]]
    hs = lax.fori_loop(0, CHUNK, step, tuple(init), unroll=4)
    for b in range(nbatch):
        for hf in range(nhalf):
            hcar_ref[b, 0, hf] = hs[2 * (b * nhalf + hf)]
            hcar_ref[b, 1, hf] = hs[2 * (b * nhalf + hf) + 1]

    def states(part, jb):
        return jnp.concatenate(
            [jnp.concatenate(
                [bu_slab[b, (part * nslab + jb * gpc + gi) * SLAB_PITCH:
                         (part * nslab + jb * gpc + gi) * SLAB_PITCH + CHUNK, :] for gi in range(gpc)], axis=1)
             for b in range(nbatch)], axis=0).astype(BF16)

    ys = []
    for jb in range(ncol):
        cs = slice(jb * cin, (jb + 1) * cin)
        y = _dot(states(0, jb), cre_ref[jb]) - _dot(states(1, jb), cim_ref[jb]) + d_ref[:, cs] * u2[:, cs]
        ys.append(_gelu_tanh(y))
    yg = jnp.concatenate(ys, axis=1)
    yc = yg * jax.nn.sigmoid(_dot(yg.astype(BF16), gw_ref[...]) + gb_ref[...])
    y_ref[:, :, :s5w] = yc.reshape(nbatch, CHUNK, s5w)

    logits = lbl_ref[...]
    pexp = jnp.exp(logits - jnp.max(logits, axis=0, keepdims=True))
    psm = pexp / jnp.sum(pexp, axis=0, keepdims=True)
    lb = jnp.zeros_like(psm[0:1, :])
    for l in range(layer):
        lb = lb + psm[l:l + 1, :]

    ti = lax.broadcasted_iota(jnp.int32, (CHUNK, CHUNK), 0)
    si = lax.broadcasted_iota(jnp.int32, (CHUNK, CHUNK), 1)
    tril = (ti >= si).astype(BF16)
    tp = lax.broadcasted_iota(jnp.int32, (CHUNK, 2 * CHUNK), 0)
    sp = lax.broadcasted_iota(jnp.int32, (CHUNK, 2 * CHUNK), 1) & (CHUNK - 1)
    diag_mask = jnp.logical_and((tp >> 3) == (sp >> 3), sp <= tp)
    nvr = CHUNK // SUBLANES

    def both_heads(x, p):
        xa = x[:, p * pair:p * pair + HG_DK]
        xb_ = x[:, p * pair + HG_DK:(p + 1) * pair]
        zero = jnp.zeros_like(xa)
        return jnp.concatenate([jnp.concatenate([xa, zero], axis=1),
                                jnp.concatenate([zero, xb_], axis=1)], axis=0)

    def hg_body(b, carry):
        f = lb + (1.0 - lb) * jax.nn.sigmoid(f_ref[b])
        logf = jnp.log(f)
        kk = 1.0 - f
        lf_hi = logf.astype(BF16)
        lf_lo = (logf - lf_hi.astype(F32)).astype(BF16)
        cum = _dot(tril, lf_hi) + _dot(tril, lf_lo)
        total = cum[CHUNK - 1:CHUNK, :]
        q = q_ref[b]
        iv = i_ref[b]
        ivb = iv.astype(BF16)
        q_in = (q * jnp.exp(cum)).astype(BF16)
        k_dec = kk * jnp.exp(total - cum)
        dec = jnp.exp(total)

        att = [jnp.zeros((CHUNK, 2 * CHUNK), F32) for _ in range(HG_HEADS // 2)]
        ends = [cum[v * SUBLANES + SUBLANES - 1:(v + 1) * SUBLANES, :] for v in range(nvr)]
        zeros8 = jnp.zeros((SUBLANES, hw), F32)
        for n in _HG_LEVELS:
            per = n // SUBLANES
            qparts, kparts = [], []
            for v in range(nvr):
                blk = v // per
                vs = slice(v * SUBLANES, (v + 1) * SUBLANES)
                if blk % 2 == 1:
                    qparts.append(q[vs] * jnp.exp(cum[vs] - ends[blk * per - 1]))
                    kparts.append(zeros8)
                else:
                    qparts.append(zeros8)
                    kparts.append(kk[vs] * jnp.exp(ends[blk * per + per - 1] - cum[vs]))
            q_t = jnp.concatenate(qparts, axis=0).astype(BF16)
            k_t = jnp.concatenate(kparts, axis=0).astype(BF16)
            shift = n.bit_length()
            for p in range(HG_HEADS // 2):
                a_n = _dot_nt(q_t[:, p * pair:(p + 1) * pair], both_heads(k_t, p))
                if 2 * n < CHUNK:
                    a_n = jnp.where((tp >> shift) == (sp >> shift), a_n, 0.0)
                att[p] = att[p] + a_n
        q3 = q.reshape(nvr, SUBLANES, hw)
        k3 = kk.reshape(nvr, SUBLANES, hw)
        c3 = cum.reshape(nvr, SUBLANES, hw)
        dsum = [jnp.zeros((CHUNK, 2 * CHUNK), F32) for _ in range(HG_HEADS // 2)]
        for s in range(SUBLANES):
            w = q3 * k3[:, s:s + 1, :] * jnp.exp(jnp.minimum(c3 - c3[:, s:s + 1, :], 0.0))
            wb = w.reshape(CHUNK, hw).astype(BF16)
            for p in range(HG_HEADS // 2):
                dsum[p] = dsum[p] + _dot(wb[:, p * pair:(p + 1) * pair], sel_ref[s])
        for p in range(HG_HEADS // 2):
            a_all = (att[p] + jnp.where(diag_mask, dsum[p], 0.0)).astype(BF16)
            o_pair = _dot(a_all, both_heads(ivb, p))
            for hh in range(2):
                h = 2 * p + hh
                hs_ = slice(h * HG_DK, (h + 1) * HG_DK)
                st = st_ref[b, h]
                oh = o_pair[:, hh * HG_DK:(hh + 1) * HG_DK] + _dot_nt(q_in[:, hs_], st.astype(BF16))
                st_ref[b, h] = dec[:, hs_] * st + _dot(iv[:, hs_].T.astype(BF16), k_dec[:, hs_].astype(BF16))
                ms = jnp.mean(oh * oh, axis=-1, keepdims=True)
                gt = g_ref[b, :, hs_]
                y_ref[b, :, s5w + h * HG_DK:s5w + (h + 1) * HG_DK] = (
                    oh * lax.rsqrt(ms + EPS) * gn_ref[:, hs_] * (gt * jax.nn.sigmoid(gt)))
        return carry

    lax.fori_loop(0, nbatch, hg_body, 0)


def _block_diag(t):
    n, g, r, c = t.shape
    out = jnp.zeros((n, g, r, g, c), t.dtype)
    for gi in range(g):
        out = out.at[:, gi, :, gi, :].set(t[:, gi])
    return out.reshape(n, g * r, g * c)


def _layer_cd(h3, g_in, w_in, w_out, g_out, a_re_log, a_im, b_re, b_im, c_re, c_im, d, log_dt, glu_w, glu_b,
              hg_gn, lb_logits, *, layer):
    batch = h3.shape[0]
    groups, state = a_re_log.shape
    s5w = groups * S5_GROUP
    nstate = groups * state
    gpc = S5_LANES // state
    ncol = groups // gpc
    cin = gpc * S5_GROUP
    hw = HG_HEADS * HG_DK
    nslab = nstate // LANES
    assert s5w == hw and nslab % SUBLANES == 0

    bre = _block_diag(jnp.transpose(b_re.reshape(ncol, gpc, state, S5_GROUP), (0, 1, 3, 2)))
    bim = _block_diag(jnp.transpose(b_im.reshape(ncol, gpc, state, S5_GROUP), (0, 1, 3, 2)))
    cre = _block_diag(jnp.transpose(c_re.reshape(ncol, gpc, S5_GROUP, state), (0, 1, 3, 2))).astype(BF16)
    cim = _block_diag(jnp.transpose(c_im.reshape(ncol, gpc, S5_GROUP, state), (0, 1, 3, 2))).astype(BF16)
    ldt = jnp.repeat(log_dt, state)
    params = [a_re_log.reshape(1, nstate), a_im.reshape(1, nstate), ldt.reshape(1, nstate),
              a_re_log.reshape(nslab, LANES), a_im.reshape(nslab, LANES), ldt.reshape(nslab, LANES),
              bre, bim, cre, cim, d.reshape(1, s5w), glu_w.astype(BF16), glu_b.reshape(1, s5w),
              hg_gn.reshape(1, hw), lb_logits]
    scratch = [
        pltpu.VMEM((ncol, cin, 2 * S5_LANES), BF16),
        pltpu.VMEM((2, nslab, LANES), F32),
        pltpu.VMEM((batch, 2, nslab // SUBLANES, SUBLANES, LANES), F32),
        pltpu.VMEM((batch, 2 * nslab * SLAB_PITCH, LANES), F32),
        pltpu.VMEM((SUBLANES, 2 * HG_DK, 2 * CHUNK), BF16),
        pltpu.VMEM((batch, HG_HEADS, HG_DK, HG_DK), F32),
    ]
    return _layer(h3, g_in, w_in, w_out, g_out, params, mixer=functools.partial(_mixer_cd_kernel, layer=layer),
                  cols=(s5w,) * 5, scratch=scratch, name="layer_cd")


def _pack_lru_gates(wa, wi):
    nblk, bd, _ = wa.shape
    per = LANES // bd
    wa_bd = _block_diag(wa.reshape(nblk // per, per, bd, bd))
    wi_bd = _block_diag(wi.reshape(nblk // per, per, bd, bd))
    return jnp.concatenate([wa_bd, wi_bd], axis=2).astype(BF16)


def kernel(x, meta, w_in_ab, w_out_ab, ret_gn, rg_wa, rg_ba, rg_wi, rg_bi, rg_lam, rg_conv_w, rg_conv_b,
           w_in_cd, w_out_cd, s5_a_re_log, s5_a_im, s5_b_re, s5_b_im, s5_c_re, s5_c_im, s5_d, s5_log_dt,
           s5_glu_w, s5_glu_b, hg_gn, hg_lb_logits, norm_g, mlp_w1, mlp_w2):
    batch, seq, d = x.shape
    depth = norm_g.shape[0]
    lp = PAD + N_META + seq
    assert lp % CHUNK == 0
    m = batch * lp

    h = jnp.concatenate([jnp.zeros((batch, PAD, d), x.dtype),
                         jnp.broadcast_to(meta[None].astype(x.dtype), (batch, N_META, d)), x], axis=1)
    for l in range(depth):
        jdx = l // 2
        if l % 2 == 0:
            h = _layer_ab(h, norm_g[l, 0], w_in_ab[jdx].astype(BF16), w_out_ab[jdx].astype(BF16), norm_g[l, 1],
                          ret_gn[jdx], _pack_lru_gates(rg_wa[jdx], rg_wi[jdx]), rg_ba[jdx], rg_bi[jdx],
                          rg_lam[jdx], rg_conv_w[jdx], rg_conv_b[jdx])
        else:
            h = _layer_cd(h, norm_g[l, 0], w_in_cd[jdx].astype(BF16), w_out_cd[jdx].astype(BF16), norm_g[l, 1],
                          s5_a_re_log[jdx], s5_a_im[jdx], s5_b_re[jdx], s5_b_im[jdx], s5_c_re[jdx], s5_c_im[jdx],
                          s5_d[jdx], s5_log_dt[jdx], s5_glu_w[jdx], s5_glu_b[jdx], hg_gn[jdx], hg_lb_logits,
                          layer=l)
        h = _mlp(h.reshape(m, d), norm_g[l, 2], mlp_w1[l].astype(BF16), mlp_w2[l].astype(BF16), norm_g[l, 3],
                 tm=_largest_divisor(lp, 520), tn=1024, rows_per_batch=lp).reshape(batch, lp, d)

    return h[:, PAD + N_META:]
```

```python
import functools
import math

import jax
import jax.numpy as jnp
from jax import lax
from jax.experimental import pallas as pl
from jax.experimental.pallas import tpu as pltpu

F32 = jnp.float32
BF16 = jnp.bfloat16

CHUNK = 64
N_META = 16
PAD = CHUNK - N_META
EPS = 1e-6

RET_HEADS = 4
RET_DK = 128
RET_DV = 256
ROPE_BASE = 10000.0
LRU_C = 8.0
CONV_WIDTH = 4
S5_GROUP = 16
S5_STATE = 64
S5_LANES = 512
HG_HEADS = 4
HG_DK = 128
LANES = 128
SUBLANES = 8
MXU_COLS = 256
SLAB_PITCH = CHUNK + SUBLANES
VMEM_LIMIT_BYTES = 56 * 1024 * 1024


def _largest_divisor(n, cap):
    return max(d for d in range(1, cap + 1) if n % d == 0)


def _gelu_tanh(x):
    return 0.5 * x * (1.0 + jnp.tanh(0.7978845608028654 * (x + 0.044715 * x * x * x)))


def _dot(a, b):
    return jnp.dot(a, b, preferred_element_type=F32)


def _dot_nt(a, b):
    return lax.dot_general(a, b, (((1,), (1,)), ((), ())), preferred_element_type=F32)


def _keep_rows(out, tiles_per_batch):
    first_tile = lax.rem(pl.program_id(0), tiles_per_batch) == 0
    row = lax.broadcasted_iota(jnp.int32, (out.shape[0], 1), 0)
    keep = jnp.logical_or(row >= PAD, jnp.logical_not(first_tile))
    return jnp.where(keep, out, 0.0)


def _mlp_kernel(h_ref, g_in_ref, w1_ref, w2_ref, g_out_ref, o_ref, *, tiles_per_batch, tn):
    x = h_ref[...]
    ms = jnp.mean(x * x, axis=-1, keepdims=True)
    hn = (x * lax.rsqrt(ms + EPS) * g_in_ref[...]).astype(BF16)
    acc = jnp.zeros(x.shape, F32)
    for j in range(w1_ref.shape[1] // tn):
        a = jnp.maximum(_dot(hn, w1_ref[:, j * tn:(j + 1) * tn]), 0.0)
        acc = acc + _dot((a * a).astype(BF16), w2_ref[j * tn:(j + 1) * tn, :])
    ms = jnp.mean(acc * acc, axis=-1, keepdims=True)
    o_ref[...] = _keep_rows(x + acc * lax.rsqrt(ms + EPS) * g_out_ref[...], tiles_per_batch)


def _mlp(h, g_in, w1, w2, g_out, *, tm, tn, rows_per_batch):
    m, d = h.shape
    f = w1.shape[1]
    resident = pl.Buffered(1)
    return pl.pallas_call(
        functools.partial(_mlp_kernel, tiles_per_batch=rows_per_batch // tm, tn=tn),
        grid=(m // tm,),
        in_specs=[
            pl.BlockSpec((tm, d), lambda i: (i, 0)),
            pl.BlockSpec((1, d), lambda i: (0, 0)),
            pl.BlockSpec((d, f), lambda i: (0, 0), pipeline_mode=resident),
            pl.BlockSpec((f, d), lambda i: (0, 0), pipeline_mode=resident),
            pl.BlockSpec((1, d), lambda i: (0, 0)),
        ],
        out_specs=pl.BlockSpec((tm, d), lambda i: (i, 0)),
        out_shape=jax.ShapeDtypeStruct((m, d), F32),
        compiler_params=pltpu.CompilerParams(
            dimension_semantics=("arbitrary",), vmem_limit_bytes=VMEM_LIMIT_BYTES),
        name="mlp",
    )(h, g_in.reshape(1, d), w1, w2, g_out.reshape(1, d))


class _Filler:
    def __init__(self, pieces):
        self._pieces = list(pieces)
        self._total = len(self._pieces)
        self._done = 0
        self._slot = 0
        self.nslots = 1

    def __call__(self):
        self._slot += 1
        target = min(self._total, -(-self._total * self._slot // self.nslots))
        while self._done < target:
            self._pieces[self._done]()
            self._done += 1

    def drain(self):
        while self._done < self._total:
            self._pieces[self._done]()
            self._done += 1


def _layer_kernel(ha_ref, hc_ref, g_in_ref, win_ref, wout_ref, g_out_ref, *rest, mixer, init, nparams, cols, tn):
    params = rest[:nparams]
    o_ref = rest[nparams]
    z_refs = rest[nparams + 1:nparams + 3]
    y_refs = rest[nparams + 3:nparams + 5]
    hn_ref = rest[nparams + 5]
    scratch = rest[nparams + 6:]
    nbatch, _, d = ha_ref.shape
    p = pl.program_id(0)

    @pl.when(p == 0)
    def _():
        z_refs[1][...] = jnp.zeros_like(z_refs[1])
        y_refs[0][...] = jnp.zeros_like(y_refs[0])
        init(*params, *scratch)

    row = lax.broadcasted_iota(jnp.int32, (1, CHUNK, 1), 1)
    rows = nbatch * CHUNK
    for e in range(2):
        s = 2 * p + e
        rs = slice(e * CHUNK, (e + 1) * CHUNK)

        pieces = []
        m_parts = []

        def out_piece(j, e=e, m_parts=m_parts):
            cs = slice(j * tn, (j + 1) * tn)
            y_in = y_refs[e][...].reshape(rows, y_refs[e].shape[2])
            m_parts.append(_dot(y_in, wout_ref[:, cs]))

        def out_finish(e=e, s=s, rs=rs, m_parts=m_parts):
            m = jnp.concatenate(m_parts, axis=1)
            ms = jnp.mean(m * m, axis=-1, keepdims=True)
            xr = hc_ref[:, rs, :].reshape(rows, d)
            out = (xr + m * lax.rsqrt(ms + EPS) * g_out_ref[...]).reshape(nbatch, CHUNK, d)
            o_ref[:, rs, :] = jnp.where(jnp.logical_or(row >= PAD, s != 2), out, 0.0)

        def in_start(rs=rs):
            x = ha_ref[:, rs, :].reshape(rows, d)
            ms = jnp.mean(x * x, axis=-1, keepdims=True)
            hn_ref[...] = (x * lax.rsqrt(ms + EPS) * g_in_ref[...]).astype(BF16)

        def in_piece(j, e=e):
            cs = slice(j * tn, (j + 1) * tn)
            z_refs[e][:, :, cs] = _dot(hn_ref[...], win_ref[:, cs]).reshape(nbatch, CHUNK, tn)

        pieces += [functools.partial(out_piece, j) for j in range(d // tn)] + [out_finish, in_start]
        pieces += [functools.partial(in_piece, j) for j in range(win_ref.shape[1] // tn)]
        fill = _Filler(pieces)

        views = []
        off = 0
        for width in cols:
            views.append(z_refs[1 - e].at[:, :, off:off + width])
            off += width
        mixer(s - 1, fill, *views, *params, y_refs[1 - e], *scratch)
        fill.drain()


def _layer(h3, g_in, w_in, w_out, g_out, params, *, mixer, init, cols, scratch, name):
    batch, lp, d = h3.shape
    nin = w_in.shape[1]
    nout = w_out.shape[0]
    tn = MXU_COLS
    assert nin % tn == 0 and d % tn == 0
    nblk = lp // (2 * CHUNK)
    resident = pl.Buffered(1)

    def pspec(shape):
        return pl.BlockSpec(shape, lambda p: (0,) * len(shape))

    return pl.pallas_call(
        functools.partial(_layer_kernel, mixer=mixer, init=init, nparams=len(params), cols=cols, tn=tn),
        grid=(nblk + 1,),
        in_specs=[
            pl.BlockSpec((batch, 2 * CHUNK, d), lambda p: (0, jnp.minimum(p, nblk - 1), 0)),
            pl.BlockSpec((batch, 2 * CHUNK, d), lambda p: (0, jnp.maximum(p - 1, 0), 0)),
            pspec((1, d)),
            pl.BlockSpec((d, nin), lambda p: (0, 0), pipeline_mode=resident),
            pl.BlockSpec((nout, d), lambda p: (0, 0), pipeline_mode=resident),
            pspec((1, d)),
        ] + [pspec(q.shape) for q in params],
        out_specs=pl.BlockSpec((batch, 2 * CHUNK, d), lambda p: (0, jnp.maximum(p - 1, 0), 0)),
        out_shape=jax.ShapeDtypeStruct((batch, lp, d), F32),
        scratch_shapes=[pltpu.VMEM((batch, CHUNK, nin), F32), pltpu.VMEM((batch, CHUNK, nin), F32),
                        pltpu.VMEM((batch, CHUNK, nout), BF16), pltpu.VMEM((batch, CHUNK, nout), BF16),
                        pltpu.VMEM((batch * CHUNK, d), BF16)] + scratch,
        compiler_params=pltpu.CompilerParams(
            dimension_semantics=("arbitrary",), vmem_limit_bytes=VMEM_LIMIT_BYTES),
        name=name,
    )(h3, h3, g_in.reshape(1, d), w_in, w_out, g_out.reshape(1, d), *params)


def _mixer_ab_init(gn_ref, wg_ref, ba_ref, bi_ref, lam_ref, cw_ref, cb_ref,
                   s_ref, xcar_ref, hcar_ref, a_slab, b_slab):
    s_ref[...] = jnp.zeros_like(s_ref)
    xcar_ref[...] = jnp.zeros_like(xcar_ref)
    hcar_ref[...] = jnp.zeros_like(hcar_ref)


def _mixer_ab(c, fill, q_ref, k_ref, v_ref, gate_ref, bx_ref, bg_ref, gn_ref, wg_ref, ba_ref, bi_ref,
              lam_ref, cw_ref, cb_ref, y_ref, s_ref, xcar_ref, hcar_ref, a_slab, b_slab):
    nbatch, _, width = bx_ref.shape
    ngrp = width // LANES
    scan_slots = CHUNK // SUBLANES
    fill.nslots = nbatch * RET_HEADS + ngrp + scan_slots

    row = lax.broadcasted_iota(jnp.int32, (CHUNK, 1), 0)
    idx = row.astype(F32)
    pos = (c * CHUNK + row - PAD).astype(F32)
    lane = lax.broadcasted_iota(jnp.int32, (1, RET_DK), 1)
    half = RET_DK // 2
    freq = jnp.exp((lane & (half - 1)).astype(F32) * (-math.log(ROPE_BASE) / half))
    ang = pos * freq
    cosv = jnp.cos(ang)
    sinv = jnp.where(lane < half, -1.0, 1.0) * jnp.sin(ang)
    ti = lax.broadcasted_iota(jnp.int32, (CHUNK, CHUNK), 0)
    si = lax.broadcasted_iota(jnp.int32, (CHUNK, CHUNK), 1)
    dist = jnp.abs(ti - si).astype(F32)

    def ret_body(b, carry):
        for h in range(RET_HEADS):
            log_g = math.log1p(-(2.0 ** (-5.0 - h)))
            qs = slice(h * RET_DK, (h + 1) * RET_DK)
            vs = slice(h * RET_DV, (h + 1) * RET_DV)
            qh = q_ref[b, :, qs]
            kh = k_ref[b, :, qs]
            qr = (qh * cosv + pltpu.roll(qh, half, 1) * sinv) * (RET_DK ** -0.5)
            kr = kh * cosv + pltpu.roll(kh, half, 1) * sinv
            vh = v_ref[b, :, vs].astype(BF16)
            scores = _dot_nt(qr.astype(BF16), kr.astype(BF16)) * jnp.exp(dist * log_g)
            o = _dot(scores.astype(BF16), vh)
            q_dec = qr * jnp.exp((idx + 1.0) * log_g)
            o = o + _dot(q_dec.astype(BF16), s_ref[b, h].astype(BF16))
            k_dec = kr * jnp.exp((CHUNK - 1.0 - idx) * log_g)
            kv = _dot(k_dec.T.astype(BF16), vh)
            s_ref[b, h] = math.exp(CHUNK * log_g) * s_ref[b, h] + kv
            oc = o - jnp.mean(o, axis=-1, keepdims=True)
            var = jnp.mean(oc * oc, axis=-1, keepdims=True)
            gt = gate_ref[b, :, vs]
            y_ref[b, :, vs] = (oc * lax.rsqrt(var + EPS) * gn_ref[:, vs]
                               * (gt * jax.nn.sigmoid(gt))).astype(y_ref.dtype)
            fill()
        return carry

    for b in range(nbatch):
        ret_body(b, 0)

    xb = bx_ref[...]
    xe = jnp.concatenate([xcar_ref[...], xb], axis=1)
    xc = cb_ref[...] + xb * cw_ref[CONV_WIDTH - 1:CONV_WIDTH, :]
    for s in range(1, CONV_WIDTH):
        xc = xc + pltpu.roll(xe, s, 1)[:, SUBLANES:, :] * cw_ref[CONV_WIDTH - 1 - s:CONV_WIDTH - s, :]
    xcar_ref[...] = xb[:, CHUNK - SUBLANES:, :]

    xc2 = xc.reshape(nbatch * CHUNK, width)
    xcb = xc2.astype(BF16)
    valid = jnp.logical_or(c > 0, jnp.logical_and(c == 0, row >= PAD))
    for p in range(ngrp):
        cs = slice(p * LANES, (p + 1) * LANES)
        g2 = _dot(xcb[:, cs], wg_ref[p])
        r = jax.nn.sigmoid(g2[:, :LANES] + ba_ref[:, cs])
        i = jax.nn.sigmoid(g2[:, LANES:] + bi_ref[:, cs])
        lam = lam_ref[:, cs]
        softplus_neg_lam = jnp.maximum(-lam, 0.0) + jnp.log1p(jnp.exp(-jnp.abs(lam)))
        a = jnp.exp(-LRU_C * r * softplus_neg_lam)
        bb = jnp.sqrt(1.0 - a * a) * (i * xc2[:, cs])
        for b in range(nbatch):
            rs = slice(b * CHUNK, (b + 1) * CHUNK)
            a_slab[b, p * SLAB_PITCH:p * SLAB_PITCH + CHUNK, :] = a[rs]
            b_slab[b, p * SLAB_PITCH:p * SLAB_PITCH + CHUNK, :] = jnp.where(valid, bb[rs], 0.0)
        fill()

    def step(t, hs):
        out = []
        for b in range(nbatch):
            ts = pl.ds(t, ngrp, stride=SLAB_PITCH)
            h = a_slab[b, ts, :] * hs[b] + b_slab[b, ts, :]
            b_slab[b, ts, :] = h
            out.append(h)
        return tuple(out)

    hs = tuple(hcar_ref[b] for b in range(nbatch))
    for t in range(CHUNK):
        hs = step(t, hs)
        if t % SUBLANES == SUBLANES - 1:
            fill()
    for b in range(nbatch):
        hcar_ref[b] = hs[b]
        hfull = jnp.concatenate(
            [b_slab[b, p * SLAB_PITCH:p * SLAB_PITCH + CHUNK, :] for p in range(ngrp)], axis=1)
        y_ref[b, :, RET_HEADS * RET_DV:] = (_gelu_tanh(bg_ref[b]) * hfull).astype(y_ref.dtype)


def _layer_ab(h3, g_in, w_in, w_out, g_out, ret_gn, wg, ba, bi, lam, conv_w, conv_b):
    batch = h3.shape[0]
    qk = RET_HEADS * RET_DK
    vw = RET_HEADS * RET_DV
    lw = lam.shape[-1]
    assert lw == SUBLANES * LANES
    params = [ret_gn.reshape(1, vw), wg, ba.reshape(1, lw), bi.reshape(1, lw), lam.reshape(1, lw),
              conv_w, conv_b.reshape(1, lw)]
    scratch = [
        pltpu.VMEM((batch, RET_HEADS, RET_DK, RET_DV), F32),
        pltpu.VMEM((batch, SUBLANES, lw), F32),
        pltpu.VMEM((batch, SUBLANES, LANES), F32),
        pltpu.VMEM((batch, SUBLANES * SLAB_PITCH, LANES), F32),
        pltpu.VMEM((batch, SUBLANES * SLAB_PITCH, LANES), F32),
    ]
    return _layer(h3, g_in, w_in, w_out, g_out, params, mixer=_mixer_ab, init=_mixer_ab_init,
                  cols=(qk, qk, vw, vw, lw, lw), scratch=scratch, name="layer_ab")


_HG_LEVELS = (8, 16, 32)


def _mixer_cd_init(arl_ref, aim_ref, ldt_ref, arl_t_ref, aim_t_ref, ldt_t_ref,
                   bre_ref, bim_ref, cre_ref, cim_ref, d_ref, gw_ref, gb_ref, gn_ref, lbl_ref,
                   bm_ref, ab_ref, hcar_ref, bu_slab, sel_ref, st_ref):
    def disc(ldt, arl, aim):
        dt = jnp.exp(ldt)
        a_re = -jnp.exp(arl)
        mag = jnp.exp(dt * a_re)
        return a_re, aim, mag * jnp.cos(dt * aim), mag * jnp.sin(dt * aim)

    _, _, t_re, t_im = disc(ldt_t_ref[...], arl_t_ref[...], aim_t_ref[...])
    ab_ref[0] = t_re
    ab_ref[1] = t_im
    a_re, a_im, ab_re, ab_im = disc(ldt_ref[...], arl_ref[...], aim_ref[...])
    den = a_re * a_re + a_im * a_im
    z_re = ((ab_re - 1.0) * a_re + ab_im * a_im) / den
    z_im = (ab_im * a_re - (ab_re - 1.0) * a_im) / den
    for jb in range(bre_ref.shape[0]):
        ls = slice(jb * S5_LANES, (jb + 1) * S5_LANES)
        zr = z_re[:, ls]
        zi = z_im[:, ls]
        bb_re = zr * bre_ref[jb] - zi * bim_ref[jb]
        bb_im = zr * bim_ref[jb] + zi * bre_ref[jb]
        bm_ref[jb] = jnp.concatenate([bb_re, bb_im], axis=1).astype(BF16)
    ri = lax.broadcasted_iota(jnp.int32, (2 * HG_DK, 2 * CHUNK), 0)
    ci = lax.broadcasted_iota(jnp.int32, (2 * HG_DK, 2 * CHUNK), 1)
    same_head = (ri >= HG_DK) == (ci >= CHUNK)
    for s in range(SUBLANES):
        sel_ref[s] = jnp.logical_and(same_head, (ci & (SUBLANES - 1)) == s).astype(BF16)
    hcar_ref[...] = jnp.zeros_like(hcar_ref)
    st_ref[...] = jnp.zeros_like(st_ref)


def _mixer_cd(c, fill, u_ref, q_ref, f_ref, i_ref, g_ref,
              arl_ref, aim_ref, ldt_ref, arl_t_ref, aim_t_ref, ldt_t_ref,
              bre_ref, bim_ref, cre_ref, cim_ref, d_ref, gw_ref, gb_ref, gn_ref, lbl_ref, y_ref,
              bm_ref, ab_ref, hcar_ref, bu_slab, sel_ref, st_ref, *, layer):
    del c
    nbatch, _, s5w = u_ref.shape
    ncol = bre_ref.shape[0]
    cin = bre_ref.shape[1]
    nslab = ab_ref.shape[1]
    nhalf = nslab // SUBLANES
    gpc = S5_LANES // LANES
    hw = HG_HEADS * HG_DK
    pair = 2 * HG_DK
    scan_slots = CHUNK // SUBLANES
    hg_slots = len(_HG_LEVELS) + SUBLANES // 2
    fill.nslots = 2 * ncol + scan_slots + 1 + nbatch * hg_slots

    u2 = u_ref[...].reshape(nbatch * CHUNK, s5w)
    ub = u2.astype(BF16)
    for jb in range(ncol):
        bu = _dot(ub[:, jb * cin:(jb + 1) * cin], bm_ref[jb])
        for part in range(2):
            for gi in range(gpc):
                slab = part * nslab + jb * gpc + gi
                col = part * S5_LANES + gi * LANES
                for b in range(nbatch):
                    bu_slab[b, slab * SLAB_PITCH:slab * SLAB_PITCH + CHUNK, :] = (
                        bu[b * CHUNK:(b + 1) * CHUNK, col:col + LANES])
        fill()

    a_re = [ab_ref[0, hf * SUBLANES:(hf + 1) * SUBLANES, :] for hf in range(nhalf)]
    a_im = [ab_ref[1, hf * SUBLANES:(hf + 1) * SUBLANES, :] for hf in range(nhalf)]

    def step(t, hs):
        out = []
        for b in range(nbatch):
            for hf in range(nhalf):
                h_re, h_im = hs[2 * (b * nhalf + hf)], hs[2 * (b * nhalf + hf) + 1]
                ts_re = pl.ds(hf * SUBLANES * SLAB_PITCH + t, SUBLANES, stride=SLAB_PITCH)
                ts_im = pl.ds((nslab + hf * SUBLANES) * SLAB_PITCH + t, SUBLANES, stride=SLAB_PITCH)
                n_re = a_re[hf] * h_re - a_im[hf] * h_im + bu_slab[b, ts_re, :]
                n_im = a_re[hf] * h_im + a_im[hf] * h_re + bu_slab[b, ts_im, :]
                bu_slab[b, ts_re, :] = n_re
                bu_slab[b, ts_im, :] = n_im
                out += [n_re, n_im]
        return tuple(out)

    init = []
    for b in range(nbatch):
        for hf in range(nhalf):
            init += [hcar_ref[b, 0, hf], hcar_ref[b, 1, hf]]
    hs = tuple(init)
    for t in range(CHUNK):
        hs = step(t, hs)
        if t % SUBLANES == SUBLANES - 1:
            fill()
    for b in range(nbatch):
        for hf in range(nhalf):
            hcar_ref[b, 0, hf] = hs[2 * (b * nhalf + hf)]
            hcar_ref[b, 1, hf] = hs[2 * (b * nhalf + hf) + 1]

    def states(part, jb):
        return jnp.concatenate(
            [jnp.concatenate(
                [bu_slab[b, (part * nslab + jb * gpc + gi) * SLAB_PITCH:
                         (part * nslab + jb * gpc + gi) * SLAB_PITCH + CHUNK, :] for gi in range(gpc)], axis=1)
             for b in range(nbatch)], axis=0).astype(BF16)

    ys = []
    for jb in range(ncol):
        cs = slice(jb * cin, (jb + 1) * cin)
        y = _dot(states(0, jb), cre_ref[jb]) - _dot(states(1, jb), cim_ref[jb]) + d_ref[:, cs] * u2[:, cs]
        ys.append(_gelu_tanh(y))
        fill()
    yg = jnp.concatenate(ys, axis=1)
    yc = yg * jax.nn.sigmoid(_dot(yg.astype(BF16), gw_ref[...]) + gb_ref[...])
    y_ref[:, :, :s5w] = yc.reshape(nbatch, CHUNK, s5w).astype(y_ref.dtype)
    fill()

    logits = lbl_ref[...]
    pexp = jnp.exp(logits - jnp.max(logits, axis=0, keepdims=True))
    psm = pexp / jnp.sum(pexp, axis=0, keepdims=True)
    lb = jnp.zeros_like(psm[0:1, :])
    for l in range(layer):
        lb = lb + psm[l:l + 1, :]

    ti = lax.broadcasted_iota(jnp.int32, (CHUNK, CHUNK), 0)
    si = lax.broadcasted_iota(jnp.int32, (CHUNK, CHUNK), 1)
    tril = (ti >= si).astype(BF16)
    tp = lax.broadcasted_iota(jnp.int32, (CHUNK, 2 * CHUNK), 0)
    sp = lax.broadcasted_iota(jnp.int32, (CHUNK, 2 * CHUNK), 1) & (CHUNK - 1)
    diag_mask = jnp.logical_and((tp >> 3) == (sp >> 3), sp <= tp)
    nvr = CHUNK // SUBLANES

    def both_heads(x, p):
        xa = x[:, p * pair:p * pair + HG_DK]
        xb_ = x[:, p * pair + HG_DK:(p + 1) * pair]
        zero = jnp.zeros_like(xa)
        return jnp.concatenate([jnp.concatenate([xa, zero], axis=1),
                                jnp.concatenate([zero, xb_], axis=1)], axis=0)

    def hg_body(b, carry):
        f = lb + (1.0 - lb) * jax.nn.sigmoid(f_ref[b])
        logf = jnp.log(f)
        kk = 1.0 - f
        lf_hi = logf.astype(BF16)
        lf_lo = (logf - lf_hi.astype(F32)).astype(BF16)
        cum = _dot(tril, lf_hi) + _dot(tril, lf_lo)
        total = cum[CHUNK - 1:CHUNK, :]
        q = q_ref[b]
        iv = i_ref[b]
        ivb = iv.astype(BF16)
        q_in = (q * jnp.exp(cum)).astype(BF16)
        k_dec = kk * jnp.exp(total - cum)
        dec = jnp.exp(total)

        att = [jnp.zeros((CHUNK, 2 * CHUNK), F32) for _ in range(HG_HEADS // 2)]
        ends = [cum[v * SUBLANES + SUBLANES - 1:(v + 1) * SUBLANES, :] for v in range(nvr)]
        zeros8 = jnp.zeros((SUBLANES, hw), F32)
        for n in _HG_LEVELS:
            per = n // SUBLANES
            qparts, kparts = [], []
            for v in range(nvr):
                blk = v // per
                vs = slice(v * SUBLANES, (v + 1) * SUBLANES)
                if blk % 2 == 1:
                    qparts.append(q[vs] * jnp.exp(cum[vs] - ends[blk * per - 1]))
                    kparts.append(zeros8)
                else:
                    qparts.append(zeros8)
                    kparts.append(kk[vs] * jnp.exp(ends[blk * per + per - 1] - cum[vs]))
            q_t = jnp.concatenate(qparts, axis=0).astype(BF16)
            k_t = jnp.concatenate(kparts, axis=0).astype(BF16)
            shift = n.bit_length()
            for p in range(HG_HEADS // 2):
                a_n = _dot_nt(q_t[:, p * pair:(p + 1) * pair], both_heads(k_t, p))
                if 2 * n < CHUNK:
                    a_n = jnp.where((tp >> shift) == (sp >> shift), a_n, 0.0)
                att[p] = att[p] + a_n
            fill()
        q3 = q.reshape(nvr, SUBLANES, hw)
        k3 = kk.reshape(nvr, SUBLANES, hw)
        c3 = cum.reshape(nvr, SUBLANES, hw)
        dsum = [jnp.zeros((CHUNK, 2 * CHUNK), F32) for _ in range(HG_HEADS // 2)]
        for s in range(SUBLANES):
            w = q3 * k3[:, s:s + 1, :] * jnp.exp(jnp.minimum(c3 - c3[:, s:s + 1, :], 0.0))
            wb = w.reshape(CHUNK, hw).astype(BF16)
            for p in range(HG_HEADS // 2):
                dsum[p] = dsum[p] + _dot(wb[:, p * pair:(p + 1) * pair], sel_ref[s])
            if s % 2 == 1:
                fill()
        for p in range(HG_HEADS // 2):
            a_all = (att[p] + jnp.where(diag_mask, dsum[p], 0.0)).astype(BF16)
            o_pair = _dot(a_all, both_heads(ivb, p))
            for hh in range(2):
                h = 2 * p + hh
                hs_ = slice(h * HG_DK, (h + 1) * HG_DK)
                st = st_ref[b, h]
                oh = o_pair[:, hh * HG_DK:(hh + 1) * HG_DK] + _dot_nt(q_in[:, hs_], st.astype(BF16))
                st_ref[b, h] = dec[:, hs_] * st + _dot(iv[:, hs_].T.astype(BF16), k_dec[:, hs_].astype(BF16))
                ms = jnp.mean(oh * oh, axis=-1, keepdims=True)
                gt = g_ref[b, :, hs_]
                y_ref[b, :, s5w + h * HG_DK:s5w + (h + 1) * HG_DK] = (
                    oh * lax.rsqrt(ms + EPS) * gn_ref[:, hs_] * (gt * jax.nn.sigmoid(gt))).astype(y_ref.dtype)
        return carry

    for b in range(nbatch):
        hg_body(b, 0)


def _block_diag(t):
    n, g, r, c = t.shape
    out = jnp.zeros((n, g, r, g, c), t.dtype)
    for gi in range(g):
        out = out.at[:, gi, :, gi, :].set(t[:, gi])
    return out.reshape(n, g * r, g * c)


def _layer_cd(h3, g_in, w_in, w_out, g_out, a_re_log, a_im, b_re, b_im, c_re, c_im, d, log_dt, glu_w, glu_b,
              hg_gn, lb_logits, *, layer):
    batch = h3.shape[0]
    groups, state = a_re_log.shape
    s5w = groups * S5_GROUP
    nstate = groups * state
    gpc = S5_LANES // state
    ncol = groups // gpc
    cin = gpc * S5_GROUP
    hw = HG_HEADS * HG_DK
    nslab = nstate // LANES
    assert s5w == hw and nslab % SUBLANES == 0

    bre = _block_diag(jnp.transpose(b_re.reshape(ncol, gpc, state, S5_GROUP), (0, 1, 3, 2)))
    bim = _block_diag(jnp.transpose(b_im.reshape(ncol, gpc, state, S5_GROUP), (0, 1, 3, 2)))
    cre = _block_diag(jnp.transpose(c_re.reshape(ncol, gpc, S5_GROUP, state), (0, 1, 3, 2))).astype(BF16)
    cim = _block_diag(jnp.transpose(c_im.reshape(ncol, gpc, S5_GROUP, state), (0, 1, 3, 2))).astype(BF16)
    ldt = jnp.repeat(log_dt, state)
    params = [a_re_log.reshape(1, nstate), a_im.reshape(1, nstate), ldt.reshape(1, nstate),
              a_re_log.reshape(nslab, LANES), a_im.reshape(nslab, LANES), ldt.reshape(nslab, LANES),
              bre, bim, cre, cim, d.reshape(1, s5w), glu_w.astype(BF16), glu_b.reshape(1, s5w),
              hg_gn.reshape(1, hw), lb_logits]
    scratch = [
        pltpu.VMEM((ncol, cin, 2 * S5_LANES), BF16),
        pltpu.VMEM((2, nslab, LANES), F32),
        pltpu.VMEM((batch, 2, nslab // SUBLANES, SUBLANES, LANES), F32),
        pltpu.VMEM((batch, 2 * nslab * SLAB_PITCH, LANES), F32),
        pltpu.VMEM((SUBLANES, 2 * HG_DK, 2 * CHUNK), BF16),
        pltpu.VMEM((batch, HG_HEADS, HG_DK, HG_DK), F32),
    ]
    return _layer(h3, g_in, w_in, w_out, g_out, params, mixer=functools.partial(_mixer_cd, layer=layer),
                  init=_mixer_cd_init, cols=(s5w,) * 5, scratch=scratch, name="layer_cd")


def _pack_lru_gates(wa, wi):
    nblk, bd, _ = wa.shape
    per = LANES // bd
    wa_bd = _block_diag(wa.reshape(nblk // per, per, bd, bd))
    wi_bd = _block_diag(wi.reshape(nblk // per, per, bd, bd))
    return jnp.concatenate([wa_bd, wi_bd], axis=2).astype(BF16)


def kernel(x, meta, w_in_ab, w_out_ab, ret_gn, rg_wa, rg_ba, rg_wi, rg_bi, rg_lam, rg_conv_w, rg_conv_b,
           w_in_cd, w_out_cd, s5_a_re_log, s5_a_im, s5_b_re, s5_b_im, s5_c_re, s5_c_im, s5_d, s5_log_dt,
           s5_glu_w, s5_glu_b, hg_gn, hg_lb_logits, norm_g, mlp_w1, mlp_w2):
    batch, seq, d = x.shape
    depth = norm_g.shape[0]
    used = PAD + N_META + seq
    assert used % CHUNK == 0
    lp = -(-used // (2 * CHUNK)) * (2 * CHUNK)
    m = batch * lp

    h = jnp.concatenate([jnp.zeros((batch, PAD, d), x.dtype),
                         jnp.broadcast_to(meta[None].astype(x.dtype), (batch, N_META, d)), x,
                         jnp.zeros((batch, lp - used, d), x.dtype)], axis=1)
    for l in range(depth):
        jdx = l // 2
        if l % 2 == 0:
            h = _layer_ab(h, norm_g[l, 0], w_in_ab[jdx].astype(BF16), w_out_ab[jdx].astype(BF16), norm_g[l, 1],
                          ret_gn[jdx], _pack_lru_gates(rg_wa[jdx], rg_wi[jdx]), rg_ba[jdx], rg_bi[jdx],
                          rg_lam[jdx], rg_conv_w[jdx], rg_conv_b[jdx])
        else:
            h = _layer_cd(h, norm_g[l, 0], w_in_cd[jdx].astype(BF16), w_out_cd[jdx].astype(BF16), norm_g[l, 1],
                          s5_a_re_log[jdx], s5_a_im[jdx], s5_b_re[jdx], s5_b_im[jdx], s5_c_re[jdx], s5_c_im[jdx],
                          s5_d[jdx], s5_log_dt[jdx], s5_glu_w[jdx], s5_glu_b[jdx], hg_gn[jdx], hg_lb_logits,
                          layer=l)
        h = _mlp(h.reshape(m, d), norm_g[l, 2], mlp_w1[l].astype(BF16), mlp_w2[l].astype(BF16), norm_g[l, 3],
                 tm=_largest_divisor(lp, 528), tn=1024, rows_per_batch=lp).reshape(batch, lp, d)

    return h[:, PAD + N_META:used]
```

```python
import functools
import math

import jax
import jax.numpy as jnp
from jax import lax
from jax.experimental import pallas as pl
from jax.experimental.pallas import tpu as pltpu

F32 = jnp.float32
BF16 = jnp.bfloat16

CHUNK = 64
N_META = 16
PAD = CHUNK - N_META
EPS = 1e-6

RET_HEADS = 4
RET_DK = 128
RET_DV = 256
ROPE_BASE = 10000.0
LRU_C = 8.0
CONV_WIDTH = 4
S5_GROUP = 16
S5_STATE = 64
S5_LANES = 512
HG_HEADS = 4
HG_DK = 128
LANES = 128
SUBLANES = 8
MXU_COLS = 256
SLAB_PITCH = CHUNK + SUBLANES
VMEM_LIMIT_BYTES = 56 * 1024 * 1024


def _largest_divisor(n, cap):
    return max(d for d in range(1, cap + 1) if n % d == 0)


def _gelu_tanh(x):
    return 0.5 * x * (1.0 + jnp.tanh(0.7978845608028654 * (x + 0.044715 * x * x * x)))


def _sigmoid(x):
    return 0.5 * jnp.tanh(0.5 * x) + 0.5


def _silu(x):
    hx = 0.5 * x
    return hx + hx * jnp.tanh(hx)


def _sqrt_nonneg(x):
    return jnp.where(x > 0.0, x * lax.rsqrt(x), 0.0)


def _dot(a, b):
    return jnp.dot(a, b, preferred_element_type=F32)


def _dot_nt(a, b):
    return lax.dot_general(a, b, (((1,), (1,)), ((), ())), preferred_element_type=F32)


def _keep_rows(out, tiles_per_batch):
    first_tile = lax.rem(pl.program_id(0), tiles_per_batch) == 0
    row = lax.broadcasted_iota(jnp.int32, (out.shape[0], 1), 0)
    keep = jnp.logical_or(row >= PAD, jnp.logical_not(first_tile))
    return jnp.where(keep, out, 0.0)


def _mlp_rows(x, g_in_ref, w1_ref, w2_ref, g_out_ref, tn):
    ms = jnp.mean(x * x, axis=-1, keepdims=True)
    hn = (x * lax.rsqrt(ms + EPS) * g_in_ref[...]).astype(BF16)
    acc = jnp.zeros(x.shape, F32)
    for j in range(w1_ref.shape[1] // tn):
        a = jnp.maximum(_dot(hn, w1_ref[:, j * tn:(j + 1) * tn]), 0.0)
        acc = acc + _dot((a * a).astype(BF16), w2_ref[j * tn:(j + 1) * tn, :])
    ms = jnp.mean(acc * acc, axis=-1, keepdims=True)
    return x + acc * lax.rsqrt(ms + EPS) * g_out_ref[...]


def _mlp_kernel(h_ref, g_in_ref, w1_ref, w2_ref, g_out_ref, o_ref, *, tiles_per_batch, tn):
    o_ref[...] = _keep_rows(_mlp_rows(h_ref[...], g_in_ref, w1_ref, w2_ref, g_out_ref, tn), tiles_per_batch)


def _mlp_frames_kernel(h_ref, g_in_ref, w1_ref, w2_ref, g_out_ref, o_ref, *, tn):
    o_ref[...] = _mlp_rows(h_ref[...], g_in_ref, w1_ref, w2_ref, g_out_ref, tn)


def _mlp_frames(h3, g_in, w1, w2, g_out, *, first, count, tm, tn):
    batch, lp, d = h3.shape
    f = w1.shape[1]
    resident = pl.Buffered(1)
    return pl.pallas_call(
        functools.partial(_mlp_frames_kernel, tn=tn),
        grid=(batch, count // tm),
        in_specs=[
            pl.BlockSpec((pl.Element(tm), pl.Element(d)),
                         lambda b, i: (pl.multiple_of(b * lp + first + i * tm, SUBLANES), 0)),
            pl.BlockSpec((1, d), lambda b, i: (0, 0)),
            pl.BlockSpec((d, f), lambda b, i: (0, 0), pipeline_mode=resident),
            pl.BlockSpec((f, d), lambda b, i: (0, 0), pipeline_mode=resident),
            pl.BlockSpec((1, d), lambda b, i: (0, 0)),
        ],
        out_specs=pl.BlockSpec((None, tm, d), lambda b, i: (b, i, 0)),
        out_shape=jax.ShapeDtypeStruct((batch, count, d), F32),
        compiler_params=pltpu.CompilerParams(
            dimension_semantics=("arbitrary", "arbitrary"), vmem_limit_bytes=VMEM_LIMIT_BYTES),
        name="mlp_frames",
    )(h3.reshape(batch * lp, d), g_in.reshape(1, d), w1, w2, g_out.reshape(1, d))


def _mlp(h, g_in, w1, w2, g_out, *, tm, tn, rows_per_batch):
    m, d = h.shape
    f = w1.shape[1]
    resident = pl.Buffered(1)
    return pl.pallas_call(
        functools.partial(_mlp_kernel, tiles_per_batch=rows_per_batch // tm, tn=tn),
        grid=(m // tm,),
        in_specs=[
            pl.BlockSpec((tm, d), lambda i: (i, 0)),
            pl.BlockSpec((1, d), lambda i: (0, 0)),
            pl.BlockSpec((d, f), lambda i: (0, 0), pipeline_mode=resident),
            pl.BlockSpec((f, d), lambda i: (0, 0), pipeline_mode=resident),
            pl.BlockSpec((1, d), lambda i: (0, 0)),
        ],
        out_specs=pl.BlockSpec((tm, d), lambda i: (i, 0)),
        out_shape=jax.ShapeDtypeStruct((m, d), F32),
        compiler_params=pltpu.CompilerParams(
            dimension_semantics=("arbitrary",), vmem_limit_bytes=VMEM_LIMIT_BYTES),
        name="mlp",
    )(h, g_in.reshape(1, d), w1, w2, g_out.reshape(1, d))


class _Filler:
    def __init__(self, pieces):
        self._pieces = list(pieces)
        self._total = len(self._pieces)
        self._done = 0
        self._slot = 0
        self.nslots = 1

    def __call__(self):
        self._slot += 1
        target = min(self._total, -(-self._total * self._slot // self.nslots))
        while self._done < target:
            self._pieces[self._done]()
            self._done += 1

    def drain(self):
        while self._done < self._total:
            self._pieces[self._done]()
            self._done += 1


def _layer_kernel(ha_ref, hc_ref, g_in_ref, win_ref, wout_ref, g_out_ref, *rest, mixer, init, nparams, cols, tn):
    params = rest[:nparams]
    o_ref = rest[nparams]
    z_refs = rest[nparams + 1:nparams + 3]
    y_refs = rest[nparams + 3:nparams + 5]
    hn_ref = rest[nparams + 5]
    scratch = rest[nparams + 6:]
    nbatch, _, d = ha_ref.shape
    p = pl.program_id(0)

    @pl.when(p == 0)
    def _():
        z_refs[1][...] = jnp.zeros_like(z_refs[1])
        y_refs[0][...] = jnp.zeros_like(y_refs[0])
        init(*params, *scratch)

    row = lax.broadcasted_iota(jnp.int32, (1, CHUNK, 1), 1)
    rows = nbatch * CHUNK
    for e in range(2):
        s = 2 * p + e
        rs = slice(e * CHUNK, (e + 1) * CHUNK)

        pieces = []
        m_parts = []

        def out_piece(j, e=e, m_parts=m_parts):
            cs = slice(j * tn, (j + 1) * tn)
            y_in = y_refs[e][...].reshape(rows, y_refs[e].shape[2])
            m_parts.append(_dot(y_in, wout_ref[:, cs]))

        def out_finish(e=e, s=s, rs=rs, m_parts=m_parts):
            m = jnp.concatenate(m_parts, axis=1)
            ms = jnp.mean(m * m, axis=-1, keepdims=True)
            xr = hc_ref[:, rs, :].reshape(rows, d)
            out = (xr + m * lax.rsqrt(ms + EPS) * g_out_ref[...]).reshape(nbatch, CHUNK, d)
            o_ref[:, rs, :] = jnp.where(jnp.logical_or(row >= PAD, s != 2), out, 0.0)

        def in_start(rs=rs):
            x = ha_ref[:, rs, :].reshape(rows, d)
            ms = jnp.mean(x * x, axis=-1, keepdims=True)
            hn_ref[...] = (x * lax.rsqrt(ms + EPS) * g_in_ref[...]).astype(BF16)

        def in_piece(j, e=e):
            cs = slice(j * tn, (j + 1) * tn)
            z_refs[e][:, :, cs] = _dot(hn_ref[...], win_ref[:, cs]).reshape(nbatch, CHUNK, tn)

        pieces += [functools.partial(out_piece, j) for j in range(d // tn)] + [out_finish, in_start]
        pieces += [functools.partial(in_piece, j) for j in range(win_ref.shape[1] // tn)]
        fill = _Filler(pieces)

        views = []
        off = 0
        for width in cols:
            views.append(z_refs[1 - e].at[:, :, off:off + width])
            off += width
        mixer(s - 1, fill, *views, *params, y_refs[1 - e], *scratch)
        fill.drain()


def _layer(h3, g_in, w_in, w_out, g_out, params, *, mixer, init, cols, scratch, name, tn):
    batch, lp, d = h3.shape
    nin = w_in.shape[1]
    nout = w_out.shape[0]
    assert nin % tn == 0 and d % tn == 0
    nblk = lp // (2 * CHUNK)
    resident = pl.Buffered(1)

    def pspec(shape):
        return pl.BlockSpec(shape, lambda p: (0,) * len(shape))

    return pl.pallas_call(
        functools.partial(_layer_kernel, mixer=mixer, init=init, nparams=len(params), cols=cols, tn=tn),
        grid=(nblk + 1,),
        in_specs=[
            pl.BlockSpec((batch, 2 * CHUNK, d), lambda p: (0, jnp.minimum(p, nblk - 1), 0)),
            pl.BlockSpec((batch, 2 * CHUNK, d), lambda p: (0, jnp.maximum(p - 1, 0), 0)),
            pspec((1, d)),
            pl.BlockSpec((d, nin), lambda p: (0, 0), pipeline_mode=resident),
            pl.BlockSpec((nout, d), lambda p: (0, 0), pipeline_mode=resident),
            pspec((1, d)),
        ] + [pspec(q.shape) for q in params],
        out_specs=pl.BlockSpec((batch, 2 * CHUNK, d), lambda p: (0, jnp.maximum(p - 1, 0), 0)),
        out_shape=jax.ShapeDtypeStruct((batch, lp, d), F32),
        scratch_shapes=[pltpu.VMEM((batch, CHUNK, nin), F32), pltpu.VMEM((batch, CHUNK, nin), F32),
                        pltpu.VMEM((batch, CHUNK, nout), BF16), pltpu.VMEM((batch, CHUNK, nout), BF16),
                        pltpu.VMEM((batch * CHUNK, d), BF16)] + scratch,
        compiler_params=pltpu.CompilerParams(
            dimension_semantics=("arbitrary",), vmem_limit_bytes=VMEM_LIMIT_BYTES),
        name=name,
    )(h3, h3, g_in.reshape(1, d), w_in, w_out, g_out.reshape(1, d), *params)


def _mixer_ab_init(gn_ref, wg_ref, ba_ref, bi_ref, lam_ref, cw_ref, cb_ref,
                   s_ref, xcar_ref, hcar_ref, a_slab, b_slab):
    s_ref[...] = jnp.zeros_like(s_ref)
    xcar_ref[...] = jnp.zeros_like(xcar_ref)
    hcar_ref[...] = jnp.zeros_like(hcar_ref)


def _mixer_ab(c, fill, q_ref, k_ref, v_ref, gate_ref, bx_ref, bg_ref, gn_ref, wg_ref, ba_ref, bi_ref,
              lam_ref, cw_ref, cb_ref, y_ref, s_ref, xcar_ref, hcar_ref, a_slab, b_slab):
    nbatch, _, width = bx_ref.shape
    ngrp = width // LANES
    scan_slots = CHUNK // SUBLANES
    fill.nslots = nbatch * RET_HEADS + ngrp + scan_slots

    row = lax.broadcasted_iota(jnp.int32, (CHUNK, 1), 0)
    idx = row.astype(F32)
    pos = (c * CHUNK + row - PAD).astype(F32)
    lane = lax.broadcasted_iota(jnp.int32, (1, RET_DK), 1)
    half = RET_DK // 2
    freq = jnp.exp((lane & (half - 1)).astype(F32) * (-math.log(ROPE_BASE) / half))
    ang = pos * freq
    cosv = jnp.cos(ang)
    sinv = jnp.where(lane < half, -1.0, 1.0) * jnp.sin(ang)
    ti = lax.broadcasted_iota(jnp.int32, (CHUNK, CHUNK), 0)
    si = lax.broadcasted_iota(jnp.int32, (CHUNK, CHUNK), 1)
    dist = jnp.abs(ti - si).astype(F32)

    def ret_body(b, carry):
        for h in range(RET_HEADS):
            log_g = math.log1p(-(2.0 ** (-5.0 - h)))
            qs = slice(h * RET_DK, (h + 1) * RET_DK)
            vs = slice(h * RET_DV, (h + 1) * RET_DV)
            qh = q_ref[b, :, qs]
            kh = k_ref[b, :, qs]
            qr = (qh * cosv + pltpu.roll(qh, half, 1) * sinv) * (RET_DK ** -0.5)
            kr = kh * cosv + pltpu.roll(kh, half, 1) * sinv
            vh = v_ref[b, :, vs].astype(BF16)
            scores = _dot_nt(qr.astype(BF16), kr.astype(BF16)) * jnp.exp(dist * log_g)
            o = _dot(scores.astype(BF16), vh)
            q_dec = qr * jnp.exp((idx + 1.0) * log_g)
            o = o + _dot(q_dec.astype(BF16), s_ref[b, h].astype(BF16))
            k_dec = kr * jnp.exp((CHUNK - 1.0 - idx) * log_g)
            kv = _dot(k_dec.T.astype(BF16), vh)
            s_ref[b, h] = math.exp(CHUNK * log_g) * s_ref[b, h] + kv
            oc = o - jnp.mean(o, axis=-1, keepdims=True)
            var = jnp.mean(oc * oc, axis=-1, keepdims=True)
            gt = gate_ref[b, :, vs]
            y_ref[b, :, vs] = (oc * lax.rsqrt(var + EPS) * gn_ref[:, vs]
                               * _silu(gt)).astype(y_ref.dtype)
            fill()
        return carry

    for b in range(nbatch):
        ret_body(b, 0)

    xb = bx_ref[...]
    xe = jnp.concatenate([xcar_ref[...], xb], axis=1)
    xc = cb_ref[...] + xb * cw_ref[CONV_WIDTH - 1:CONV_WIDTH, :]
    for s in range(1, CONV_WIDTH):
        xc = xc + pltpu.roll(xe, s, 1)[:, SUBLANES:, :] * cw_ref[CONV_WIDTH - 1 - s:CONV_WIDTH - s, :]
    xcar_ref[...] = xb[:, CHUNK - SUBLANES:, :]

    xc2 = xc.reshape(nbatch * CHUNK, width)
    xcb = xc2.astype(BF16)
    valid = jnp.logical_or(c > 0, jnp.logical_and(c == 0, row >= PAD))
    for p in range(ngrp):
        cs = slice(p * LANES, (p + 1) * LANES)
        g2 = _dot(xcb[:, cs], wg_ref[p])
        r = _sigmoid(g2[:, :LANES] + ba_ref[:, cs])
        i = _sigmoid(g2[:, LANES:] + bi_ref[:, cs])
        lam = lam_ref[:, cs]
        softplus_neg_lam = jnp.maximum(-lam, 0.0) + jnp.log1p(jnp.exp(-jnp.abs(lam)))
        a = jnp.exp(-LRU_C * r * softplus_neg_lam)
        bb = _sqrt_nonneg(1.0 - a * a) * (i * xc2[:, cs])
        for b in range(nbatch):
            rs = slice(b * CHUNK, (b + 1) * CHUNK)
            a_slab[b, p * SLAB_PITCH:p * SLAB_PITCH + CHUNK, :] = a[rs]
            b_slab[b, p * SLAB_PITCH:p * SLAB_PITCH + CHUNK, :] = jnp.where(valid, bb[rs], 0.0)
        fill()

    def step(t, hs):
        out = []
        for b in range(nbatch):
            ts = pl.ds(t, ngrp, stride=SLAB_PITCH)
            h = a_slab[b, ts, :] * hs[b] + b_slab[b, ts, :]
            b_slab[b, ts, :] = h
            out.append(h)
        return tuple(out)

    hs = tuple(hcar_ref[b] for b in range(nbatch))
    for t in range(CHUNK):
        hs = step(t, hs)
        if t % SUBLANES == SUBLANES - 1:
            fill()
    for b in range(nbatch):
        hcar_ref[b] = hs[b]
        hfull = jnp.concatenate(
            [b_slab[b, p * SLAB_PITCH:p * SLAB_PITCH + CHUNK, :] for p in range(ngrp)], axis=1)
        y_ref[b, :, RET_HEADS * RET_DV:] = (_gelu_tanh(bg_ref[b]) * hfull).astype(y_ref.dtype)


def _layer_ab(h3, g_in, w_in, w_out, g_out, ret_gn, wg, ba, bi, lam, conv_w, conv_b):
    batch = h3.shape[0]
    qk = RET_HEADS * RET_DK
    vw = RET_HEADS * RET_DV
    lw = lam.shape[-1]
    assert lw == SUBLANES * LANES
    params = [ret_gn.reshape(1, vw), wg, ba.reshape(1, lw), bi.reshape(1, lw), lam.reshape(1, lw),
              conv_w, conv_b.reshape(1, lw)]
    scratch = [
        pltpu.VMEM((batch, RET_HEADS, RET_DK, RET_DV), F32),
        pltpu.VMEM((batch, SUBLANES, lw), F32),
        pltpu.VMEM((batch, SUBLANES, LANES), F32),
        pltpu.VMEM((batch, SUBLANES * SLAB_PITCH, LANES), F32),
        pltpu.VMEM((batch, SUBLANES * SLAB_PITCH, LANES), F32),
    ]
    return _layer(h3, g_in, w_in, w_out, g_out, params, mixer=_mixer_ab, init=_mixer_ab_init,
                  cols=(qk, qk, vw, vw, lw, lw), scratch=scratch, name="layer_ab", tn=2 * MXU_COLS)


_HG_LEVELS = (8, 16, 32)


def _mixer_cd_init(arl_ref, aim_ref, ldt_ref, arl_t_ref, aim_t_ref, ldt_t_ref,
                   bre_ref, bim_ref, cre_ref, cim_ref, d_ref, gw_ref, gb_ref, gn_ref, lbl_ref,
                   bm_ref, ab_ref, hcar_ref, bu_slab, sel_ref, st_ref):
    def disc(ldt, arl, aim):
        dt = jnp.exp(ldt)
        a_re = -jnp.exp(arl)
        mag = jnp.exp(dt * a_re)
        return a_re, aim, mag * jnp.cos(dt * aim), mag * jnp.sin(dt * aim)

    _, _, t_re, t_im = disc(ldt_t_ref[...], arl_t_ref[...], aim_t_ref[...])
    ab_ref[0] = t_re
    ab_ref[1] = t_im
    a_re, a_im, ab_re, ab_im = disc(ldt_ref[...], arl_ref[...], aim_ref[...])
    den = a_re * a_re + a_im * a_im
    z_re = ((ab_re - 1.0) * a_re + ab_im * a_im) / den
    z_im = (ab_im * a_re - (ab_re - 1.0) * a_im) / den
    for jb in range(bre_ref.shape[0]):
        ls = slice(jb * S5_LANES, (jb + 1) * S5_LANES)
        zr = z_re[:, ls]
        zi = z_im[:, ls]
        bb_re = zr * bre_ref[jb] - zi * bim_ref[jb]
        bb_im = zr * bim_ref[jb] + zi * bre_ref[jb]
        bm_ref[jb] = jnp.concatenate([bb_re, bb_im], axis=1).astype(BF16)
    ri = lax.broadcasted_iota(jnp.int32, (2 * HG_DK, 2 * CHUNK), 0)
    ci = lax.broadcasted_iota(jnp.int32, (2 * HG_DK, 2 * CHUNK), 1)
    same_head = (ri >= HG_DK) == (ci >= CHUNK)
    for s in range(SUBLANES):
        sel_ref[s] = jnp.logical_and(same_head, (ci & (SUBLANES - 1)) == s).astype(BF16)
    hcar_ref[...] = jnp.zeros_like(hcar_ref)
    st_ref[...] = jnp.zeros_like(st_ref)


def _mixer_cd(c, fill, u_ref, q_ref, f_ref, i_ref, g_ref,
              arl_ref, aim_ref, ldt_ref, arl_t_ref, aim_t_ref, ldt_t_ref,
              bre_ref, bim_ref, cre_ref, cim_ref, d_ref, gw_ref, gb_ref, gn_ref, lbl_ref, y_ref,
              bm_ref, ab_ref, hcar_ref, bu_slab, sel_ref, st_ref, *, layer):
    del c
    nbatch, _, s5w = u_ref.shape
    ncol = bre_ref.shape[0]
    cin = bre_ref.shape[1]
    nslab = ab_ref.shape[1]
    nhalf = nslab // SUBLANES
    gpc = S5_LANES // LANES
    hw = HG_HEADS * HG_DK
    pair = 2 * HG_DK
    scan_slots = CHUNK // SUBLANES
    hg_slots = len(_HG_LEVELS) + SUBLANES // 2
    fill.nslots = 2 * ncol + scan_slots + 1 + nbatch * hg_slots

    u2 = u_ref[...].reshape(nbatch * CHUNK, s5w)
    ub = u2.astype(BF16)
    for jb in range(ncol):
        bu = _dot(ub[:, jb * cin:(jb + 1) * cin], bm_ref[jb])
        for part in range(2):
            for gi in range(gpc):
                slab = part * nslab + jb * gpc + gi
                col = part * S5_LANES + gi * LANES
                for b in range(nbatch):
                    bu_slab[b, slab * SLAB_PITCH:slab * SLAB_PITCH + CHUNK, :] = (
                        bu[b * CHUNK:(b + 1) * CHUNK, col:col + LANES])
        fill()

    a_re = [ab_ref[0, hf * SUBLANES:(hf + 1) * SUBLANES, :] for hf in range(nhalf)]
    a_im = [ab_ref[1, hf * SUBLANES:(hf + 1) * SUBLANES, :] for hf in range(nhalf)]

    def step(t, hs):
        out = []
        for b in range(nbatch):
            for hf in range(nhalf):
                h_re, h_im = hs[2 * (b * nhalf + hf)], hs[2 * (b * nhalf + hf) + 1]
                ts_re = pl.ds(hf * SUBLANES * SLAB_PITCH + t, SUBLANES, stride=SLAB_PITCH)
                ts_im = pl.ds((nslab + hf * SUBLANES) * SLAB_PITCH + t, SUBLANES, stride=SLAB_PITCH)
                n_re = a_re[hf] * h_re - a_im[hf] * h_im + bu_slab[b, ts_re, :]
                n_im = a_re[hf] * h_im + a_im[hf] * h_re + bu_slab[b, ts_im, :]
                bu_slab[b, ts_re, :] = n_re
                bu_slab[b, ts_im, :] = n_im
                out += [n_re, n_im]
        return tuple(out)

    init = []
    for b in range(nbatch):
        for hf in range(nhalf):
            init += [hcar_ref[b, 0, hf], hcar_ref[b, 1, hf]]
    hs = tuple(init)
    for t in range(CHUNK):
        hs = step(t, hs)
        if t % SUBLANES == SUBLANES - 1:
            fill()
    for b in range(nbatch):
        for hf in range(nhalf):
            hcar_ref[b, 0, hf] = hs[2 * (b * nhalf + hf)]
            hcar_ref[b, 1, hf] = hs[2 * (b * nhalf + hf) + 1]

    def states(part, jb):
        return jnp.concatenate(
            [jnp.concatenate(
                [bu_slab[b, (part * nslab + jb * gpc + gi) * SLAB_PITCH:
                         (part * nslab + jb * gpc + gi) * SLAB_PITCH + CHUNK, :] for gi in range(gpc)], axis=1)
             for b in range(nbatch)], axis=0).astype(BF16)

    ys = []
    for jb in range(ncol):
        cs = slice(jb * cin, (jb + 1) * cin)
        y = _dot(states(0, jb), cre_ref[jb]) - _dot(states(1, jb), cim_ref[jb]) + d_ref[:, cs] * u2[:, cs]
        ys.append(_gelu_tanh(y))
        fill()
    yg = jnp.concatenate(ys, axis=1)
    yc = yg * _sigmoid(_dot(yg.astype(BF16), gw_ref[...]) + gb_ref[...])
    y_ref[:, :, :s5w] = yc.reshape(nbatch, CHUNK, s5w).astype(y_ref.dtype)
    fill()

    logits = lbl_ref[...]
    pexp = jnp.exp(logits - jnp.max(logits, axis=0, keepdims=True))
    psm = pexp / jnp.sum(pexp, axis=0, keepdims=True)
    lb = jnp.zeros_like(psm[0:1, :])
    for l in range(layer):
        lb = lb + psm[l:l + 1, :]

    ti = lax.broadcasted_iota(jnp.int32, (CHUNK, CHUNK), 0)
    si = lax.broadcasted_iota(jnp.int32, (CHUNK, CHUNK), 1)
    tril = (ti >= si).astype(BF16)
    tp = lax.broadcasted_iota(jnp.int32, (CHUNK, 2 * CHUNK), 0)
    sp = lax.broadcasted_iota(jnp.int32, (CHUNK, 2 * CHUNK), 1) & (CHUNK - 1)
    diag_mask = jnp.logical_and((tp >> 3) == (sp >> 3), sp <= tp)
    nvr = CHUNK // SUBLANES

    def both_heads(x, p):
        xa = x[:, p * pair:p * pair + HG_DK]
        xb_ = x[:, p * pair + HG_DK:(p + 1) * pair]
        zero = jnp.zeros_like(xa)
        return jnp.concatenate([jnp.concatenate([xa, zero], axis=1),
                                jnp.concatenate([zero, xb_], axis=1)], axis=0)

    def hg_body(b, carry):
        f = lb + (1.0 - lb) * _sigmoid(f_ref[b])
        logf = jnp.log(f)
        kk = 1.0 - f
        lf_hi = logf.astype(BF16)
        lf_lo = (logf - lf_hi.astype(F32)).astype(BF16)
        cum = _dot(tril, lf_hi) + _dot(tril, lf_lo)
        total = cum[CHUNK - 1:CHUNK, :]
        q = q_ref[b]
        iv = i_ref[b]
        ivb = iv.astype(BF16)
        q_in = (q * jnp.exp(cum)).astype(BF16)
        k_dec = kk * jnp.exp(total - cum)
        dec = jnp.exp(total)

        att = [jnp.zeros((CHUNK, 2 * CHUNK), F32) for _ in range(HG_HEADS // 2)]
        ends = [cum[v * SUBLANES + SUBLANES - 1:(v + 1) * SUBLANES, :] for v in range(nvr)]
        zeros8 = jnp.zeros((SUBLANES, hw), F32)
        for n in _HG_LEVELS:
            per = n // SUBLANES
            qparts, kparts = [], []
            for v in range(nvr):
                blk = v // per
                vs = slice(v * SUBLANES, (v + 1) * SUBLANES)
                if blk % 2 == 1:
                    qparts.append(q[vs] * jnp.exp(cum[vs] - ends[blk * per - 1]))
                    kparts.append(zeros8)
                else:
                    qparts.append(zeros8)
                    kparts.append(kk[vs] * jnp.exp(ends[blk * per + per - 1] - cum[vs]))
            q_t = jnp.concatenate(qparts, axis=0).astype(BF16)
            k_t = jnp.concatenate(kparts, axis=0).astype(BF16)
            shift = n.bit_length()
            for p in range(HG_HEADS // 2):
                a_n = _dot_nt(q_t[:, p * pair:(p + 1) * pair], both_heads(k_t, p))
                if 2 * n < CHUNK:
                    a_n = jnp.where((tp >> shift) == (sp >> shift), a_n, 0.0)
                att[p] = att[p] + a_n
            fill()
        q3 = q.reshape(nvr, SUBLANES, hw)
        c3 = cum.reshape(nvr, SUBLANES, hw)
        e3 = (jnp.log(jnp.maximum(kk, 0.0)) - cum).reshape(nvr, SUBLANES, hw)
        dsum = [jnp.zeros((CHUNK, 2 * CHUNK), F32) for _ in range(HG_HEADS // 2)]
        for s in range(SUBLANES):
            w = q3 * jnp.exp(jnp.minimum(c3 + e3[:, s:s + 1, :], 0.0))
            wb = w.reshape(CHUNK, hw).astype(BF16)
            for p in range(HG_HEADS // 2):
                dsum[p] = dsum[p] + _dot(wb[:, p * pair:(p + 1) * pair], sel_ref[s])
            if s % 2 == 1:
                fill()
        for p in range(HG_HEADS // 2):
            a_all = (att[p] + jnp.where(diag_mask, dsum[p], 0.0)).astype(BF16)
            o_pair = _dot(a_all, both_heads(ivb, p))
            for hh in range(2):
                h = 2 * p + hh
                hs_ = slice(h * HG_DK, (h + 1) * HG_DK)
                st = st_ref[b, h]
                oh = o_pair[:, hh * HG_DK:(hh + 1) * HG_DK] + _dot_nt(q_in[:, hs_], st.astype(BF16))
                st_ref[b, h] = dec[:, hs_] * st + _dot(iv[:, hs_].T.astype(BF16), k_dec[:, hs_].astype(BF16))
                ms = jnp.mean(oh * oh, axis=-1, keepdims=True)
                gt = g_ref[b, :, hs_]
                y_ref[b, :, s5w + h * HG_DK:s5w + (h + 1) * HG_DK] = (
                    oh * lax.rsqrt(ms + EPS) * gn_ref[:, hs_] * _silu(gt)).astype(y_ref.dtype)
        return carry

    for b in range(nbatch):
        hg_body(b, 0)


def _block_diag(t):
    n, g, r, c = t.shape
    out = jnp.zeros((n, g, r, g, c), t.dtype)
    for gi in range(g):
        out = out.at[:, gi, :, gi, :].set(t[:, gi])
    return out.reshape(n, g * r, g * c)


def _layer_cd(h3, g_in, w_in, w_out, g_out, a_re_log, a_im, b_re, b_im, c_re, c_im, d, log_dt, glu_w, glu_b,
              hg_gn, lb_logits, *, layer):
    batch = h3.shape[0]
    groups, state = a_re_log.shape
    s5w = groups * S5_GROUP
    nstate = groups * state
    gpc = S5_LANES // state
    ncol = groups // gpc
    cin = gpc * S5_GROUP
    hw = HG_HEADS * HG_DK
    nslab = nstate // LANES
    assert s5w == hw and nslab % SUBLANES == 0

    bre = _block_diag(jnp.transpose(b_re.reshape(ncol, gpc, state, S5_GROUP), (0, 1, 3, 2)))
    bim = _block_diag(jnp.transpose(b_im.reshape(ncol, gpc, state, S5_GROUP), (0, 1, 3, 2)))
    cre = _block_diag(jnp.transpose(c_re.reshape(ncol, gpc, S5_GROUP, state), (0, 1, 3, 2))).astype(BF16)
    cim = _block_diag(jnp.transpose(c_im.reshape(ncol, gpc, S5_GROUP, state), (0, 1, 3, 2))).astype(BF16)
    ldt = jnp.repeat(log_dt, state)
    params = [a_re_log.reshape(1, nstate), a_im.reshape(1, nstate), ldt.reshape(1, nstate),
              a_re_log.reshape(nslab, LANES), a_im.reshape(nslab, LANES), ldt.reshape(nslab, LANES),
              bre, bim, cre, cim, d.reshape(1, s5w), glu_w.astype(BF16), glu_b.reshape(1, s5w),
              hg_gn.reshape(1, hw), lb_logits]
    scratch = [
        pltpu.VMEM((ncol, cin, 2 * S5_LANES), BF16),
        pltpu.VMEM((2, nslab, LANES), F32),
        pltpu.VMEM((batch, 2, nslab // SUBLANES, SUBLANES, LANES), F32),
        pltpu.VMEM((batch, 2 * nslab * SLAB_PITCH, LANES), F32),
        pltpu.VMEM((SUBLANES, 2 * HG_DK, 2 * CHUNK), BF16),
        pltpu.VMEM((batch, HG_HEADS, HG_DK, HG_DK), F32),
    ]
    return _layer(h3, g_in, w_in, w_out, g_out, params, mixer=functools.partial(_mixer_cd, layer=layer),
                  init=_mixer_cd_init, cols=(s5w,) * 5, scratch=scratch, name="layer_cd", tn=MXU_COLS)


def _pack_lru_gates(wa, wi):
    nblk, bd, _ = wa.shape
    per = LANES // bd
    wa_bd = _block_diag(wa.reshape(nblk // per, per, bd, bd))
    wi_bd = _block_diag(wi.reshape(nblk // per, per, bd, bd))
    return jnp.concatenate([wa_bd, wi_bd], axis=2).astype(BF16)


def kernel(x, meta, w_in_ab, w_out_ab, ret_gn, rg_wa, rg_ba, rg_wi, rg_bi, rg_lam, rg_conv_w, rg_conv_b,
           w_in_cd, w_out_cd, s5_a_re_log, s5_a_im, s5_b_re, s5_b_im, s5_c_re, s5_c_im, s5_d, s5_log_dt,
           s5_glu_w, s5_glu_b, hg_gn, hg_lb_logits, norm_g, mlp_w1, mlp_w2):
    batch, seq, d = x.shape
    depth = norm_g.shape[0]
    used = PAD + N_META + seq
    assert used % CHUNK == 0
    lp = -(-used // (2 * CHUNK)) * (2 * CHUNK)
    m = batch * lp

    h = jnp.concatenate([jnp.zeros((batch, PAD, d), x.dtype),
                         jnp.broadcast_to(meta[None].astype(x.dtype), (batch, N_META, d)), x,
                         jnp.zeros((batch, lp - used, d), x.dtype)], axis=1)
    for l in range(depth):
        jdx = l // 2
        if l % 2 == 0:
            h = _layer_ab(h, norm_g[l, 0], w_in_ab[jdx].astype(BF16), w_out_ab[jdx].astype(BF16), norm_g[l, 1],
                          ret_gn[jdx], _pack_lru_gates(rg_wa[jdx], rg_wi[jdx]), rg_ba[jdx], rg_bi[jdx],
                          rg_lam[jdx], rg_conv_w[jdx], rg_conv_b[jdx])
        else:
            h = _layer_cd(h, norm_g[l, 0], w_in_cd[jdx].astype(BF16), w_out_cd[jdx].astype(BF16), norm_g[l, 1],
                          s5_a_re_log[jdx], s5_a_im[jdx], s5_b_re[jdx], s5_b_im[jdx], s5_c_re[jdx], s5_c_im[jdx],
                          s5_d[jdx], s5_log_dt[jdx], s5_glu_w[jdx], s5_glu_b[jdx], hg_gn[jdx], hg_lb_logits,
                          layer=l)
        w1, w2 = mlp_w1[l].astype(BF16), mlp_w2[l].astype(BF16)
        if l + 1 < depth:
            h = _mlp(h.reshape(m, d), norm_g[l, 2], w1, w2, norm_g[l, 3],
                     tm=_largest_divisor(lp, 528), tn=1024, rows_per_batch=lp).reshape(batch, lp, d)
        else:
            h = _mlp_frames(h, norm_g[l, 2], w1, w2, norm_g[l, 3], first=PAD + N_META, count=seq,
                            tm=_largest_divisor(seq, 512), tn=1024)
    return h
```

```python
import functools
import math

import jax
import jax.numpy as jnp
from jax import lax
from jax.experimental import pallas as pl
from jax.experimental.pallas import tpu as pltpu

F32 = jnp.float32
BF16 = jnp.bfloat16

CHUNK = 64
N_META = 16
PAD = CHUNK - N_META
EPS = 1e-6

RET_HEADS = 4
RET_DK = 128
RET_DV = 256
ROPE_BASE = 10000.0
LRU_C = 8.0
CONV_WIDTH = 4
S5_GROUP = 16
S5_STATE = 64
S5_LANES = 512
HG_HEADS = 4
HG_DK = 128
LANES = 128
SUBLANES = 8
MXU_COLS = 256
SLAB_PITCH = CHUNK + SUBLANES
VMEM_LIMIT_BYTES = 56 * 1024 * 1024


def _largest_divisor(n, cap):
    return max(d for d in range(1, cap + 1) if n % d == 0)


def _gelu_tanh(x):
    return 0.5 * x * (1.0 + jnp.tanh(0.7978845608028654 * (x + 0.044715 * x * x * x)))


def _sigmoid(x):
    return 0.5 * jnp.tanh(0.5 * x) + 0.5


def _silu(x):
    hx = 0.5 * x
    return hx + hx * jnp.tanh(hx)


def _sqrt_nonneg(x):
    return jnp.where(x > 0.0, x * lax.rsqrt(x), 0.0)


def _dot(a, b):
    return jnp.dot(a, b, preferred_element_type=F32)


def _dot_nt(a, b):
    return lax.dot_general(a, b, (((1,), (1,)), ((), ())), preferred_element_type=F32)


def _keep_rows(out, tiles_per_batch):
    first_tile = lax.rem(pl.program_id(0), tiles_per_batch) == 0
    row = lax.broadcasted_iota(jnp.int32, (out.shape[0], 1), 0)
    keep = jnp.logical_or(row >= PAD, jnp.logical_not(first_tile))
    return jnp.where(keep, out, 0.0)


def _mlp_rows(x, g_in_ref, w1_ref, w2_ref, g_out_ref, tn):
    ms = jnp.mean(x * x, axis=-1, keepdims=True)
    hn = (x * lax.rsqrt(ms + EPS) * g_in_ref[...]).astype(BF16)
    acc = jnp.zeros(x.shape, F32)
    for j in range(w1_ref.shape[1] // tn):
        a = jnp.maximum(_dot(hn, w1_ref[:, j * tn:(j + 1) * tn]), 0.0)
        acc = acc + _dot((a * a).astype(BF16), w2_ref[j * tn:(j + 1) * tn, :])
    ms = jnp.mean(acc * acc, axis=-1, keepdims=True)
    return x + acc * lax.rsqrt(ms + EPS) * g_out_ref[...]


def _mlp_kernel(h_ref, g_in_ref, w1_ref, w2_ref, g_out_ref, o_ref, *, tiles_per_batch, tn):
    o_ref[...] = _keep_rows(_mlp_rows(h_ref[...], g_in_ref, w1_ref, w2_ref, g_out_ref, tn), tiles_per_batch)


def _mlp_frames_kernel(h_ref, g_in_ref, w1_ref, w2_ref, g_out_ref, o_ref, *, tn):
    o_ref[...] = _mlp_rows(h_ref[...], g_in_ref, w1_ref, w2_ref, g_out_ref, tn)


def _mlp_frames(h3, g_in, w1, w2, g_out, *, first, count, tm, tn):
    batch, lp, d = h3.shape
    f = w1.shape[1]
    resident = pl.Buffered(1)
    return pl.pallas_call(
        functools.partial(_mlp_frames_kernel, tn=tn),
        grid=(batch, count // tm),
        in_specs=[
            pl.BlockSpec((pl.Element(tm), pl.Element(d)),
                         lambda b, i: (pl.multiple_of(b * lp + first + i * tm, SUBLANES), 0)),
            pl.BlockSpec((1, d), lambda b, i: (0, 0)),
            pl.BlockSpec((d, f), lambda b, i: (0, 0), pipeline_mode=resident),
            pl.BlockSpec((f, d), lambda b, i: (0, 0), pipeline_mode=resident),
            pl.BlockSpec((1, d), lambda b, i: (0, 0)),
        ],
        out_specs=pl.BlockSpec((None, tm, d), lambda b, i: (b, i, 0)),
        out_shape=jax.ShapeDtypeStruct((batch, count, d), F32),
        compiler_params=pltpu.CompilerParams(
            dimension_semantics=("arbitrary", "arbitrary"), vmem_limit_bytes=VMEM_LIMIT_BYTES),
        name="mlp_frames",
    )(h3.reshape(batch * lp, d), g_in.reshape(1, d), w1, w2, g_out.reshape(1, d))


def _mlp(h, g_in, w1, w2, g_out, *, tm, tn, rows_per_batch):
    m, d = h.shape
    f = w1.shape[1]
    resident = pl.Buffered(1)
    return pl.pallas_call(
        functools.partial(_mlp_kernel, tiles_per_batch=rows_per_batch // tm, tn=tn),
        grid=(m // tm,),
        in_specs=[
            pl.BlockSpec((tm, d), lambda i: (i, 0)),
            pl.BlockSpec((1, d), lambda i: (0, 0)),
            pl.BlockSpec((d, f), lambda i: (0, 0), pipeline_mode=resident),
            pl.BlockSpec((f, d), lambda i: (0, 0), pipeline_mode=resident),
            pl.BlockSpec((1, d), lambda i: (0, 0)),
        ],
        out_specs=pl.BlockSpec((tm, d), lambda i: (i, 0)),
        out_shape=jax.ShapeDtypeStruct((m, d), F32),
        compiler_params=pltpu.CompilerParams(
            dimension_semantics=("arbitrary",), vmem_limit_bytes=VMEM_LIMIT_BYTES),
        name="mlp",
    )(h, g_in.reshape(1, d), w1, w2, g_out.reshape(1, d))


def _window_start(p, seq):
    return jnp.clip(2 * CHUNK * p - CHUNK, 0, seq - 2 * CHUNK)


class _Filler:
    def __init__(self, pieces):
        self._pieces = list(pieces)
        self._total = len(self._pieces)
        self._done = 0
        self._slot = 0
        self.nslots = 1

    def __call__(self):
        self._slot += 1
        target = min(self._total, -(-self._total * self._slot // self.nslots))
        while self._done < target:
            self._pieces[self._done]()
            self._done += 1

    def drain(self):
        while self._done < self._total:
            self._pieces[self._done]()
            self._done += 1


def _layer_kernel(*refs, mixer, init, nparams, cols, tn, nbatch, frames_seq):
    it = iter(refs)
    if frames_seq:
        xa_refs = [next(it) for _ in range(nbatch)]
        xc_refs = [next(it) for _ in range(nbatch)]
        meta_ref = next(it)
    else:
        ha_ref, hc_ref = next(it), next(it)
    g_in_ref, win_ref, wout_ref, g_out_ref = next(it), next(it), next(it), next(it)
    params = [next(it) for _ in range(nparams)]
    o_ref = next(it)
    z_refs = (next(it), next(it))
    y_refs = (next(it), next(it))
    hn_ref = next(it)
    scratch = list(it)
    d = o_ref.shape[2]
    p = pl.program_id(0)
    rows = nbatch * CHUNK

    if frames_seq:
        meta = meta_ref[...]
        meta_chunk = jnp.concatenate([jnp.zeros((CHUNK - meta.shape[0], d), F32), meta], axis=0)
        meta_rows = jnp.concatenate([meta_chunk] * nbatch, axis=0)

        def frame_rows(window_refs, window_start, chunk):
            off = jnp.clip((chunk - 1) * CHUNK - window_start, 0, CHUNK)
            off = pl.multiple_of(off, CHUNK)
            x = jnp.concatenate([r[pl.ds(off, CHUNK), :] for r in window_refs], axis=0)
            return jnp.where(chunk == 0, meta_rows, x)

    def load_in(e, s):
        if frames_seq:
            return frame_rows(xa_refs, _window_start(p, frames_seq), s)
        return ha_ref[:, e * CHUNK:(e + 1) * CHUNK, :].reshape(rows, d)

    def load_res(e, s):
        if frames_seq:
            return frame_rows(xc_refs, _window_start(p - 1, frames_seq), s - 2)
        return hc_ref[:, e * CHUNK:(e + 1) * CHUNK, :].reshape(rows, d)

    @pl.when(p == 0)
    def _():
        z_refs[1][...] = jnp.zeros_like(z_refs[1])
        y_refs[0][...] = jnp.zeros_like(y_refs[0])
        init(*params, *scratch)

    row = lax.broadcasted_iota(jnp.int32, (1, CHUNK, 1), 1)
    for e in range(2):
        s = 2 * p + e
        rs = slice(e * CHUNK, (e + 1) * CHUNK)

        pieces = []
        m_parts = []

        def out_piece(j, e=e, m_parts=m_parts):
            cs = slice(j * tn, (j + 1) * tn)
            y_in = y_refs[e][...].reshape(rows, y_refs[e].shape[2])
            m_parts.append(_dot(y_in, wout_ref[:, cs]))

        def out_finish(e=e, s=s, rs=rs, m_parts=m_parts):
            m = jnp.concatenate(m_parts, axis=1)
            ms = jnp.mean(m * m, axis=-1, keepdims=True)
            xr = load_res(e, s)
            out = (xr + m * lax.rsqrt(ms + EPS) * g_out_ref[...]).reshape(nbatch, CHUNK, d)
            o_ref[:, rs, :] = jnp.where(jnp.logical_or(row >= PAD, s != 2), out, 0.0)

        def in_start(e=e, s=s):
            x = load_in(e, s)
            ms = jnp.mean(x * x, axis=-1, keepdims=True)
            hn_ref[...] = (x * lax.rsqrt(ms + EPS) * g_in_ref[...]).astype(BF16)

        def in_piece(j, e=e):
            cs = slice(j * tn, (j + 1) * tn)
            z_refs[e][:, :, cs] = _dot(hn_ref[...], win_ref[:, cs]).reshape(nbatch, CHUNK, tn)

        pieces += [functools.partial(out_piece, j) for j in range(d // tn)] + [out_finish, in_start]
        pieces += [functools.partial(in_piece, j) for j in range(win_ref.shape[1] // tn)]
        fill = _Filler(pieces)

        views = []
        off = 0
        for width in cols:
            views.append(z_refs[1 - e].at[:, :, off:off + width])
            off += width
        mixer(s - 1, fill, *views, *params, y_refs[1 - e], *scratch)
        fill.drain()


def _layer(h, g_in, w_in, w_out, g_out, params, *, mixer, init, cols, scratch, name, tn, lp, meta=None):
    batch, hlen, d = h.shape
    nin = w_in.shape[1]
    nout = w_out.shape[0]
    assert nin % tn == 0 and d % tn == 0
    nblk = lp // (2 * CHUNK)
    resident = pl.Buffered(1)

    def pspec(shape):
        return pl.BlockSpec(shape, lambda p: (0,) * len(shape))

    if meta is None:
        sources = [h, h]
        source_specs = [pl.BlockSpec((batch, 2 * CHUNK, d), lambda p: (0, jnp.minimum(p, nblk - 1), 0)),
                        pl.BlockSpec((batch, 2 * CHUNK, d), lambda p: (0, jnp.maximum(p - 1, 0), 0))]
    else:
        def window(b, lag):
            return pl.BlockSpec(
                (pl.Element(2 * CHUNK), pl.Element(d)),
                lambda p: (pl.multiple_of(b * hlen + _window_start(p - lag, hlen), CHUNK), 0))

        frames2 = h.reshape(batch * hlen, d)
        sources = [frames2] * (2 * batch) + [meta]
        source_specs = [window(b, lag) for lag in (0, 1) for b in range(batch)] + [pspec(meta.shape)]

    return pl.pallas_call(
        functools.partial(_layer_kernel, mixer=mixer, init=init, nparams=len(params), cols=cols, tn=tn,
                          nbatch=batch, frames_seq=None if meta is None else hlen),
        grid=(nblk + 1,),
        in_specs=source_specs + [
            pspec((1, d)),
            pl.BlockSpec((d, nin), lambda p: (0, 0), pipeline_mode=resident),
            pl.BlockSpec((nout, d), lambda p: (0, 0), pipeline_mode=resident),
            pspec((1, d)),
        ] + [pspec(q.shape) for q in params],
        out_specs=pl.BlockSpec((batch, 2 * CHUNK, d), lambda p: (0, jnp.maximum(p - 1, 0), 0)),
        out_shape=jax.ShapeDtypeStruct((batch, lp, d), F32),
        scratch_shapes=[pltpu.VMEM((batch, CHUNK, nin), F32), pltpu.VMEM((batch, CHUNK, nin), F32),
                        pltpu.VMEM((batch, CHUNK, nout), BF16), pltpu.VMEM((batch, CHUNK, nout), BF16),
                        pltpu.VMEM((batch * CHUNK, d), BF16)] + scratch,
        compiler_params=pltpu.CompilerParams(
            dimension_semantics=("arbitrary",), vmem_limit_bytes=VMEM_LIMIT_BYTES),
        name=name,
    )(*sources, g_in.reshape(1, d), w_in, w_out, g_out.reshape(1, d), *params)


def _mixer_ab_init(gn_ref, wg_ref, ba_ref, bi_ref, lam_ref, cw_ref, cb_ref,
                   s_ref, xcar_ref, hcar_ref, a_slab, b_slab):
    s_ref[...] = jnp.zeros_like(s_ref)
    xcar_ref[...] = jnp.zeros_like(xcar_ref)
    hcar_ref[...] = jnp.zeros_like(hcar_ref)


def _mixer_ab(c, fill, q_ref, k_ref, v_ref, gate_ref, bx_ref, bg_ref, gn_ref, wg_ref, ba_ref, bi_ref,
              lam_ref, cw_ref, cb_ref, y_ref, s_ref, xcar_ref, hcar_ref, a_slab, b_slab):
    nbatch, _, width = bx_ref.shape
    ngrp = width // LANES
    scan_slots = CHUNK // SUBLANES
    fill.nslots = nbatch * RET_HEADS + ngrp + scan_slots

    row = lax.broadcasted_iota(jnp.int32, (CHUNK, 1), 0)
    idx = row.astype(F32)
    pos = (c * CHUNK + row - PAD).astype(F32)
    lane = lax.broadcasted_iota(jnp.int32, (1, RET_DK), 1)
    half = RET_DK // 2
    freq = jnp.exp((lane & (half - 1)).astype(F32) * (-math.log(ROPE_BASE) / half))
    ang = pos * freq
    cosv = jnp.cos(ang)
    sinv = jnp.where(lane < half, -1.0, 1.0) * jnp.sin(ang)
    ti = lax.broadcasted_iota(jnp.int32, (CHUNK, CHUNK), 0)
    si = lax.broadcasted_iota(jnp.int32, (CHUNK, CHUNK), 1)
    dist = jnp.abs(ti - si).astype(F32)

    def ret_body(b, carry):
        for h in range(RET_HEADS):
            log_g = math.log1p(-(2.0 ** (-5.0 - h)))
            qs = slice(h * RET_DK, (h + 1) * RET_DK)
            vs = slice(h * RET_DV, (h + 1) * RET_DV)
            qh = q_ref[b, :, qs]
            kh = k_ref[b, :, qs]
            qr = (qh * cosv + pltpu.roll(qh, half, 1) * sinv) * (RET_DK ** -0.5)
            kr = kh * cosv + pltpu.roll(kh, half, 1) * sinv
            vh = v_ref[b, :, vs].astype(BF16)
            scores = _dot_nt(qr.astype(BF16), kr.astype(BF16)) * jnp.exp(dist * log_g)
            o = _dot(scores.astype(BF16), vh)
            q_dec = qr * jnp.exp((idx + 1.0) * log_g)
            o = o + _dot(q_dec.astype(BF16), s_ref[b, h].astype(BF16))
            k_dec = kr * jnp.exp((CHUNK - 1.0 - idx) * log_g)
            kv = _dot(k_dec.T.astype(BF16), vh)
            s_ref[b, h] = math.exp(CHUNK * log_g) * s_ref[b, h] + kv
            oc = o - jnp.mean(o, axis=-1, keepdims=True)
            var = jnp.mean(oc * oc, axis=-1, keepdims=True)
            gt = gate_ref[b, :, vs]
            y_ref[b, :, vs] = (oc * lax.rsqrt(var + EPS) * gn_ref[:, vs]
                               * _silu(gt)).astype(y_ref.dtype)
            fill()
        return carry

    for b in range(nbatch):
        ret_body(b, 0)

    xb = bx_ref[...]
    xe = jnp.concatenate([xcar_ref[...], xb], axis=1)
    xc = cb_ref[...] + xb * cw_ref[CONV_WIDTH - 1:CONV_WIDTH, :]
    for s in range(1, CONV_WIDTH):
        xc = xc + pltpu.roll(xe, s, 1)[:, SUBLANES:, :] * cw_ref[CONV_WIDTH - 1 - s:CONV_WIDTH - s, :]
    xcar_ref[...] = xb[:, CHUNK - SUBLANES:, :]

    xc2 = xc.reshape(nbatch * CHUNK, width)
    xcb = xc2.astype(BF16)
    valid = jnp.logical_or(c > 0, jnp.logical_and(c == 0, row >= PAD))
    for p in range(ngrp):
        cs = slice(p * LANES, (p + 1) * LANES)
        g2 = _dot(xcb[:, cs], wg_ref[p])
        r = _sigmoid(g2[:, :LANES] + ba_ref[:, cs])
        i = _sigmoid(g2[:, LANES:] + bi_ref[:, cs])
        lam = lam_ref[:, cs]
        softplus_neg_lam = jnp.maximum(-lam, 0.0) + jnp.log1p(jnp.exp(-jnp.abs(lam)))
        a = jnp.exp(-LRU_C * r * softplus_neg_lam)
        bb = _sqrt_nonneg(1.0 - a * a) * (i * xc2[:, cs])
        for b in range(nbatch):
            rs = slice(b * CHUNK, (b + 1) * CHUNK)
            a_slab[b, p * SLAB_PITCH:p * SLAB_PITCH + CHUNK, :] = a[rs]
            b_slab[b, p * SLAB_PITCH:p * SLAB_PITCH + CHUNK, :] = jnp.where(valid, bb[rs], 0.0)
        fill()

    def step(t, hs):
        out = []
        for b in range(nbatch):
            ts = pl.ds(t, ngrp, stride=SLAB_PITCH)
            h = a_slab[b, ts, :] * hs[b] + b_slab[b, ts, :]
            b_slab[b, ts, :] = h
            out.append(h)
        return tuple(out)

    hs = tuple(hcar_ref[b] for b in range(nbatch))
    for t in range(CHUNK):
        hs = step(t, hs)
        if t % SUBLANES == SUBLANES - 1:
            fill()
    for b in range(nbatch):
        hcar_ref[b] = hs[b]
        hfull = jnp.concatenate(
            [b_slab[b, p * SLAB_PITCH:p * SLAB_PITCH + CHUNK, :] for p in range(ngrp)], axis=1)
        y_ref[b, :, RET_HEADS * RET_DV:] = (_gelu_tanh(bg_ref[b]) * hfull).astype(y_ref.dtype)


def _layer_ab(h3, g_in, w_in, w_out, g_out, ret_gn, wg, ba, bi, lam, conv_w, conv_b, *, lp, meta=None):
    batch = h3.shape[0]
    qk = RET_HEADS * RET_DK
    vw = RET_HEADS * RET_DV
    lw = lam.shape[-1]
    assert lw == SUBLANES * LANES
    params = [ret_gn.reshape(1, vw), wg, ba.reshape(1, lw), bi.reshape(1, lw), lam.reshape(1, lw),
              conv_w, conv_b.reshape(1, lw)]
    scratch = [
        pltpu.VMEM((batch, RET_HEADS, RET_DK, RET_DV), F32),
        pltpu.VMEM((batch, SUBLANES, lw), F32),
        pltpu.VMEM((batch, SUBLANES, LANES), F32),
        pltpu.VMEM((batch, SUBLANES * SLAB_PITCH, LANES), F32),
        pltpu.VMEM((batch, SUBLANES * SLAB_PITCH, LANES), F32),
    ]
    return _layer(h3, g_in, w_in, w_out, g_out, params, mixer=_mixer_ab, init=_mixer_ab_init,
                  cols=(qk, qk, vw, vw, lw, lw), scratch=scratch, name="layer_ab", tn=2 * MXU_COLS,
                  lp=lp, meta=meta)


_HG_LEVELS = (8, 16, 32)


def _mixer_cd_init(arl_ref, aim_ref, ldt_ref, arl_t_ref, aim_t_ref, ldt_t_ref,
                   bre_ref, bim_ref, cre_ref, cim_ref, d_ref, gw_ref, gb_ref, gn_ref, lbl_ref,
                   bm_ref, ab_ref, hcar_ref, bu_slab, sel_ref, st_ref):
    def disc(ldt, arl, aim):
        dt = jnp.exp(ldt)
        a_re = -jnp.exp(arl)
        mag = jnp.exp(dt * a_re)
        return a_re, aim, mag * jnp.cos(dt * aim), mag * jnp.sin(dt * aim)

    _, _, t_re, t_im = disc(ldt_t_ref[...], arl_t_ref[...], aim_t_ref[...])
    ab_ref[0] = t_re
    ab_ref[1] = t_im
    a_re, a_im, ab_re, ab_im = disc(ldt_ref[...], arl_ref[...], aim_ref[...])
    den = a_re * a_re + a_im * a_im
    z_re = ((ab_re - 1.0) * a_re + ab_im * a_im) / den
    z_im = (ab_im * a_re - (ab_re - 1.0) * a_im) / den
    for jb in range(bre_ref.shape[0]):
        ls = slice(jb * S5_LANES, (jb + 1) * S5_LANES)
        zr = z_re[:, ls]
        zi = z_im[:, ls]
        bb_re = zr * bre_ref[jb] - zi * bim_ref[jb]
        bb_im = zr * bim_ref[jb] + zi * bre_ref[jb]
        bm_ref[jb] = jnp.concatenate([bb_re, bb_im], axis=1).astype(BF16)
    ri = lax.broadcasted_iota(jnp.int32, (2 * HG_DK, 2 * CHUNK), 0)
    ci = lax.broadcasted_iota(jnp.int32, (2 * HG_DK, 2 * CHUNK), 1)
    same_head = (ri >= HG_DK) == (ci >= CHUNK)
    for s in range(SUBLANES):
        sel_ref[s] = jnp.logical_and(same_head, (ci & (SUBLANES - 1)) == s).astype(BF16)
    hcar_ref[...] = jnp.zeros_like(hcar_ref)
    st_ref[...] = jnp.zeros_like(st_ref)


def _mixer_cd(c, fill, u_ref, q_ref, f_ref, i_ref, g_ref,
              arl_ref, aim_ref, ldt_ref, arl_t_ref, aim_t_ref, ldt_t_ref,
              bre_ref, bim_ref, cre_ref, cim_ref, d_ref, gw_ref, gb_ref, gn_ref, lbl_ref, y_ref,
              bm_ref, ab_ref, hcar_ref, bu_slab, sel_ref, st_ref, *, layer):
    del c
    nbatch, _, s5w = u_ref.shape
    ncol = bre_ref.shape[0]
    cin = bre_ref.shape[1]
    nslab = ab_ref.shape[1]
    nhalf = nslab // SUBLANES
    gpc = S5_LANES // LANES
    hw = HG_HEADS * HG_DK
    pair = 2 * HG_DK
    scan_slots = CHUNK // SUBLANES
    hg_slots = len(_HG_LEVELS) + SUBLANES // 2
    fill.nslots = 2 * ncol + scan_slots + 1 + nbatch * hg_slots

    u2 = u_ref[...].reshape(nbatch * CHUNK, s5w)
    ub = u2.astype(BF16)
    for jb in range(ncol):
        bu = _dot(ub[:, jb * cin:(jb + 1) * cin], bm_ref[jb])
        for part in range(2):
            for gi in range(gpc):
                slab = part * nslab + jb * gpc + gi
                col = part * S5_LANES + gi * LANES
                for b in range(nbatch):
                    bu_slab[b, slab * SLAB_PITCH:slab * SLAB_PITCH + CHUNK, :] = (
                        bu[b * CHUNK:(b + 1) * CHUNK, col:col + LANES])
        fill()

    a_re = [ab_ref[0, hf * SUBLANES:(hf + 1) * SUBLANES, :] for hf in range(nhalf)]
    a_im = [ab_ref[1, hf * SUBLANES:(hf + 1) * SUBLANES, :] for hf in range(nhalf)]

    def step(t, hs):
        out = []
        for b in range(nbatch):
            for hf in range(nhalf):
                h_re, h_im = hs[2 * (b * nhalf + hf)], hs[2 * (b * nhalf + hf) + 1]
                ts_re = pl.ds(hf * SUBLANES * SLAB_PITCH + t, SUBLANES, stride=SLAB_PITCH)
                ts_im = pl.ds((nslab + hf * SUBLANES) * SLAB_PITCH + t, SUBLANES, stride=SLAB_PITCH)
                n_re = a_re[hf] * h_re - a_im[hf] * h_im + bu_slab[b, ts_re, :]
                n_im = a_re[hf] * h_im + a_im[hf] * h_re + bu_slab[b, ts_im, :]
                bu_slab[b, ts_re, :] = n_re
                bu_slab[b, ts_im, :] = n_im
                out += [n_re, n_im]
        return tuple(out)

    init = []
    for b in range(nbatch):
        for hf in range(nhalf):
            init += [hcar_ref[b, 0, hf], hcar_ref[b, 1, hf]]
    hs = tuple(init)
    for t in range(CHUNK):
        hs = step(t, hs)
        if t % SUBLANES == SUBLANES - 1:
            fill()
    for b in range(nbatch):
        for hf in range(nhalf):
            hcar_ref[b, 0, hf] = hs[2 * (b * nhalf + hf)]
            hcar_ref[b, 1, hf] = hs[2 * (b * nhalf + hf) + 1]

    def states(part, jb):
        return jnp.concatenate(
            [jnp.concatenate(
                [bu_slab[b, (part * nslab + jb * gpc + gi) * SLAB_PITCH:
                         (part * nslab + jb * gpc + gi) * SLAB_PITCH + CHUNK, :] for gi in range(gpc)], axis=1)
             for b in range(nbatch)], axis=0).astype(BF16)

    ys = []
    for jb in range(ncol):
        cs = slice(jb * cin, (jb + 1) * cin)
        y = _dot(states(0, jb), cre_ref[jb]) - _dot(states(1, jb), cim_ref[jb]) + d_ref[:, cs] * u2[:, cs]
        ys.append(_gelu_tanh(y))
        fill()
    yg = jnp.concatenate(ys, axis=1)
    yc = yg * _sigmoid(_dot(yg.astype(BF16), gw_ref[...]) + gb_ref[...])
    y_ref[:, :, :s5w] = yc.reshape(nbatch, CHUNK, s5w).astype(y_ref.dtype)
    fill()

    logits = lbl_ref[...]
    pexp = jnp.exp(logits - jnp.max(logits, axis=0, keepdims=True))
    psm = pexp / jnp.sum(pexp, axis=0, keepdims=True)
    lb = jnp.zeros_like(psm[0:1, :])
    for l in range(layer):
        lb = lb + psm[l:l + 1, :]

    ti = lax.broadcasted_iota(jnp.int32, (CHUNK, CHUNK), 0)
    si = lax.broadcasted_iota(jnp.int32, (CHUNK, CHUNK), 1)
    tril = (ti >= si).astype(BF16)
    tp = lax.broadcasted_iota(jnp.int32, (CHUNK, 2 * CHUNK), 0)
    sp = lax.broadcasted_iota(jnp.int32, (CHUNK, 2 * CHUNK), 1) & (CHUNK - 1)
    diag_mask = jnp.logical_and((tp >> 3) == (sp >> 3), sp <= tp)
    nvr = CHUNK // SUBLANES

    def both_heads(x, p):
        xa = x[:, p * pair:p * pair + HG_DK]
        xb_ = x[:, p * pair + HG_DK:(p + 1) * pair]
        zero = jnp.zeros_like(xa)
        return jnp.concatenate([jnp.concatenate([xa, zero], axis=1),
                                jnp.concatenate([zero, xb_], axis=1)], axis=0)

    def hg_body(b, carry):
        f = lb + (1.0 - lb) * _sigmoid(f_ref[b])
        logf = jnp.log(f)
        kk = 1.0 - f
        lf_hi = logf.astype(BF16)
        lf_lo = (logf - lf_hi.astype(F32)).astype(BF16)
        cum = _dot(tril, lf_hi) + _dot(tril, lf_lo)
        total = cum[CHUNK - 1:CHUNK, :]
        q = q_ref[b]
        iv = i_ref[b]
        ivb = iv.astype(BF16)
        q_in = (q * jnp.exp(cum)).astype(BF16)
        k_dec = kk * jnp.exp(total - cum)
        dec = jnp.exp(total)

        att = [jnp.zeros((CHUNK, 2 * CHUNK), F32) for _ in range(HG_HEADS // 2)]
        ends = [cum[v * SUBLANES + SUBLANES - 1:(v + 1) * SUBLANES, :] for v in range(nvr)]
        zeros8 = jnp.zeros((SUBLANES, hw), F32)
        for n in _HG_LEVELS:
            per = n // SUBLANES
            qparts, kparts = [], []
            for v in range(nvr):
                blk = v // per
                vs = slice(v * SUBLANES, (v + 1) * SUBLANES)
                if blk % 2 == 1:
                    qparts.append(q[vs] * jnp.exp(cum[vs] - ends[blk * per - 1]))
                    kparts.append(zeros8)
                else:
                    qparts.append(zeros8)
                    kparts.append(kk[vs] * jnp.exp(ends[blk * per + per - 1] - cum[vs]))
            q_t = jnp.concatenate(qparts, axis=0).astype(BF16)
            k_t = jnp.concatenate(kparts, axis=0).astype(BF16)
            shift = n.bit_length()
            for p in range(HG_HEADS // 2):
                a_n = _dot_nt(q_t[:, p * pair:(p + 1) * pair], both_heads(k_t, p))
                if 2 * n < CHUNK:
                    a_n = jnp.where((tp >> shift) == (sp >> shift), a_n, 0.0)
                att[p] = att[p] + a_n
            fill()
        q3 = q.reshape(nvr, SUBLANES, hw)
        c3 = cum.reshape(nvr, SUBLANES, hw)
        e3 = (jnp.log(jnp.maximum(kk, 0.0)) - cum).reshape(nvr, SUBLANES, hw)
        dsum = [jnp.zeros((CHUNK, 2 * CHUNK), F32) for _ in range(HG_HEADS // 2)]
        for s in range(SUBLANES):
            w = q3 * jnp.exp(jnp.minimum(c3 + e3[:, s:s + 1, :], 0.0))
            wb = w.reshape(CHUNK, hw).astype(BF16)
            for p in range(HG_HEADS // 2):
                dsum[p] = dsum[p] + _dot(wb[:, p * pair:(p + 1) * pair], sel_ref[s])
            if s % 2 == 1:
                fill()
        for p in range(HG_HEADS // 2):
            a_all = (att[p] + jnp.where(diag_mask, dsum[p], 0.0)).astype(BF16)
            o_pair = _dot(a_all, both_heads(ivb, p))
            for hh in range(2):
                h = 2 * p + hh
                hs_ = slice(h * HG_DK, (h + 1) * HG_DK)
                st = st_ref[b, h]
                oh = o_pair[:, hh * HG_DK:(hh + 1) * HG_DK] + _dot_nt(q_in[:, hs_], st.astype(BF16))
                st_ref[b, h] = dec[:, hs_] * st + _dot(iv[:, hs_].T.astype(BF16), k_dec[:, hs_].astype(BF16))
                ms = jnp.mean(oh * oh, axis=-1, keepdims=True)
                gt = g_ref[b, :, hs_]
                y_ref[b, :, s5w + h * HG_DK:s5w + (h + 1) * HG_DK] = (
                    oh * lax.rsqrt(ms + EPS) * gn_ref[:, hs_] * _silu(gt)).astype(y_ref.dtype)
        return carry

    for b in range(nbatch):
        hg_body(b, 0)


def _block_diag(t):
    n, g, r, c = t.shape
    out = jnp.zeros((n, g, r, g, c), t.dtype)
    for gi in range(g):
        out = out.at[:, gi, :, gi, :].set(t[:, gi])
    return out.reshape(n, g * r, g * c)


def _layer_cd(h3, g_in, w_in, w_out, g_out, a_re_log, a_im, b_re, b_im, c_re, c_im, d, log_dt, glu_w, glu_b,
              hg_gn, lb_logits, *, layer):
    batch = h3.shape[0]
    groups, state = a_re_log.shape
    s5w = groups * S5_GROUP
    nstate = groups * state
    gpc = S5_LANES // state
    ncol = groups // gpc
    cin = gpc * S5_GROUP
    hw = HG_HEADS * HG_DK
    nslab = nstate // LANES
    assert s5w == hw and nslab % SUBLANES == 0

    bre = _block_diag(jnp.transpose(b_re.reshape(ncol, gpc, state, S5_GROUP), (0, 1, 3, 2)))
    bim = _block_diag(jnp.transpose(b_im.reshape(ncol, gpc, state, S5_GROUP), (0, 1, 3, 2)))
    cre = _block_diag(jnp.transpose(c_re.reshape(ncol, gpc, S5_GROUP, state), (0, 1, 3, 2))).astype(BF16)
    cim = _block_diag(jnp.transpose(c_im.reshape(ncol, gpc, S5_GROUP, state), (0, 1, 3, 2))).astype(BF16)
    ldt = jnp.repeat(log_dt, state)
    params = [a_re_log.reshape(1, nstate), a_im.reshape(1, nstate), ldt.reshape(1, nstate),
              a_re_log.reshape(nslab, LANES), a_im.reshape(nslab, LANES), ldt.reshape(nslab, LANES),
              bre, bim, cre, cim, d.reshape(1, s5w), glu_w.astype(BF16), glu_b.reshape(1, s5w),
              hg_gn.reshape(1, hw), lb_logits]
    scratch = [
        pltpu.VMEM((ncol, cin, 2 * S5_LANES), BF16),
        pltpu.VMEM((2, nslab, LANES), F32),
        pltpu.VMEM((batch, 2, nslab // SUBLANES, SUBLANES, LANES), F32),
        pltpu.VMEM((batch, 2 * nslab * SLAB_PITCH, LANES), F32),
        pltpu.VMEM((SUBLANES, 2 * HG_DK, 2 * CHUNK), BF16),
        pltpu.VMEM((batch, HG_HEADS, HG_DK, HG_DK), F32),
    ]
    return _layer(h3, g_in, w_in, w_out, g_out, params, mixer=functools.partial(_mixer_cd, layer=layer),
                  init=_mixer_cd_init, cols=(s5w,) * 5, scratch=scratch, name="layer_cd", tn=MXU_COLS,
                  lp=h3.shape[1])


def _pack_lru_gates(wa, wi):
    nblk, bd, _ = wa.shape
    per = LANES // bd
    wa_bd = _block_diag(wa.reshape(nblk // per, per, bd, bd))
    wi_bd = _block_diag(wi.reshape(nblk // per, per, bd, bd))
    return jnp.concatenate([wa_bd, wi_bd], axis=2).astype(BF16)


def kernel(x, meta, w_in_ab, w_out_ab, ret_gn, rg_wa, rg_ba, rg_wi, rg_bi, rg_lam, rg_conv_w, rg_conv_b,
           w_in_cd, w_out_cd, s5_a_re_log, s5_a_im, s5_b_re, s5_b_im, s5_c_re, s5_c_im, s5_d, s5_log_dt,
           s5_glu_w, s5_glu_b, hg_gn, hg_lb_logits, norm_g, mlp_w1, mlp_w2):
    batch, seq, d = x.shape
    depth = norm_g.shape[0]
    used = PAD + N_META + seq
    assert used % CHUNK == 0
    lp = -(-used // (2 * CHUNK)) * (2 * CHUNK)
    m = batch * lp

    h = x
    for l in range(depth):
        jdx = l // 2
        if l % 2 == 0:
            h = _layer_ab(h, norm_g[l, 0], w_in_ab[jdx].astype(BF16), w_out_ab[jdx].astype(BF16), norm_g[l, 1],
                          ret_gn[jdx], _pack_lru_gates(rg_wa[jdx], rg_wi[jdx]), rg_ba[jdx], rg_bi[jdx],
                          rg_lam[jdx], rg_conv_w[jdx], rg_conv_b[jdx], lp=lp,
                          meta=meta.astype(x.dtype) if l == 0 else None)
        else:
            h = _layer_cd(h, norm_g[l, 0], w_in_cd[jdx].astype(BF16), w_out_cd[jdx].astype(BF16), norm_g[l, 1],
                          s5_a_re_log[jdx], s5_a_im[jdx], s5_b_re[jdx], s5_b_im[jdx], s5_c_re[jdx], s5_c_im[jdx],
                          s5_d[jdx], s5_log_dt[jdx], s5_glu_w[jdx], s5_glu_b[jdx], hg_gn[jdx], hg_lb_logits,
                          layer=l)
        w1, w2 = mlp_w1[l].astype(BF16), mlp_w2[l].astype(BF16)
        if l + 1 < depth:
            h = _mlp(h.reshape(m, d), norm_g[l, 2], w1, w2, norm_g[l, 3],
                     tm=_largest_divisor(lp, 528), tn=1024, rows_per_batch=lp).reshape(batch, lp, d)
        else:
            h = _mlp_frames(h, norm_g[l, 2], w1, w2, norm_g[l, 3], first=PAD + N_META, count=seq,
                            tm=_largest_divisor(seq, 512), tn=1024)
    return h
```

```python
import functools
import math

import jax
import jax.numpy as jnp
from jax import lax
from jax.experimental import pallas as pl
from jax.experimental.pallas import tpu as pltpu

F32 = jnp.float32
BF16 = jnp.bfloat16

CHUNK = 64
N_META = 16
PAD = CHUNK - N_META
EPS = 1e-6

RET_HEADS = 4
RET_DK = 128
RET_DV = 256
ROPE_BASE = 10000.0
LRU_C = 8.0
CONV_WIDTH = 4
S5_GROUP = 16
S5_STATE = 64
S5_LANES = 512
HG_HEADS = 4
HG_DK = 128
LANES = 128
SUBLANES = 8
MXU_COLS = 256
SLAB_PITCH = CHUNK + SUBLANES
VMEM_LIMIT_BYTES = 56 * 1024 * 1024


def _largest_divisor(n, cap):
    return max(d for d in range(1, cap + 1) if n % d == 0)


def _gelu_tanh(x):
    return 0.5 * x * (1.0 + jnp.tanh(0.7978845608028654 * (x + 0.044715 * x * x * x)))


def _sigmoid(x):
    return 0.5 * jnp.tanh(0.5 * x) + 0.5


def _silu(x):
    hx = 0.5 * x
    return hx + hx * jnp.tanh(hx)


def _sqrt_nonneg(x):
    return jnp.where(x > 0.0, x * lax.rsqrt(x), 0.0)


def _dot(a, b):
    return jnp.dot(a, b, preferred_element_type=F32)


def _dot_nt(a, b):
    return lax.dot_general(a, b, (((1,), (1,)), ((), ())), preferred_element_type=F32)


def _keep_rows(out, tiles_per_batch):
    first_tile = lax.rem(pl.program_id(0), tiles_per_batch) == 0
    row = lax.broadcasted_iota(jnp.int32, (out.shape[0], 1), 0)
    keep = jnp.logical_or(row >= PAD, jnp.logical_not(first_tile))
    return jnp.where(keep, out, 0.0)


def _mlp_rows(x, g_in_ref, w1_ref, w2_ref, g_out_ref, tn):
    ms = jnp.mean(x * x, axis=-1, keepdims=True)
    hn = (x * lax.rsqrt(ms + EPS) * g_in_ref[...]).astype(BF16)
    acc = jnp.zeros(x.shape, F32)
    for j in range(w1_ref.shape[1] // tn):
        a = jnp.maximum(_dot(hn, w1_ref[:, j * tn:(j + 1) * tn].astype(BF16)), 0.0)
        acc = acc + _dot((a * a).astype(BF16), w2_ref[j * tn:(j + 1) * tn, :].astype(BF16))
    ms = jnp.mean(acc * acc, axis=-1, keepdims=True)
    return x + acc * lax.rsqrt(ms + EPS) * g_out_ref[...]


def _mlp_kernel(h_ref, g_in_ref, w1_ref, w2_ref, g_out_ref, o_ref, *, tiles_per_batch, tn):
    o_ref[...] = _keep_rows(_mlp_rows(h_ref[...], g_in_ref, w1_ref, w2_ref, g_out_ref, tn), tiles_per_batch)


def _mlp_frames_kernel(h_ref, g_in_ref, w1_ref, w2_ref, g_out_ref, o_ref, *, tn):
    o_ref[...] = _mlp_rows(h_ref[...], g_in_ref, w1_ref, w2_ref, g_out_ref, tn)


def _mlp_frames(h3, g_in, w1, w2, g_out, *, first, count, tm, tn):
    batch, lp, d = h3.shape
    f = w1.shape[1]
    resident = pl.Buffered(1)
    return pl.pallas_call(
        functools.partial(_mlp_frames_kernel, tn=tn),
        grid=(batch, count // tm),
        in_specs=[
            pl.BlockSpec((pl.Element(tm), pl.Element(d)),
                         lambda b, i: (pl.multiple_of(b * lp + first + i * tm, SUBLANES), 0)),
            pl.BlockSpec((1, d), lambda b, i: (0, 0)),
            pl.BlockSpec((d, f), lambda b, i: (0, 0), pipeline_mode=resident),
            pl.BlockSpec((f, d), lambda b, i: (0, 0), pipeline_mode=resident),
            pl.BlockSpec((1, d), lambda b, i: (0, 0)),
        ],
        out_specs=pl.BlockSpec((None, tm, d), lambda b, i: (b, i, 0)),
        out_shape=jax.ShapeDtypeStruct((batch, count, d), F32),
        compiler_params=pltpu.CompilerParams(
            dimension_semantics=("arbitrary", "arbitrary"), vmem_limit_bytes=VMEM_LIMIT_BYTES),
        name="mlp_frames",
    )(h3.reshape(batch * lp, d), g_in.reshape(1, d), w1, w2, g_out.reshape(1, d))


def _mlp(h, g_in, w1, w2, g_out, *, tm, tn, rows_per_batch):
    m, d = h.shape
    f = w1.shape[1]
    resident = pl.Buffered(1)
    return pl.pallas_call(
        functools.partial(_mlp_kernel, tiles_per_batch=rows_per_batch // tm, tn=tn),
        grid=(m // tm,),
        in_specs=[
            pl.BlockSpec((tm, d), lambda i: (i, 0)),
            pl.BlockSpec((1, d), lambda i: (0, 0)),
            pl.BlockSpec((d, f), lambda i: (0, 0), pipeline_mode=resident),
            pl.BlockSpec((f, d), lambda i: (0, 0), pipeline_mode=resident),
            pl.BlockSpec((1, d), lambda i: (0, 0)),
        ],
        out_specs=pl.BlockSpec((tm, d), lambda i: (i, 0)),
        out_shape=jax.ShapeDtypeStruct((m, d), F32),
        compiler_params=pltpu.CompilerParams(
            dimension_semantics=("arbitrary",), vmem_limit_bytes=VMEM_LIMIT_BYTES),
        name="mlp",
    )(h, g_in.reshape(1, d), w1, w2, g_out.reshape(1, d))


def _window_start(p, seq):
    return jnp.clip(2 * CHUNK * p - CHUNK, 0, seq - 2 * CHUNK)


class _Filler:
    def __init__(self, pieces):
        self._pieces = list(pieces)
        self._total = len(self._pieces)
        self._done = 0
        self._slot = 0
        self.nslots = 1

    def __call__(self):
        self._slot += 1
        target = min(self._total, -(-self._total * self._slot // self.nslots))
        while self._done < target:
            self._pieces[self._done]()
            self._done += 1

    def drain(self):
        while self._done < self._total:
            self._pieces[self._done]()
            self._done += 1


def _layer_kernel(*refs, mixer, init, nparams, cols, tn, nbatch, frames_seq):
    it = iter(refs)
    if frames_seq:
        xa_refs = [next(it) for _ in range(nbatch)]
        xc_refs = [next(it) for _ in range(nbatch)]
        meta_ref = next(it)
    else:
        ha_ref, hc_ref = next(it), next(it)
    g_in_ref, win_ref, wout_ref, g_out_ref = next(it), next(it), next(it), next(it)
    params = [next(it) for _ in range(nparams)]
    o_ref = next(it)
    z_refs = (next(it), next(it))
    y_refs = (next(it), next(it))
    hn_ref = next(it)
    scratch = list(it)
    d = o_ref.shape[2]
    p = pl.program_id(0)
    rows = nbatch * CHUNK

    if frames_seq:
        meta = meta_ref[...]
        meta_chunk = jnp.concatenate([jnp.zeros((CHUNK - meta.shape[0], d), F32), meta], axis=0)
        meta_rows = jnp.concatenate([meta_chunk] * nbatch, axis=0)

        def frame_rows(window_refs, window_start, chunk):
            off = jnp.clip((chunk - 1) * CHUNK - window_start, 0, CHUNK)
            off = pl.multiple_of(off, CHUNK)
            x = jnp.concatenate([r[pl.ds(off, CHUNK), :] for r in window_refs], axis=0)
            return jnp.where(chunk == 0, meta_rows, x)

    def load_in(e, s):
        if frames_seq:
            return frame_rows(xa_refs, _window_start(p, frames_seq), s)
        return ha_ref[:, e * CHUNK:(e + 1) * CHUNK, :].reshape(rows, d)

    def load_res(e, s):
        if frames_seq:
            return frame_rows(xc_refs, _window_start(p - 1, frames_seq), s - 2)
        return hc_ref[:, e * CHUNK:(e + 1) * CHUNK, :].reshape(rows, d)

    @pl.when(p == 0)
    def _():
        z_refs[1][...] = jnp.zeros_like(z_refs[1])
        y_refs[0][...] = jnp.zeros_like(y_refs[0])
        init(*params, *scratch)

    row = lax.broadcasted_iota(jnp.int32, (1, CHUNK, 1), 1)
    for e in range(2):
        s = 2 * p + e
        rs = slice(e * CHUNK, (e + 1) * CHUNK)

        pieces = []
        m_parts = []

        def out_piece(j, e=e, m_parts=m_parts):
            cs = slice(j * tn, (j + 1) * tn)
            y_in = y_refs[e][...].reshape(rows, y_refs[e].shape[2])
            m_parts.append(_dot(y_in, wout_ref[:, cs]))

        def out_finish(e=e, s=s, rs=rs, m_parts=m_parts):
            m = jnp.concatenate(m_parts, axis=1)
            ms = jnp.mean(m * m, axis=-1, keepdims=True)
            xr = load_res(e, s)
            out = (xr + m * lax.rsqrt(ms + EPS) * g_out_ref[...]).reshape(nbatch, CHUNK, d)
            o_ref[:, rs, :] = jnp.where(jnp.logical_or(row >= PAD, s != 2), out, 0.0)

        def in_start(e=e, s=s):
            x = load_in(e, s)
            ms = jnp.mean(x * x, axis=-1, keepdims=True)
            hn_ref[...] = (x * lax.rsqrt(ms + EPS) * g_in_ref[...]).astype(BF16)

        def in_piece(j, e=e):
            cs = slice(j * tn, (j + 1) * tn)
            z_refs[e][:, :, cs] = _dot(hn_ref[...], win_ref[:, cs]).reshape(nbatch, CHUNK, tn)

        pieces += [functools.partial(out_piece, j) for j in range(d // tn)] + [out_finish, in_start]
        pieces += [functools.partial(in_piece, j) for j in range(win_ref.shape[1] // tn)]
        fill = _Filler(pieces)

        views = []
        off = 0
        for width in cols:
            views.append(z_refs[1 - e].at[:, :, off:off + width])
            off += width
        mixer(s - 1, fill, *views, *params, y_refs[1 - e], *scratch)
        fill.drain()


def _layer(h, g_in, w_in, w_out, g_out, params, *, mixer, init, cols, scratch, name, tn, lp, meta=None):
    batch, hlen, d = h.shape
    nin = w_in.shape[1]
    nout = w_out.shape[0]
    assert nin % tn == 0 and d % tn == 0
    nblk = lp // (2 * CHUNK)
    resident = pl.Buffered(1)

    def pspec(shape):
        return pl.BlockSpec(shape, lambda p: (0,) * len(shape))

    if meta is None:
        sources = [h, h]
        source_specs = [pl.BlockSpec((batch, 2 * CHUNK, d), lambda p: (0, jnp.minimum(p, nblk - 1), 0)),
                        pl.BlockSpec((batch, 2 * CHUNK, d), lambda p: (0, jnp.maximum(p - 1, 0), 0))]
    else:
        def window(b, lag):
            return pl.BlockSpec(
                (pl.Element(2 * CHUNK), pl.Element(d)),
                lambda p: (pl.multiple_of(b * hlen + _window_start(p - lag, hlen), CHUNK), 0))

        frames2 = h.reshape(batch * hlen, d)
        sources = [frames2] * (2 * batch) + [meta]
        source_specs = [window(b, lag) for lag in (0, 1) for b in range(batch)] + [pspec(meta.shape)]

    return pl.pallas_call(
        functools.partial(_layer_kernel, mixer=mixer, init=init, nparams=len(params), cols=cols, tn=tn,
                          nbatch=batch, frames_seq=None if meta is None else hlen),
        grid=(nblk + 1,),
        in_specs=source_specs + [
            pspec((1, d)),
            pl.BlockSpec((d, nin), lambda p: (0, 0), pipeline_mode=resident),
            pl.BlockSpec((nout, d), lambda p: (0, 0), pipeline_mode=resident),
            pspec((1, d)),
        ] + [pspec(q.shape) for q in params],
        out_specs=pl.BlockSpec((batch, 2 * CHUNK, d), lambda p: (0, jnp.maximum(p - 1, 0), 0)),
        out_shape=jax.ShapeDtypeStruct((batch, lp, d), F32),
        scratch_shapes=[pltpu.VMEM((batch, CHUNK, nin), F32), pltpu.VMEM((batch, CHUNK, nin), F32),
                        pltpu.VMEM((batch, CHUNK, nout), BF16), pltpu.VMEM((batch, CHUNK, nout), BF16),
                        pltpu.VMEM((batch * CHUNK, d), BF16)] + scratch,
        compiler_params=pltpu.CompilerParams(
            dimension_semantics=("arbitrary",), vmem_limit_bytes=VMEM_LIMIT_BYTES),
        name=name,
    )(*sources, g_in.reshape(1, d), w_in, w_out, g_out.reshape(1, d), *params)


def _mixer_ab_init(gn_ref, wg_ref, ba_ref, bi_ref, lam_ref, cw_ref, cb_ref,
                   s_ref, xcar_ref, hcar_ref, a_slab, b_slab):
    s_ref[...] = jnp.zeros_like(s_ref)
    xcar_ref[...] = jnp.zeros_like(xcar_ref)
    hcar_ref[...] = jnp.zeros_like(hcar_ref)


def _mixer_ab(c, fill, q_ref, k_ref, v_ref, gate_ref, bx_ref, bg_ref, gn_ref, wg_ref, ba_ref, bi_ref,
              lam_ref, cw_ref, cb_ref, y_ref, s_ref, xcar_ref, hcar_ref, a_slab, b_slab):
    nbatch, _, width = bx_ref.shape
    ngrp = width // LANES
    scan_slots = CHUNK // SUBLANES
    fill.nslots = nbatch * RET_HEADS + ngrp + scan_slots

    row = lax.broadcasted_iota(jnp.int32, (CHUNK, 1), 0)
    idx = row.astype(F32)
    pos = (c * CHUNK + row - PAD).astype(F32)
    lane = lax.broadcasted_iota(jnp.int32, (1, RET_DK), 1)
    half = RET_DK // 2
    freq = jnp.exp((lane & (half - 1)).astype(F32) * (-math.log(ROPE_BASE) / half))
    ang = pos * freq
    cosv = jnp.cos(ang)
    sinv = jnp.where(lane < half, -1.0, 1.0) * jnp.sin(ang)
    ti = lax.broadcasted_iota(jnp.int32, (CHUNK, CHUNK), 0)
    si = lax.broadcasted_iota(jnp.int32, (CHUNK, CHUNK), 1)
    dist = jnp.abs(ti - si).astype(F32)

    def ret_body(b, carry):
        for h in range(RET_HEADS):
            log_g = math.log1p(-(2.0 ** (-5.0 - h)))
            qs = slice(h * RET_DK, (h + 1) * RET_DK)
            vs = slice(h * RET_DV, (h + 1) * RET_DV)
            qh = q_ref[b, :, qs]
            kh = k_ref[b, :, qs]
            qr = (qh * cosv + pltpu.roll(qh, half, 1) * sinv) * (RET_DK ** -0.5)
            kr = kh * cosv + pltpu.roll(kh, half, 1) * sinv
            vh = v_ref[b, :, vs].astype(BF16)
            scores = _dot_nt(qr.astype(BF16), kr.astype(BF16)) * jnp.exp(dist * log_g)
            o = _dot(scores.astype(BF16), vh)
            q_dec = qr * jnp.exp((idx + 1.0) * log_g)
            o = o + _dot(q_dec.astype(BF16), s_ref[b, h].astype(BF16))
            k_dec = kr * jnp.exp((CHUNK - 1.0 - idx) * log_g)
            kv = _dot(k_dec.T.astype(BF16), vh)
            s_ref[b, h] = math.exp(CHUNK * log_g) * s_ref[b, h] + kv
            oc = o - jnp.mean(o, axis=-1, keepdims=True)
            var = jnp.mean(oc * oc, axis=-1, keepdims=True)
            gt = gate_ref[b, :, vs]
            y_ref[b, :, vs] = (oc * lax.rsqrt(var + EPS) * gn_ref[:, vs]
                               * _silu(gt)).astype(y_ref.dtype)
            fill()
        return carry

    for b in range(nbatch):
        ret_body(b, 0)

    xb = bx_ref[...]
    xe = jnp.concatenate([xcar_ref[...], xb], axis=1)
    xc = cb_ref[...] + xb * cw_ref[CONV_WIDTH - 1:CONV_WIDTH, :]
    for s in range(1, CONV_WIDTH):
        xc = xc + pltpu.roll(xe, s, 1)[:, SUBLANES:, :] * cw_ref[CONV_WIDTH - 1 - s:CONV_WIDTH - s, :]
    xcar_ref[...] = xb[:, CHUNK - SUBLANES:, :]

    xc2 = xc.reshape(nbatch * CHUNK, width)
    xcb = xc2.astype(BF16)
    valid = jnp.logical_or(c > 0, jnp.logical_and(c == 0, row >= PAD))
    for p in range(ngrp):
        cs = slice(p * LANES, (p + 1) * LANES)
        g2 = _dot(xcb[:, cs], wg_ref[p])
        r = _sigmoid(g2[:, :LANES] + ba_ref[:, cs])
        i = _sigmoid(g2[:, LANES:] + bi_ref[:, cs])
        lam = lam_ref[:, cs]
        softplus_neg_lam = jnp.maximum(-lam, 0.0) + jnp.log1p(jnp.exp(-jnp.abs(lam)))
        a = jnp.exp(-LRU_C * r * softplus_neg_lam)
        bb = _sqrt_nonneg(1.0 - a * a) * (i * xc2[:, cs])
        for b in range(nbatch):
            rs = slice(b * CHUNK, (b + 1) * CHUNK)
            a_slab[b, p * SLAB_PITCH:p * SLAB_PITCH + CHUNK, :] = a[rs]
            b_slab[b, p * SLAB_PITCH:p * SLAB_PITCH + CHUNK, :] = jnp.where(valid, bb[rs], 0.0)
        fill()

    def step(t, hs):
        out = []
        for b in range(nbatch):
            ts = pl.ds(t, ngrp, stride=SLAB_PITCH)
            h = a_slab[b, ts, :] * hs[b] + b_slab[b, ts, :]
            b_slab[b, ts, :] = h
            out.append(h)
        return tuple(out)

    hs = tuple(hcar_ref[b] for b in range(nbatch))
    for t in range(CHUNK):
        hs = step(t, hs)
        if t % SUBLANES == SUBLANES - 1:
            fill()
    for b in range(nbatch):
        hcar_ref[b] = hs[b]
        hfull = jnp.concatenate(
            [b_slab[b, p * SLAB_PITCH:p * SLAB_PITCH + CHUNK, :] for p in range(ngrp)], axis=1)
        y_ref[b, :, RET_HEADS * RET_DV:] = (_gelu_tanh(bg_ref[b]) * hfull).astype(y_ref.dtype)


def _layer_ab(h3, g_in, w_in, w_out, g_out, ret_gn, wg, ba, bi, lam, conv_w, conv_b, *, lp, meta=None):
    batch = h3.shape[0]
    qk = RET_HEADS * RET_DK
    vw = RET_HEADS * RET_DV
    lw = lam.shape[-1]
    assert lw == SUBLANES * LANES
    params = [ret_gn.reshape(1, vw), wg, ba.reshape(1, lw), bi.reshape(1, lw), lam.reshape(1, lw),
              conv_w, conv_b.reshape(1, lw)]
    scratch = [
        pltpu.VMEM((batch, RET_HEADS, RET_DK, RET_DV), F32),
        pltpu.VMEM((batch, SUBLANES, lw), F32),
        pltpu.VMEM((batch, SUBLANES, LANES), F32),
        pltpu.VMEM((batch, SUBLANES * SLAB_PITCH, LANES), F32),
        pltpu.VMEM((batch, SUBLANES * SLAB_PITCH, LANES), F32),
    ]
    return _layer(h3, g_in, w_in, w_out, g_out, params, mixer=_mixer_ab, init=_mixer_ab_init,
                  cols=(qk, qk, vw, vw, lw, lw), scratch=scratch, name="layer_ab", tn=2 * MXU_COLS,
                  lp=lp, meta=meta)


_HG_LEVELS = (8, 16, 32)


def _mixer_cd_init(arl_ref, aim_ref, ldt_ref, arl_t_ref, aim_t_ref, ldt_t_ref,
                   bre_ref, bim_ref, cre_ref, cim_ref, d_ref, gw_ref, gb_ref, gn_ref, lbl_ref,
                   bm_ref, ab_ref, hcar_ref, bu_slab, sel_ref, st_ref):
    def disc(ldt, arl, aim):
        dt = jnp.exp(ldt)
        a_re = -jnp.exp(arl)
        mag = jnp.exp(dt * a_re)
        return a_re, aim, mag * jnp.cos(dt * aim), mag * jnp.sin(dt * aim)

    _, _, t_re, t_im = disc(ldt_t_ref[...], arl_t_ref[...], aim_t_ref[...])
    ab_ref[0] = t_re
    ab_ref[1] = t_im
    a_re, a_im, ab_re, ab_im = disc(ldt_ref[...], arl_ref[...], aim_ref[...])
    den = a_re * a_re + a_im * a_im
    z_re = ((ab_re - 1.0) * a_re + ab_im * a_im) / den
    z_im = (ab_im * a_re - (ab_re - 1.0) * a_im) / den
    for jb in range(bre_ref.shape[0]):
        ls = slice(jb * S5_LANES, (jb + 1) * S5_LANES)
        zr = z_re[:, ls]
        zi = z_im[:, ls]
        bb_re = zr * bre_ref[jb] - zi * bim_ref[jb]
        bb_im = zr * bim_ref[jb] + zi * bre_ref[jb]
        bm_ref[jb] = jnp.concatenate([bb_re, bb_im], axis=1).astype(BF16)
    ri = lax.broadcasted_iota(jnp.int32, (2 * HG_DK, 2 * CHUNK), 0)
    ci = lax.broadcasted_iota(jnp.int32, (2 * HG_DK, 2 * CHUNK), 1)
    same_head = (ri >= HG_DK) == (ci >= CHUNK)
    for s in range(SUBLANES):
        sel_ref[s] = jnp.logical_and(same_head, (ci & (SUBLANES - 1)) == s).astype(BF16)
    hcar_ref[...] = jnp.zeros_like(hcar_ref)
    st_ref[...] = jnp.zeros_like(st_ref)


def _mixer_cd(c, fill, u_ref, q_ref, f_ref, i_ref, g_ref,
              arl_ref, aim_ref, ldt_ref, arl_t_ref, aim_t_ref, ldt_t_ref,
              bre_ref, bim_ref, cre_ref, cim_ref, d_ref, gw_ref, gb_ref, gn_ref, lbl_ref, y_ref,
              bm_ref, ab_ref, hcar_ref, bu_slab, sel_ref, st_ref, *, layer):
    del c
    nbatch, _, s5w = u_ref.shape
    ncol = bre_ref.shape[0]
    cin = bre_ref.shape[1]
    nslab = ab_ref.shape[1]
    nhalf = nslab // SUBLANES
    gpc = S5_LANES // LANES
    hw = HG_HEADS * HG_DK
    pair = 2 * HG_DK
    scan_slots = CHUNK // SUBLANES
    hg_slots = len(_HG_LEVELS) + SUBLANES // 2
    fill.nslots = 2 * ncol + scan_slots + 1 + nbatch * hg_slots

    u2 = u_ref[...].reshape(nbatch * CHUNK, s5w)
    ub = u2.astype(BF16)
    for jb in range(ncol):
        bu = _dot(ub[:, jb * cin:(jb + 1) * cin], bm_ref[jb])
        for part in range(2):
            for gi in range(gpc):
                slab = part * nslab + jb * gpc + gi
                col = part * S5_LANES + gi * LANES
                for b in range(nbatch):
                    bu_slab[b, slab * SLAB_PITCH:slab * SLAB_PITCH + CHUNK, :] = (
                        bu[b * CHUNK:(b + 1) * CHUNK, col:col + LANES])
        fill()

    a_re = [ab_ref[0, hf * SUBLANES:(hf + 1) * SUBLANES, :] for hf in range(nhalf)]
    a_im = [ab_ref[1, hf * SUBLANES:(hf + 1) * SUBLANES, :] for hf in range(nhalf)]

    def step(t, hs):
        out = []
        for b in range(nbatch):
            for hf in range(nhalf):
                h_re, h_im = hs[2 * (b * nhalf + hf)], hs[2 * (b * nhalf + hf) + 1]
                ts_re = pl.ds(hf * SUBLANES * SLAB_PITCH + t, SUBLANES, stride=SLAB_PITCH)
                ts_im = pl.ds((nslab + hf * SUBLANES) * SLAB_PITCH + t, SUBLANES, stride=SLAB_PITCH)
                n_re = a_re[hf] * h_re - a_im[hf] * h_im + bu_slab[b, ts_re, :]
                n_im = a_re[hf] * h_im + a_im[hf] * h_re + bu_slab[b, ts_im, :]
                bu_slab[b, ts_re, :] = n_re
                bu_slab[b, ts_im, :] = n_im
                out += [n_re, n_im]
        return tuple(out)

    init = []
    for b in range(nbatch):
        for hf in range(nhalf):
            init += [hcar_ref[b, 0, hf], hcar_ref[b, 1, hf]]
    hs = tuple(init)
    for t in range(CHUNK):
        hs = step(t, hs)
        if t % SUBLANES == SUBLANES - 1:
            fill()
    for b in range(nbatch):
        for hf in range(nhalf):
            hcar_ref[b, 0, hf] = hs[2 * (b * nhalf + hf)]
            hcar_ref[b, 1, hf] = hs[2 * (b * nhalf + hf) + 1]

    def states(part, jb):
        return jnp.concatenate(
            [jnp.concatenate(
                [bu_slab[b, (part * nslab + jb * gpc + gi) * SLAB_PITCH:
                         (part * nslab + jb * gpc + gi) * SLAB_PITCH + CHUNK, :] for gi in range(gpc)], axis=1)
             for b in range(nbatch)], axis=0).astype(BF16)

    ys = []
    for jb in range(ncol):
        cs = slice(jb * cin, (jb + 1) * cin)
        y = _dot(states(0, jb), cre_ref[jb]) - _dot(states(1, jb), cim_ref[jb]) + d_ref[:, cs] * u2[:, cs]
        ys.append(_gelu_tanh(y))
        fill()
    yg = jnp.concatenate(ys, axis=1)
    yc = yg * _sigmoid(_dot(yg.astype(BF16), gw_ref[...]) + gb_ref[...])
    y_ref[:, :, :s5w] = yc.reshape(nbatch, CHUNK, s5w).astype(y_ref.dtype)
    fill()

    logits = lbl_ref[...]
    pexp = jnp.exp(logits - jnp.max(logits, axis=0, keepdims=True))
    psm = pexp / jnp.sum(pexp, axis=0, keepdims=True)
    lb = jnp.zeros_like(psm[0:1, :])
    for l in range(layer):
        lb = lb + psm[l:l + 1, :]

    ti = lax.broadcasted_iota(jnp.int32, (CHUNK, CHUNK), 0)
    si = lax.broadcasted_iota(jnp.int32, (CHUNK, CHUNK), 1)
    tril = (ti >= si).astype(BF16)
    tp = lax.broadcasted_iota(jnp.int32, (CHUNK, 2 * CHUNK), 0)
    sp = lax.broadcasted_iota(jnp.int32, (CHUNK, 2 * CHUNK), 1) & (CHUNK - 1)
    diag_mask = jnp.logical_and((tp >> 3) == (sp >> 3), sp <= tp)
    nvr = CHUNK // SUBLANES

    def both_heads(x, p):
        xa = x[:, p * pair:p * pair + HG_DK]
        xb_ = x[:, p * pair + HG_DK:(p + 1) * pair]
        zero = jnp.zeros_like(xa)
        return jnp.concatenate([jnp.concatenate([xa, zero], axis=1),
                                jnp.concatenate([zero, xb_], axis=1)], axis=0)

    def hg_body(b, carry):
        f = lb + (1.0 - lb) * _sigmoid(f_ref[b])
        logf = jnp.log(f)
        kk = 1.0 - f
        lf_hi = logf.astype(BF16)
        lf_lo = (logf - lf_hi.astype(F32)).astype(BF16)
        cum = _dot(tril, lf_hi) + _dot(tril, lf_lo)
        total = cum[CHUNK - 1:CHUNK, :]
        q = q_ref[b]
        iv = i_ref[b]
        ivb = iv.astype(BF16)
        q_in = (q * jnp.exp(cum)).astype(BF16)
        k_dec = kk * jnp.exp(total - cum)
        dec = jnp.exp(total)

        att = [jnp.zeros((CHUNK, 2 * CHUNK), F32) for _ in range(HG_HEADS // 2)]
        ends = [cum[v * SUBLANES + SUBLANES - 1:(v + 1) * SUBLANES, :] for v in range(nvr)]
        zeros8 = jnp.zeros((SUBLANES, hw), F32)
        for n in _HG_LEVELS:
            per = n // SUBLANES
            qparts, kparts = [], []
            for v in range(nvr):
                blk = v // per
                vs = slice(v * SUBLANES, (v + 1) * SUBLANES)
                if blk % 2 == 1:
                    qparts.append(q[vs] * jnp.exp(cum[vs] - ends[blk * per - 1]))
                    kparts.append(zeros8)
                else:
                    qparts.append(zeros8)
                    kparts.append(kk[vs] * jnp.exp(ends[blk * per + per - 1] - cum[vs]))
            q_t = jnp.concatenate(qparts, axis=0).astype(BF16)
            k_t = jnp.concatenate(kparts, axis=0).astype(BF16)
            shift = n.bit_length()
            for p in range(HG_HEADS // 2):
                a_n = _dot_nt(q_t[:, p * pair:(p + 1) * pair], both_heads(k_t, p))
                if 2 * n < CHUNK:
                    a_n = jnp.where((tp >> shift) == (sp >> shift), a_n, 0.0)
                att[p] = att[p] + a_n
            fill()
        q3 = q.reshape(nvr, SUBLANES, hw)
        c3 = cum.reshape(nvr, SUBLANES, hw)
        e3 = (jnp.log(jnp.maximum(kk, 0.0)) - cum).reshape(nvr, SUBLANES, hw)
        dsum = [jnp.zeros((CHUNK, 2 * CHUNK), F32) for _ in range(HG_HEADS // 2)]
        for s in range(SUBLANES):
            w = q3 * jnp.exp(jnp.minimum(c3 + e3[:, s:s + 1, :], 0.0))
            wb = w.reshape(CHUNK, hw).astype(BF16)
            for p in range(HG_HEADS // 2):
                dsum[p] = dsum[p] + _dot(wb[:, p * pair:(p + 1) * pair], sel_ref[s])
            if s % 2 == 1:
                fill()
        for p in range(HG_HEADS // 2):
            a_all = (att[p] + jnp.where(diag_mask, dsum[p], 0.0)).astype(BF16)
            o_pair = _dot(a_all, both_heads(ivb, p))
            for hh in range(2):
                h = 2 * p + hh
                hs_ = slice(h * HG_DK, (h + 1) * HG_DK)
                st = st_ref[b, h]
                oh = o_pair[:, hh * HG_DK:(hh + 1) * HG_DK] + _dot_nt(q_in[:, hs_], st.astype(BF16))
                st_ref[b, h] = dec[:, hs_] * st + _dot(iv[:, hs_].T.astype(BF16), k_dec[:, hs_].astype(BF16))
                ms = jnp.mean(oh * oh, axis=-1, keepdims=True)
                gt = g_ref[b, :, hs_]
                y_ref[b, :, s5w + h * HG_DK:s5w + (h + 1) * HG_DK] = (
                    oh * lax.rsqrt(ms + EPS) * gn_ref[:, hs_] * _silu(gt)).astype(y_ref.dtype)
        return carry

    for b in range(nbatch):
        hg_body(b, 0)


def _block_diag(t):
    n, g, r, c = t.shape
    out = jnp.zeros((n, g, r, g, c), t.dtype)
    for gi in range(g):
        out = out.at[:, gi, :, gi, :].set(t[:, gi])
    return out.reshape(n, g * r, g * c)


def _layer_cd(h3, g_in, w_in, w_out, g_out, a_re_log, a_im, b_re, b_im, c_re, c_im, d, log_dt, glu_w, glu_b,
              hg_gn, lb_logits, *, layer):
    batch = h3.shape[0]
    groups, state = a_re_log.shape
    s5w = groups * S5_GROUP
    nstate = groups * state
    gpc = S5_LANES // state
    ncol = groups // gpc
    cin = gpc * S5_GROUP
    hw = HG_HEADS * HG_DK
    nslab = nstate // LANES
    assert s5w == hw and nslab % SUBLANES == 0

    bre = _block_diag(jnp.transpose(b_re.reshape(ncol, gpc, state, S5_GROUP), (0, 1, 3, 2)))
    bim = _block_diag(jnp.transpose(b_im.reshape(ncol, gpc, state, S5_GROUP), (0, 1, 3, 2)))
    cre = _block_diag(jnp.transpose(c_re.reshape(ncol, gpc, S5_GROUP, state), (0, 1, 3, 2))).astype(BF16)
    cim = _block_diag(jnp.transpose(c_im.reshape(ncol, gpc, S5_GROUP, state), (0, 1, 3, 2))).astype(BF16)
    ldt = jnp.repeat(log_dt, state)
    params = [a_re_log.reshape(1, nstate), a_im.reshape(1, nstate), ldt.reshape(1, nstate),
              a_re_log.reshape(nslab, LANES), a_im.reshape(nslab, LANES), ldt.reshape(nslab, LANES),
              bre, bim, cre, cim, d.reshape(1, s5w), glu_w.astype(BF16), glu_b.reshape(1, s5w),
              hg_gn.reshape(1, hw), lb_logits]
    scratch = [
        pltpu.VMEM((ncol, cin, 2 * S5_LANES), BF16),
        pltpu.VMEM((2, nslab, LANES), F32),
        pltpu.VMEM((batch, 2, nslab // SUBLANES, SUBLANES, LANES), F32),
        pltpu.VMEM((batch, 2 * nslab * SLAB_PITCH, LANES), F32),
        pltpu.VMEM((SUBLANES, 2 * HG_DK, 2 * CHUNK), BF16),
        pltpu.VMEM((batch, HG_HEADS, HG_DK, HG_DK), F32),
    ]
    return _layer(h3, g_in, w_in, w_out, g_out, params, mixer=functools.partial(_mixer_cd, layer=layer),
                  init=_mixer_cd_init, cols=(s5w,) * 5, scratch=scratch, name="layer_cd", tn=MXU_COLS,
                  lp=h3.shape[1])


def _pack_lru_gates(wa, wi):
    nblk, bd, _ = wa.shape
    per = LANES // bd
    wa_bd = _block_diag(wa.reshape(nblk // per, per, bd, bd))
    wi_bd = _block_diag(wi.reshape(nblk // per, per, bd, bd))
    return jnp.concatenate([wa_bd, wi_bd], axis=2).astype(BF16)


def kernel(x, meta, w_in_ab, w_out_ab, ret_gn, rg_wa, rg_ba, rg_wi, rg_bi, rg_lam, rg_conv_w, rg_conv_b,
           w_in_cd, w_out_cd, s5_a_re_log, s5_a_im, s5_b_re, s5_b_im, s5_c_re, s5_c_im, s5_d, s5_log_dt,
           s5_glu_w, s5_glu_b, hg_gn, hg_lb_logits, norm_g, mlp_w1, mlp_w2):
    batch, seq, d = x.shape
    depth = norm_g.shape[0]
    used = PAD + N_META + seq
    assert used % CHUNK == 0
    lp = -(-used // (2 * CHUNK)) * (2 * CHUNK)
    m = batch * lp

    h = x
    for l in range(depth):
        jdx = l // 2
        if l % 2 == 0:
            h = _layer_ab(h, norm_g[l, 0], w_in_ab[jdx].astype(BF16), w_out_ab[jdx].astype(BF16), norm_g[l, 1],
                          ret_gn[jdx], _pack_lru_gates(rg_wa[jdx], rg_wi[jdx]), rg_ba[jdx], rg_bi[jdx],
                          rg_lam[jdx], rg_conv_w[jdx], rg_conv_b[jdx], lp=lp,
                          meta=meta.astype(x.dtype) if l == 0 else None)
        else:
            h = _layer_cd(h, norm_g[l, 0], w_in_cd[jdx].astype(BF16), w_out_cd[jdx].astype(BF16), norm_g[l, 1],
                          s5_a_re_log[jdx], s5_a_im[jdx], s5_b_re[jdx], s5_b_im[jdx], s5_c_re[jdx], s5_c_im[jdx],
                          s5_d[jdx], s5_log_dt[jdx], s5_glu_w[jdx], s5_glu_b[jdx], hg_gn[jdx], hg_lb_logits,
                          layer=l)
        w1, w2 = mlp_w1[l], mlp_w2[l]
        if l + 1 < depth:
            h = _mlp(h.reshape(m, d), norm_g[l, 2], w1, w2, norm_g[l, 3],
                     tm=_largest_divisor(lp, 528), tn=1024, rows_per_batch=lp).reshape(batch, lp, d)
        else:
            h = _mlp_frames(h, norm_g[l, 2], w1, w2, norm_g[l, 3], first=PAD + N_META, count=seq,
                            tm=_largest_divisor(seq, 512), tn=1024)
    return h
```

```python
import functools
import math

import jax
import jax.numpy as jnp
from jax import lax
from jax.experimental import pallas as pl
from jax.experimental.pallas import tpu as pltpu

F32 = jnp.float32
BF16 = jnp.bfloat16

CHUNK = 64
N_META = 16
PAD = CHUNK - N_META
EPS = 1e-6

RET_HEADS = 4
RET_DK = 128
RET_DV = 256
ROPE_BASE = 10000.0
LRU_C = 8.0
CONV_WIDTH = 4
S5_GROUP = 16
S5_STATE = 64
S5_LANES = 512
HG_HEADS = 4
HG_DK = 128
LANES = 128
SUBLANES = 8
MXU_COLS = 256
SLAB_PITCH = CHUNK + SUBLANES
VMEM_LIMIT_BYTES = 56 * 1024 * 1024


def _largest_divisor(n, cap):
    return max(d for d in range(1, cap + 1) if n % d == 0)


def _gelu_tanh(x):
    return 0.5 * x * (1.0 + jnp.tanh(0.7978845608028654 * (x + 0.044715 * x * x * x)))


def _sigmoid(x):
    return 0.5 * jnp.tanh(0.5 * x) + 0.5


def _silu(x):
    hx = 0.5 * x
    return hx + hx * jnp.tanh(hx)


def _sqrt_nonneg(x):
    return jnp.where(x > 0.0, x * lax.rsqrt(x), 0.0)


def _dot(a, b):
    return jnp.dot(a, b, preferred_element_type=F32)


def _dot_nt(a, b):
    return lax.dot_general(a, b, (((1,), (1,)), ((), ())), preferred_element_type=F32)


def _keep_rows(out, tiles_per_batch):
    first_tile = lax.rem(pl.program_id(0), tiles_per_batch) == 0
    row = lax.broadcasted_iota(jnp.int32, (out.shape[0], 1), 0)
    keep = jnp.logical_or(row >= PAD, jnp.logical_not(first_tile))
    return jnp.where(keep, out, 0.0)


def _mlp_rows(x, g_in_ref, w1_ref, w2_ref, g_out_ref, tn):
    ms = jnp.mean(x * x, axis=-1, keepdims=True)
    hn = (x * lax.rsqrt(ms + EPS) * g_in_ref[...]).astype(BF16)
    acc = jnp.zeros(x.shape, F32)
    for j in range(w1_ref.shape[1] // tn):
        a = jnp.maximum(_dot(hn, w1_ref[:, j * tn:(j + 1) * tn].astype(BF16)), 0.0)
        acc = acc + _dot((a * a).astype(BF16), w2_ref[j * tn:(j + 1) * tn, :].astype(BF16))
    ms = jnp.mean(acc * acc, axis=-1, keepdims=True)
    return x + acc * lax.rsqrt(ms + EPS) * g_out_ref[...]


def _mlp_kernel(h_ref, g_in_ref, w1_ref, w2_ref, g_out_ref, o_ref, *, tiles_per_batch, tn):
    o_ref[...] = _keep_rows(_mlp_rows(h_ref[...], g_in_ref, w1_ref, w2_ref, g_out_ref, tn), tiles_per_batch)


def _mlp_frames_kernel(h_ref, g_in_ref, w1_ref, w2_ref, g_out_ref, o_ref, *, tn):
    o_ref[...] = _mlp_rows(h_ref[...], g_in_ref, w1_ref, w2_ref, g_out_ref, tn)


def _mlp_frames(h3, g_in, w1, w2, g_out, *, layer, first, count, tm, tn):
    batch, lp, d = h3.shape
    f = w1.shape[2]
    resident = pl.Buffered(1)
    return pl.pallas_call(
        functools.partial(_mlp_frames_kernel, tn=tn),
        grid=(batch, count // tm),
        in_specs=[
            pl.BlockSpec((pl.Element(tm), pl.Element(d)),
                         lambda b, i: (pl.multiple_of(b * lp + first + i * tm, SUBLANES), 0)),
            pl.BlockSpec((1, d), lambda b, i: (0, 0)),
            pl.BlockSpec((None, d, f), lambda b, i: (layer, 0, 0), pipeline_mode=resident),
            pl.BlockSpec((None, f, d), lambda b, i: (layer, 0, 0), pipeline_mode=resident),
            pl.BlockSpec((1, d), lambda b, i: (0, 0)),
        ],
        out_specs=pl.BlockSpec((None, tm, d), lambda b, i: (b, i, 0)),
        out_shape=jax.ShapeDtypeStruct((batch, count, d), F32),
        compiler_params=pltpu.CompilerParams(
            dimension_semantics=("arbitrary", "arbitrary"), vmem_limit_bytes=VMEM_LIMIT_BYTES),
        name="mlp_frames",
    )(h3.reshape(batch * lp, d), g_in.reshape(1, d), w1, w2, g_out.reshape(1, d))


def _mlp(h, g_in, w1, w2, g_out, *, layer, tm, tn, rows_per_batch):
    m, d = h.shape
    f = w1.shape[2]
    resident = pl.Buffered(1)
    return pl.pallas_call(
        functools.partial(_mlp_kernel, tiles_per_batch=rows_per_batch // tm, tn=tn),
        grid=(m // tm,),
        in_specs=[
            pl.BlockSpec((tm, d), lambda i: (i, 0)),
            pl.BlockSpec((1, d), lambda i: (0, 0)),
            pl.BlockSpec((None, d, f), lambda i: (layer, 0, 0), pipeline_mode=resident),
            pl.BlockSpec((None, f, d), lambda i: (layer, 0, 0), pipeline_mode=resident),
            pl.BlockSpec((1, d), lambda i: (0, 0)),
        ],
        out_specs=pl.BlockSpec((tm, d), lambda i: (i, 0)),
        out_shape=jax.ShapeDtypeStruct((m, d), F32),
        compiler_params=pltpu.CompilerParams(
            dimension_semantics=("arbitrary",), vmem_limit_bytes=VMEM_LIMIT_BYTES),
        name="mlp",
    )(h, g_in.reshape(1, d), w1, w2, g_out.reshape(1, d))


def _window_start(p, seq):
    return jnp.clip(2 * CHUNK * p - CHUNK, 0, seq - 2 * CHUNK)


class _Filler:
    def __init__(self, pieces):
        self._pieces = list(pieces)
        self._total = len(self._pieces)
        self._done = 0
        self._slot = 0
        self.nslots = 1

    def __call__(self):
        self._slot += 1
        target = min(self._total, -(-self._total * self._slot // self.nslots))
        while self._done < target:
            self._pieces[self._done]()
            self._done += 1

    def drain(self):
        while self._done < self._total:
            self._pieces[self._done]()
            self._done += 1


def _layer_kernel(*refs, mixer, init, nparams, cols, tn, nbatch, frames_seq):
    it = iter(refs)
    if frames_seq:
        xa_refs = [next(it) for _ in range(nbatch)]
        xc_refs = [next(it) for _ in range(nbatch)]
        meta_ref = next(it)
    else:
        ha_ref, hc_ref = next(it), next(it)
    g_in_ref, win_ref, wout_ref, g_out_ref = next(it), next(it), next(it), next(it)
    params = [next(it) for _ in range(nparams)]
    o_ref = next(it)
    z_refs = (next(it), next(it))
    y_refs = (next(it), next(it))
    hn_ref = next(it)
    scratch = list(it)
    d = o_ref.shape[2]
    p = pl.program_id(0)
    rows = nbatch * CHUNK

    if frames_seq:
        meta = meta_ref[...]
        meta_chunk = jnp.concatenate([jnp.zeros((CHUNK - meta.shape[0], d), F32), meta], axis=0)
        meta_rows = jnp.concatenate([meta_chunk] * nbatch, axis=0)

        def frame_rows(window_refs, window_start, chunk):
            off = jnp.clip((chunk - 1) * CHUNK - window_start, 0, CHUNK)
            off = pl.multiple_of(off, CHUNK)
            x = jnp.concatenate([r[pl.ds(off, CHUNK), :] for r in window_refs], axis=0)
            return jnp.where(chunk == 0, meta_rows, x)

    def load_in(e, s):
        if frames_seq:
            return frame_rows(xa_refs, _window_start(p, frames_seq), s)
        return ha_ref[:, e * CHUNK:(e + 1) * CHUNK, :].reshape(rows, d)

    def load_res(e, s):
        if frames_seq:
            return frame_rows(xc_refs, _window_start(p - 1, frames_seq), s - 2)
        return hc_ref[:, e * CHUNK:(e + 1) * CHUNK, :].reshape(rows, d)

    @pl.when(p == 0)
    def _():
        z_refs[1][...] = jnp.zeros_like(z_refs[1])
        y_refs[0][...] = jnp.zeros_like(y_refs[0])
        init(*params, *scratch)

    row = lax.broadcasted_iota(jnp.int32, (1, CHUNK, 1), 1)
    for e in range(2):
        s = 2 * p + e
        rs = slice(e * CHUNK, (e + 1) * CHUNK)

        pieces = []
        m_parts = []

        def out_piece(j, e=e, m_parts=m_parts):
            cs = slice(j * tn, (j + 1) * tn)
            y_in = y_refs[e][...].reshape(rows, y_refs[e].shape[2])
            m_parts.append(_dot(y_in, wout_ref[:, cs]))

        def out_finish(e=e, s=s, rs=rs, m_parts=m_parts):
            m = jnp.concatenate(m_parts, axis=1)
            ms = jnp.mean(m * m, axis=-1, keepdims=True)
            xr = load_res(e, s)
            out = (xr + m * lax.rsqrt(ms + EPS) * g_out_ref[...]).reshape(nbatch, CHUNK, d)
            o_ref[:, rs, :] = jnp.where(jnp.logical_or(row >= PAD, s != 2), out, 0.0)

        def in_start(e=e, s=s):
            x = load_in(e, s)
            ms = jnp.mean(x * x, axis=-1, keepdims=True)
            hn_ref[...] = (x * lax.rsqrt(ms + EPS) * g_in_ref[...]).astype(BF16)

        def in_piece(j, e=e):
            cs = slice(j * tn, (j + 1) * tn)
            z_refs[e][:, :, cs] = _dot(hn_ref[...], win_ref[:, cs]).reshape(nbatch, CHUNK, tn)

        pieces += [functools.partial(out_piece, j) for j in range(d // tn)] + [out_finish, in_start]
        pieces += [functools.partial(in_piece, j) for j in range(win_ref.shape[1] // tn)]
        fill = _Filler(pieces)

        views = []
        off = 0
        for width in cols:
            views.append(z_refs[1 - e].at[:, :, off:off + width])
            off += width
        mixer(s - 1, fill, *views, *params, y_refs[1 - e], *scratch)
        fill.drain()


def _layer(h, g_in, w_in, w_out, g_out, params, *, mixer, init, cols, scratch, name, tn, lp, meta=None):
    batch, hlen, d = h.shape
    nin = w_in.shape[1]
    nout = w_out.shape[0]
    assert nin % tn == 0 and d % tn == 0
    nblk = lp // (2 * CHUNK)
    resident = pl.Buffered(1)

    def pspec(shape):
        return pl.BlockSpec(shape, lambda p: (0,) * len(shape))

    if meta is None:
        sources = [h, h]
        source_specs = [pl.BlockSpec((batch, 2 * CHUNK, d), lambda p: (0, jnp.minimum(p, nblk - 1), 0)),
                        pl.BlockSpec((batch, 2 * CHUNK, d), lambda p: (0, jnp.maximum(p - 1, 0), 0))]
    else:
        def window(b, lag):
            return pl.BlockSpec(
                (pl.Element(2 * CHUNK), pl.Element(d)),
                lambda p: (pl.multiple_of(b * hlen + _window_start(p - lag, hlen), CHUNK), 0))

        frames2 = h.reshape(batch * hlen, d)
        sources = [frames2] * (2 * batch) + [meta]
        source_specs = [window(b, lag) for lag in (0, 1) for b in range(batch)] + [pspec(meta.shape)]

    return pl.pallas_call(
        functools.partial(_layer_kernel, mixer=mixer, init=init, nparams=len(params), cols=cols, tn=tn,
                          nbatch=batch, frames_seq=None if meta is None else hlen),
        grid=(nblk + 1,),
        in_specs=source_specs + [
            pspec((1, d)),
            pl.BlockSpec((d, nin), lambda p: (0, 0), pipeline_mode=resident),
            pl.BlockSpec((nout, d), lambda p: (0, 0), pipeline_mode=resident),
            pspec((1, d)),
        ] + [pspec(q.shape) for q in params],
        out_specs=pl.BlockSpec((batch, 2 * CHUNK, d), lambda p: (0, jnp.maximum(p - 1, 0), 0)),
        out_shape=jax.ShapeDtypeStruct((batch, lp, d), F32),
        scratch_shapes=[pltpu.VMEM((batch, CHUNK, nin), F32), pltpu.VMEM((batch, CHUNK, nin), F32),
                        pltpu.VMEM((batch, CHUNK, nout), BF16), pltpu.VMEM((batch, CHUNK, nout), BF16),
                        pltpu.VMEM((batch * CHUNK, d), BF16)] + scratch,
        compiler_params=pltpu.CompilerParams(
            dimension_semantics=("arbitrary",), vmem_limit_bytes=VMEM_LIMIT_BYTES),
        name=name,
    )(*sources, g_in.reshape(1, d), w_in, w_out, g_out.reshape(1, d), *params)


def _mixer_ab_init(gn_ref, wg_ref, ba_ref, bi_ref, lam_ref, cw_ref, cb_ref,
                   s_ref, xcar_ref, hcar_ref, a_slab, b_slab):
    s_ref[...] = jnp.zeros_like(s_ref)
    xcar_ref[...] = jnp.zeros_like(xcar_ref)
    hcar_ref[...] = jnp.zeros_like(hcar_ref)


def _mixer_ab(c, fill, q_ref, k_ref, v_ref, gate_ref, bx_ref, bg_ref, gn_ref, wg_ref, ba_ref, bi_ref,
              lam_ref, cw_ref, cb_ref, y_ref, s_ref, xcar_ref, hcar_ref, a_slab, b_slab):
    nbatch, _, width = bx_ref.shape
    ngrp = width // LANES
    scan_slots = CHUNK // SUBLANES
    fill.nslots = nbatch * RET_HEADS + ngrp + scan_slots

    row = lax.broadcasted_iota(jnp.int32, (CHUNK, 1), 0)
    idx = row.astype(F32)
    pos = (c * CHUNK + row - PAD).astype(F32)
    lane = lax.broadcasted_iota(jnp.int32, (1, RET_DK), 1)
    half = RET_DK // 2
    freq = jnp.exp((lane & (half - 1)).astype(F32) * (-math.log(ROPE_BASE) / half))
    ang = pos * freq
    cosv = jnp.cos(ang)
    sinv = jnp.where(lane < half, -1.0, 1.0) * jnp.sin(ang)
    ti = lax.broadcasted_iota(jnp.int32, (CHUNK, CHUNK), 0)
    si = lax.broadcasted_iota(jnp.int32, (CHUNK, CHUNK), 1)
    dist = jnp.abs(ti - si).astype(F32)

    def ret_body(b, carry):
        for h in range(RET_HEADS):
            log_g = math.log1p(-(2.0 ** (-5.0 - h)))
            qs = slice(h * RET_DK, (h + 1) * RET_DK)
            vs = slice(h * RET_DV, (h + 1) * RET_DV)
            qh = q_ref[b, :, qs]
            kh = k_ref[b, :, qs]
            qr = (qh * cosv + pltpu.roll(qh, half, 1) * sinv) * (RET_DK ** -0.5)
            kr = kh * cosv + pltpu.roll(kh, half, 1) * sinv
            vh = v_ref[b, :, vs].astype(BF16)
            scores = _dot_nt(qr.astype(BF16), kr.astype(BF16)) * jnp.exp(dist * log_g)
            o = _dot(scores.astype(BF16), vh)
            q_dec = qr * jnp.exp((idx + 1.0) * log_g)
            o = o + _dot(q_dec.astype(BF16), s_ref[b, h].astype(BF16))
            k_dec = kr * jnp.exp((CHUNK - 1.0 - idx) * log_g)
            kv = _dot(k_dec.T.astype(BF16), vh)
            s_ref[b, h] = math.exp(CHUNK * log_g) * s_ref[b, h] + kv
            oc = o - jnp.mean(o, axis=-1, keepdims=True)
            var = jnp.mean(oc * oc, axis=-1, keepdims=True)
            gt = gate_ref[b, :, vs]
            y_ref[b, :, vs] = (oc * lax.rsqrt(var + EPS) * gn_ref[:, vs]
                               * _silu(gt)).astype(y_ref.dtype)
            fill()
        return carry

    for b in range(nbatch):
        ret_body(b, 0)

    xb = bx_ref[...]
    xe = jnp.concatenate([xcar_ref[...], xb], axis=1)
    xc = cb_ref[...] + xb * cw_ref[CONV_WIDTH - 1:CONV_WIDTH, :]
    for s in range(1, CONV_WIDTH):
        xc = xc + pltpu.roll(xe, s, 1)[:, SUBLANES:, :] * cw_ref[CONV_WIDTH - 1 - s:CONV_WIDTH - s, :]
    xcar_ref[...] = xb[:, CHUNK - SUBLANES:, :]

    xc2 = xc.reshape(nbatch * CHUNK, width)
    xcb = xc2.astype(BF16)
    valid = jnp.logical_or(c > 0, jnp.logical_and(c == 0, row >= PAD))
    for p in range(ngrp):
        cs = slice(p * LANES, (p + 1) * LANES)
        g2 = _dot(xcb[:, cs], wg_ref[p])
        r = _sigmoid(g2[:, :LANES] + ba_ref[:, cs])
        i = _sigmoid(g2[:, LANES:] + bi_ref[:, cs])
        lam = lam_ref[:, cs]
        softplus_neg_lam = jnp.maximum(-lam, 0.0) + jnp.log1p(jnp.exp(-jnp.abs(lam)))
        a = jnp.exp(-LRU_C * r * softplus_neg_lam)
        bb = _sqrt_nonneg(1.0 - a * a) * (i * xc2[:, cs])
        for b in range(nbatch):
            rs = slice(b * CHUNK, (b + 1) * CHUNK)
            a_slab[b, p * SLAB_PITCH:p * SLAB_PITCH + CHUNK, :] = a[rs]
            b_slab[b, p * SLAB_PITCH:p * SLAB_PITCH + CHUNK, :] = jnp.where(valid, bb[rs], 0.0)
        fill()

    def step(t, hs):
        out = []
        for b in range(nbatch):
            ts = pl.ds(t, ngrp, stride=SLAB_PITCH)
            h = a_slab[b, ts, :] * hs[b] + b_slab[b, ts, :]
            b_slab[b, ts, :] = h
            out.append(h)
        return tuple(out)

    hs = tuple(hcar_ref[b] for b in range(nbatch))
    for t in range(CHUNK):
        hs = step(t, hs)
        if t % SUBLANES == SUBLANES - 1:
            fill()
    for b in range(nbatch):
        hcar_ref[b] = hs[b]
        hfull = jnp.concatenate(
            [b_slab[b, p * SLAB_PITCH:p * SLAB_PITCH + CHUNK, :] for p in range(ngrp)], axis=1)
        y_ref[b, :, RET_HEADS * RET_DV:] = (_gelu_tanh(bg_ref[b]) * hfull).astype(y_ref.dtype)


def _layer_ab(h3, g_in, w_in, w_out, g_out, ret_gn, wg, ba, bi, lam, conv_w, conv_b, *, lp, meta=None):
    batch = h3.shape[0]
    qk = RET_HEADS * RET_DK
    vw = RET_HEADS * RET_DV
    lw = lam.shape[-1]
    assert lw == SUBLANES * LANES
    params = [ret_gn.reshape(1, vw), wg, ba.reshape(1, lw), bi.reshape(1, lw), lam.reshape(1, lw),
              conv_w, conv_b.reshape(1, lw)]
    scratch = [
        pltpu.VMEM((batch, RET_HEADS, RET_DK, RET_DV), F32),
        pltpu.VMEM((batch, SUBLANES, lw), F32),
        pltpu.VMEM((batch, SUBLANES, LANES), F32),
        pltpu.VMEM((batch, SUBLANES * SLAB_PITCH, LANES), F32),
        pltpu.VMEM((batch, SUBLANES * SLAB_PITCH, LANES), F32),
    ]
    return _layer(h3, g_in, w_in, w_out, g_out, params, mixer=_mixer_ab, init=_mixer_ab_init,
                  cols=(qk, qk, vw, vw, lw, lw), scratch=scratch, name="layer_ab", tn=2 * MXU_COLS,
                  lp=lp, meta=meta)


_HG_LEVELS = (8, 16, 32)


def _mixer_cd_init(arl_ref, aim_ref, ldt_ref, arl_t_ref, aim_t_ref, ldt_t_ref,
                   bre_ref, bim_ref, cre_ref, cim_ref, d_ref, gw_ref, gb_ref, gn_ref, lbl_ref,
                   bm_ref, ab_ref, hcar_ref, bu_slab, sel_ref, st_ref):
    def disc(ldt, arl, aim):
        dt = jnp.exp(ldt)
        a_re = -jnp.exp(arl)
        mag = jnp.exp(dt * a_re)
        return a_re, aim, mag * jnp.cos(dt * aim), mag * jnp.sin(dt * aim)

    _, _, t_re, t_im = disc(ldt_t_ref[...], arl_t_ref[...], aim_t_ref[...])
    ab_ref[0] = t_re
    ab_ref[1] = t_im
    a_re, a_im, ab_re, ab_im = disc(ldt_ref[...], arl_ref[...], aim_ref[...])
    den = a_re * a_re + a_im * a_im
    z_re = ((ab_re - 1.0) * a_re + ab_im * a_im) / den
    z_im = (ab_im * a_re - (ab_re - 1.0) * a_im) / den
    for jb in range(bre_ref.shape[0]):
        ls = slice(jb * S5_LANES, (jb + 1) * S5_LANES)
        zr = z_re[:, ls]
        zi = z_im[:, ls]
        bb_re = zr * bre_ref[jb] - zi * bim_ref[jb]
        bb_im = zr * bim_ref[jb] + zi * bre_ref[jb]
        bm_ref[jb] = jnp.concatenate([bb_re, bb_im], axis=1).astype(BF16)
    ri = lax.broadcasted_iota(jnp.int32, (2 * HG_DK, 2 * CHUNK), 0)
    ci = lax.broadcasted_iota(jnp.int32, (2 * HG_DK, 2 * CHUNK), 1)
    same_head = (ri >= HG_DK) == (ci >= CHUNK)
    for s in range(SUBLANES):
        sel_ref[s] = jnp.logical_and(same_head, (ci & (SUBLANES - 1)) == s).astype(BF16)
    hcar_ref[...] = jnp.zeros_like(hcar_ref)
    st_ref[...] = jnp.zeros_like(st_ref)


def _mixer_cd(c, fill, u_ref, q_ref, f_ref, i_ref, g_ref,
              arl_ref, aim_ref, ldt_ref, arl_t_ref, aim_t_ref, ldt_t_ref,
              bre_ref, bim_ref, cre_ref, cim_ref, d_ref, gw_ref, gb_ref, gn_ref, lbl_ref, y_ref,
              bm_ref, ab_ref, hcar_ref, bu_slab, sel_ref, st_ref, *, layer):
    del c
    nbatch, _, s5w = u_ref.shape
    ncol = bre_ref.shape[0]
    cin = bre_ref.shape[1]
    nslab = ab_ref.shape[1]
    nhalf = nslab // SUBLANES
    gpc = S5_LANES // LANES
    hw = HG_HEADS * HG_DK
    pair = 2 * HG_DK
    scan_slots = CHUNK // SUBLANES
    hg_slots = len(_HG_LEVELS) + SUBLANES // 2
    fill.nslots = 2 * ncol + scan_slots + 1 + nbatch * hg_slots

    u2 = u_ref[...].reshape(nbatch * CHUNK, s5w)
    ub = u2.astype(BF16)
    for jb in range(ncol):
        bu = _dot(ub[:, jb * cin:(jb + 1) * cin], bm_ref[jb])
        for part in range(2):
            for gi in range(gpc):
                slab = part * nslab + jb * gpc + gi
                col = part * S5_LANES + gi * LANES
                for b in range(nbatch):
                    bu_slab[b, slab * SLAB_PITCH:slab * SLAB_PITCH + CHUNK, :] = (
                        bu[b * CHUNK:(b + 1) * CHUNK, col:col + LANES])
        fill()

    a_re = [ab_ref[0, hf * SUBLANES:(hf + 1) * SUBLANES, :] for hf in range(nhalf)]
    a_im = [ab_ref[1, hf * SUBLANES:(hf + 1) * SUBLANES, :] for hf in range(nhalf)]

    def step(t, hs):
        out = []
        for b in range(nbatch):
            for hf in range(nhalf):
                h_re, h_im = hs[2 * (b * nhalf + hf)], hs[2 * (b * nhalf + hf) + 1]
                ts_re = pl.ds(hf * SUBLANES * SLAB_PITCH + t, SUBLANES, stride=SLAB_PITCH)
                ts_im = pl.ds((nslab + hf * SUBLANES) * SLAB_PITCH + t, SUBLANES, stride=SLAB_PITCH)
                n_re = a_re[hf] * h_re - a_im[hf] * h_im + bu_slab[b, ts_re, :]
                n_im = a_re[hf] * h_im + a_im[hf] * h_re + bu_slab[b, ts_im, :]
                bu_slab[b, ts_re, :] = n_re
                bu_slab[b, ts_im, :] = n_im
                out += [n_re, n_im]
        return tuple(out)

    init = []
    for b in range(nbatch):
        for hf in range(nhalf):
            init += [hcar_ref[b, 0, hf], hcar_ref[b, 1, hf]]
    hs = tuple(init)
    for t in range(CHUNK):
        hs = step(t, hs)
        if t % SUBLANES == SUBLANES - 1:
            fill()
    for b in range(nbatch):
        for hf in range(nhalf):
            hcar_ref[b, 0, hf] = hs[2 * (b * nhalf + hf)]
            hcar_ref[b, 1, hf] = hs[2 * (b * nhalf + hf) + 1]

    def states(part, jb):
        return jnp.concatenate(
            [jnp.concatenate(
                [bu_slab[b, (part * nslab + jb * gpc + gi) * SLAB_PITCH:
                         (part * nslab + jb * gpc + gi) * SLAB_PITCH + CHUNK, :] for gi in range(gpc)], axis=1)
             for b in range(nbatch)], axis=0).astype(BF16)

    ys = []
    for jb in range(ncol):
        cs = slice(jb * cin, (jb + 1) * cin)
        y = _dot(states(0, jb), cre_ref[jb]) - _dot(states(1, jb), cim_ref[jb]) + d_ref[:, cs] * u2[:, cs]
        ys.append(_gelu_tanh(y))
        fill()
    yg = jnp.concatenate(ys, axis=1)
    yc = yg * _sigmoid(_dot(yg.astype(BF16), gw_ref[...]) + gb_ref[...])
    y_ref[:, :, :s5w] = yc.reshape(nbatch, CHUNK, s5w).astype(y_ref.dtype)
    fill()

    logits = lbl_ref[...]
    pexp = jnp.exp(logits - jnp.max(logits, axis=0, keepdims=True))
    psm = pexp / jnp.sum(pexp, axis=0, keepdims=True)
    lb = jnp.zeros_like(psm[0:1, :])
    for l in range(layer):
        lb = lb + psm[l:l + 1, :]

    ti = lax.broadcasted_iota(jnp.int32, (CHUNK, CHUNK), 0)
    si = lax.broadcasted_iota(jnp.int32, (CHUNK, CHUNK), 1)
    tril = (ti >= si).astype(BF16)
    tp = lax.broadcasted_iota(jnp.int32, (CHUNK, 2 * CHUNK), 0)
    sp = lax.broadcasted_iota(jnp.int32, (CHUNK, 2 * CHUNK), 1) & (CHUNK - 1)
    diag_mask = jnp.logical_and((tp >> 3) == (sp >> 3), sp <= tp)
    nvr = CHUNK // SUBLANES

    def both_heads(x, p):
        xa = x[:, p * pair:p * pair + HG_DK]
        xb_ = x[:, p * pair + HG_DK:(p + 1) * pair]
        zero = jnp.zeros_like(xa)
        return jnp.concatenate([jnp.concatenate([xa, zero], axis=1),
                                jnp.concatenate([zero, xb_], axis=1)], axis=0)

    def hg_body(b, carry):
        f = lb + (1.0 - lb) * _sigmoid(f_ref[b])
        logf = jnp.log(f)
        kk = 1.0 - f
        lf_hi = logf.astype(BF16)
        lf_lo = (logf - lf_hi.astype(F32)).astype(BF16)
        cum = _dot(tril, lf_hi) + _dot(tril, lf_lo)
        total = cum[CHUNK - 1:CHUNK, :]
        q = q_ref[b]
        iv = i_ref[b]
        ivb = iv.astype(BF16)
        q_in = (q * jnp.exp(cum)).astype(BF16)
        k_dec = kk * jnp.exp(total - cum)
        dec = jnp.exp(total)

        att = [jnp.zeros((CHUNK, 2 * CHUNK), F32) for _ in range(HG_HEADS // 2)]
        ends = [cum[v * SUBLANES + SUBLANES - 1:(v + 1) * SUBLANES, :] for v in range(nvr)]
        zeros8 = jnp.zeros((SUBLANES, hw), F32)
        for n in _HG_LEVELS:
            per = n // SUBLANES
            qparts, kparts = [], []
            for v in range(nvr):
                blk = v // per
                vs = slice(v * SUBLANES, (v + 1) * SUBLANES)
                if blk % 2 == 1:
                    qparts.append(q[vs] * jnp.exp(cum[vs] - ends[blk * per - 1]))
                    kparts.append(zeros8)
                else:
                    qparts.append(zeros8)
                    kparts.append(kk[vs] * jnp.exp(ends[blk * per + per - 1] - cum[vs]))
            q_t = jnp.concatenate(qparts, axis=0).astype(BF16)
            k_t = jnp.concatenate(kparts, axis=0).astype(BF16)
            shift = n.bit_length()
            for p in range(HG_HEADS // 2):
                a_n = _dot_nt(q_t[:, p * pair:(p + 1) * pair], both_heads(k_t, p))
                if 2 * n < CHUNK:
                    a_n = jnp.where((tp >> shift) == (sp >> shift), a_n, 0.0)
                att[p] = att[p] + a_n
            fill()
        q3 = q.reshape(nvr, SUBLANES, hw)
        c3 = cum.reshape(nvr, SUBLANES, hw)
        e3 = (jnp.log(jnp.maximum(kk, 0.0)) - cum).reshape(nvr, SUBLANES, hw)
        dsum = [jnp.zeros((CHUNK, 2 * CHUNK), F32) for _ in range(HG_HEADS // 2)]
        for s in range(SUBLANES):
            w = q3 * jnp.exp(jnp.minimum(c3 + e3[:, s:s + 1, :], 0.0))
            wb = w.reshape(CHUNK, hw).astype(BF16)
            for p in range(HG_HEADS // 2):
                dsum[p] = dsum[p] + _dot(wb[:, p * pair:(p + 1) * pair], sel_ref[s])
            if s % 2 == 1:
                fill()
        for p in range(HG_HEADS // 2):
            a_all = (att[p] + jnp.where(diag_mask, dsum[p], 0.0)).astype(BF16)
            o_pair = _dot(a_all, both_heads(ivb, p))
            for hh in range(2):
                h = 2 * p + hh
                hs_ = slice(h * HG_DK, (h + 1) * HG_DK)
                st = st_ref[b, h]
                oh = o_pair[:, hh * HG_DK:(hh + 1) * HG_DK] + _dot_nt(q_in[:, hs_], st.astype(BF16))
                st_ref[b, h] = dec[:, hs_] * st + _dot(iv[:, hs_].T.astype(BF16), k_dec[:, hs_].astype(BF16))
                ms = jnp.mean(oh * oh, axis=-1, keepdims=True)
                gt = g_ref[b, :, hs_]
                y_ref[b, :, s5w + h * HG_DK:s5w + (h + 1) * HG_DK] = (
                    oh * lax.rsqrt(ms + EPS) * gn_ref[:, hs_] * _silu(gt)).astype(y_ref.dtype)
        return carry

    for b in range(nbatch):
        hg_body(b, 0)


def _block_diag(t):
    n, g, r, c = t.shape
    out = jnp.zeros((n, g, r, g, c), t.dtype)
    for gi in range(g):
        out = out.at[:, gi, :, gi, :].set(t[:, gi])
    return out.reshape(n, g * r, g * c)


def _layer_cd(h3, g_in, w_in, w_out, g_out, a_re_log, a_im, b_re, b_im, c_re, c_im, d, log_dt, glu_w, glu_b,
              hg_gn, lb_logits, *, layer):
    batch = h3.shape[0]
    groups, state = a_re_log.shape
    s5w = groups * S5_GROUP
    nstate = groups * state
    gpc = S5_LANES // state
    ncol = groups // gpc
    cin = gpc * S5_GROUP
    hw = HG_HEADS * HG_DK
    nslab = nstate // LANES
    assert s5w == hw and nslab % SUBLANES == 0

    bre = _block_diag(jnp.transpose(b_re.reshape(ncol, gpc, state, S5_GROUP), (0, 1, 3, 2)))
    bim = _block_diag(jnp.transpose(b_im.reshape(ncol, gpc, state, S5_GROUP), (0, 1, 3, 2)))
    cre = _block_diag(jnp.transpose(c_re.reshape(ncol, gpc, S5_GROUP, state), (0, 1, 3, 2))).astype(BF16)
    cim = _block_diag(jnp.transpose(c_im.reshape(ncol, gpc, S5_GROUP, state), (0, 1, 3, 2))).astype(BF16)
    ldt = jnp.repeat(log_dt, state)
    params = [a_re_log.reshape(1, nstate), a_im.reshape(1, nstate), ldt.reshape(1, nstate),
              a_re_log.reshape(nslab, LANES), a_im.reshape(nslab, LANES), ldt.reshape(nslab, LANES),
              bre, bim, cre, cim, d.reshape(1, s5w), glu_w.astype(BF16), glu_b.reshape(1, s5w),
              hg_gn.reshape(1, hw), lb_logits]
    scratch = [
        pltpu.VMEM((ncol, cin, 2 * S5_LANES), BF16),
        pltpu.VMEM((2, nslab, LANES), F32),
        pltpu.VMEM((batch, 2, nslab // SUBLANES, SUBLANES, LANES), F32),
        pltpu.VMEM((batch, 2 * nslab * SLAB_PITCH, LANES), F32),
        pltpu.VMEM((SUBLANES, 2 * HG_DK, 2 * CHUNK), BF16),
        pltpu.VMEM((batch, HG_HEADS, HG_DK, HG_DK), F32),
    ]
    return _layer(h3, g_in, w_in, w_out, g_out, params, mixer=functools.partial(_mixer_cd, layer=layer),
                  init=_mixer_cd_init, cols=(s5w,) * 5, scratch=scratch, name="layer_cd", tn=MXU_COLS,
                  lp=h3.shape[1])


def _pack_lru_gates(wa, wi):
    nblk, bd, _ = wa.shape
    per = LANES // bd
    wa_bd = _block_diag(wa.reshape(nblk // per, per, bd, bd))
    wi_bd = _block_diag(wi.reshape(nblk // per, per, bd, bd))
    return jnp.concatenate([wa_bd, wi_bd], axis=2).astype(BF16)


def kernel(x, meta, w_in_ab, w_out_ab, ret_gn, rg_wa, rg_ba, rg_wi, rg_bi, rg_lam, rg_conv_w, rg_conv_b,
           w_in_cd, w_out_cd, s5_a_re_log, s5_a_im, s5_b_re, s5_b_im, s5_c_re, s5_c_im, s5_d, s5_log_dt,
           s5_glu_w, s5_glu_b, hg_gn, hg_lb_logits, norm_g, mlp_w1, mlp_w2):
    batch, seq, d = x.shape
    depth = norm_g.shape[0]
    used = PAD + N_META + seq
    assert used % CHUNK == 0
    lp = -(-used // (2 * CHUNK)) * (2 * CHUNK)
    m = batch * lp

    h = x
    for l in range(depth):
        jdx = l // 2
        if l % 2 == 0:
            h = _layer_ab(h, norm_g[l, 0], w_in_ab[jdx].astype(BF16), w_out_ab[jdx].astype(BF16), norm_g[l, 1],
                          ret_gn[jdx], _pack_lru_gates(rg_wa[jdx], rg_wi[jdx]), rg_ba[jdx], rg_bi[jdx],
                          rg_lam[jdx], rg_conv_w[jdx], rg_conv_b[jdx], lp=lp,
                          meta=meta.astype(x.dtype) if l == 0 else None)
        else:
            h = _layer_cd(h, norm_g[l, 0], w_in_cd[jdx].astype(BF16), w_out_cd[jdx].astype(BF16), norm_g[l, 1],
                          s5_a_re_log[jdx], s5_a_im[jdx], s5_b_re[jdx], s5_b_im[jdx], s5_c_re[jdx], s5_c_im[jdx],
                          s5_d[jdx], s5_log_dt[jdx], s5_glu_w[jdx], s5_glu_b[jdx], hg_gn[jdx], hg_lb_logits,
                          layer=l)
        if l + 1 < depth:
            h = _mlp(h.reshape(m, d), norm_g[l, 2], mlp_w1, mlp_w2, norm_g[l, 3], layer=l,
                     tm=_largest_divisor(lp, 528), tn=1024, rows_per_batch=lp).reshape(batch, lp, d)
        else:
            h = _mlp_frames(h, norm_g[l, 2], mlp_w1, mlp_w2, norm_g[l, 3], layer=l, first=PAD + N_META,
                            count=seq, tm=_largest_divisor(seq, 512), tn=1024)
    return h
```

```python
import functools
import math

import jax
import jax.numpy as jnp
from jax import lax
from jax.experimental import pallas as pl
from jax.experimental.pallas import tpu as pltpu

F32 = jnp.float32
BF16 = jnp.bfloat16

CHUNK = 64
N_META = 16
PAD = CHUNK - N_META
EPS = 1e-6

RET_HEADS = 4
RET_DK = 128
RET_DV = 256
ROPE_BASE = 10000.0
LRU_C = 8.0
CONV_WIDTH = 4
S5_GROUP = 16
S5_STATE = 64
S5_LANES = 512
HG_HEADS = 4
HG_DK = 128
LANES = 128
SUBLANES = 8
MXU_COLS = 256
SLAB_PITCH = CHUNK + SUBLANES
VMEM_LIMIT_BYTES = 56 * 1024 * 1024


def _largest_divisor(n, cap):
    return max(d for d in range(1, cap + 1) if n % d == 0)


def _gelu_tanh(x):
    return 0.5 * x * (1.0 + jnp.tanh(0.7978845608028654 * (x + 0.044715 * x * x * x)))


def _sigmoid(x):
    return 0.5 * jnp.tanh(0.5 * x) + 0.5


def _silu(x):
    hx = 0.5 * x
    return hx + hx * jnp.tanh(hx)


def _sqrt_nonneg(x):
    return jnp.where(x > 0.0, x * lax.rsqrt(x), 0.0)


def _dot(a, b):
    return jnp.dot(a, b, preferred_element_type=F32)


def _dot_nt(a, b):
    return lax.dot_general(a, b, (((1,), (1,)), ((), ())), preferred_element_type=F32)


def _keep_rows(out, tiles_per_batch):
    first_tile = lax.rem(pl.program_id(0), tiles_per_batch) == 0
    row = lax.broadcasted_iota(jnp.int32, (out.shape[0], 1), 0)
    keep = jnp.logical_or(row >= PAD, jnp.logical_not(first_tile))
    return jnp.where(keep, out, 0.0)


def _mlp_rows(x, g_in_ref, w1_ref, w2_ref, g_out_ref, tn):
    ms = jnp.mean(x * x, axis=-1, keepdims=True)
    hn = (x * lax.rsqrt(ms + EPS) * g_in_ref[...]).astype(BF16)
    acc = jnp.zeros(x.shape, F32)
    for j in range(w1_ref.shape[1] // tn):
        a = jnp.maximum(_dot(hn, w1_ref[:, j * tn:(j + 1) * tn].astype(BF16)), 0.0)
        acc = acc + _dot((a * a).astype(BF16), w2_ref[j * tn:(j + 1) * tn, :].astype(BF16))
    ms = jnp.mean(acc * acc, axis=-1, keepdims=True)
    return x + acc * lax.rsqrt(ms + EPS) * g_out_ref[...]


def _mlp_kernel(h_ref, g_in_ref, w1_ref, w2_ref, g_out_ref, o_ref, *, tiles_per_batch, tn):
    o_ref[...] = _keep_rows(_mlp_rows(h_ref[...], g_in_ref, w1_ref, w2_ref, g_out_ref, tn), tiles_per_batch)


def _mlp_frames_kernel(h_ref, g_in_ref, w1_ref, w2_ref, g_out_ref, o_ref, *, tn):
    o_ref[...] = _mlp_rows(h_ref[...], g_in_ref, w1_ref, w2_ref, g_out_ref, tn)


def _mlp_frames(h3, g_in, w1, w2, g_out, *, layer, first, count, tm, tn):
    batch, lp, d = h3.shape
    f = w1.shape[2]
    resident = pl.Buffered(1)
    return pl.pallas_call(
        functools.partial(_mlp_frames_kernel, tn=tn),
        grid=(batch, count // tm),
        in_specs=[
            pl.BlockSpec((pl.Element(tm), pl.Element(d)),
                         lambda b, i: (pl.multiple_of(b * lp + first + i * tm, SUBLANES), 0)),
            pl.BlockSpec((1, d), lambda b, i: (0, 0)),
            pl.BlockSpec((None, d, f), lambda b, i: (layer, 0, 0), pipeline_mode=resident),
            pl.BlockSpec((None, f, d), lambda b, i: (layer, 0, 0), pipeline_mode=resident),
            pl.BlockSpec((1, d), lambda b, i: (0, 0)),
        ],
        out_specs=pl.BlockSpec((None, tm, d), lambda b, i: (b, i, 0)),
        out_shape=jax.ShapeDtypeStruct((batch, count, d), F32),
        compiler_params=pltpu.CompilerParams(
            dimension_semantics=("arbitrary", "arbitrary"), vmem_limit_bytes=VMEM_LIMIT_BYTES),
        name="mlp_frames",
    )(h3.reshape(batch * lp, d), g_in.reshape(1, d), w1, w2, g_out.reshape(1, d))


def _mlp(h, g_in, w1, w2, g_out, *, layer, tm, tn, rows_per_batch):
    m, d = h.shape
    f = w1.shape[2]
    resident = pl.Buffered(1)
    return pl.pallas_call(
        functools.partial(_mlp_kernel, tiles_per_batch=rows_per_batch // tm, tn=tn),
        grid=(m // tm,),
        in_specs=[
            pl.BlockSpec((tm, d), lambda i: (i, 0)),
            pl.BlockSpec((1, d), lambda i: (0, 0)),
            pl.BlockSpec((None, d, f), lambda i: (layer, 0, 0), pipeline_mode=resident),
            pl.BlockSpec((None, f, d), lambda i: (layer, 0, 0), pipeline_mode=resident),
            pl.BlockSpec((1, d), lambda i: (0, 0)),
        ],
        out_specs=pl.BlockSpec((tm, d), lambda i: (i, 0)),
        out_shape=jax.ShapeDtypeStruct((m, d), F32),
        compiler_params=pltpu.CompilerParams(
            dimension_semantics=("arbitrary",), vmem_limit_bytes=VMEM_LIMIT_BYTES),
        name="mlp",
    )(h, g_in.reshape(1, d), w1, w2, g_out.reshape(1, d))


def _window_start(p, seq):
    return jnp.clip(2 * CHUNK * p - CHUNK, 0, seq - 2 * CHUNK)


class _Filler:
    def __init__(self, pieces):
        self._pieces = list(pieces)
        self._total = len(self._pieces)
        self._done = 0
        self._slot = 0
        self.nslots = 1

    def __call__(self):
        self._slot += 1
        target = min(self._total, -(-self._total * self._slot // self.nslots))
        while self._done < target:
            self._pieces[self._done]()
            self._done += 1

    def drain(self):
        while self._done < self._total:
            self._pieces[self._done]()
            self._done += 1


def _layer_kernel(*refs, mixer, init, nparams, cols, tn, nbatch, frames_seq, tail_is_padding):
    it = iter(refs)
    if frames_seq:
        xa_refs = [next(it) for _ in range(nbatch)]
        xc_refs = [next(it) for _ in range(nbatch)]
        meta_ref = next(it)
    else:
        ha_ref, hc_ref = next(it), next(it)
    g_in_ref, win_ref, wout_ref, g_out_ref = next(it), next(it), next(it), next(it)
    params = [next(it) for _ in range(nparams)]
    o_ref = next(it)
    z_refs = (next(it), next(it))
    y_refs = (next(it), next(it))
    hn_ref = next(it)
    scratch = list(it)
    d = o_ref.shape[2]
    p = pl.program_id(0)
    rows = nbatch * CHUNK

    if frames_seq:
        meta = meta_ref[...]
        meta_chunk = jnp.concatenate([jnp.zeros((CHUNK - meta.shape[0], d), F32), meta], axis=0)
        meta_rows = jnp.concatenate([meta_chunk] * nbatch, axis=0)

        def frame_rows(window_refs, window_start, chunk):
            off = jnp.clip((chunk - 1) * CHUNK - window_start, 0, CHUNK)
            off = pl.multiple_of(off, CHUNK)
            x = jnp.concatenate([r[pl.ds(off, CHUNK), :] for r in window_refs], axis=0)
            return jnp.where(chunk == 0, meta_rows, x)

    def load_in(e, s):
        if frames_seq:
            return frame_rows(xa_refs, _window_start(p, frames_seq), s)
        return ha_ref[:, e * CHUNK:(e + 1) * CHUNK, :].reshape(rows, d)

    def load_res(e, s):
        if frames_seq:
            return frame_rows(xc_refs, _window_start(p - 1, frames_seq), s - 2)
        return hc_ref[:, e * CHUNK:(e + 1) * CHUNK, :].reshape(rows, d)

    @pl.when(p == 0)
    def _():
        z_refs[1][...] = jnp.zeros_like(z_refs[1])
        y_refs[0][...] = jnp.zeros_like(y_refs[0])
        init(*params, *scratch)

    row = lax.broadcasted_iota(jnp.int32, (1, CHUNK, 1), 1)

    def out_pieces(e):
        s = 2 * p + e
        m_parts = []

        def piece(j):
            cs = slice(j * tn, (j + 1) * tn)
            y_in = y_refs[e][...].reshape(rows, y_refs[e].shape[2])
            m_parts.append(_dot(y_in, wout_ref[:, cs]))

        def finish():
            m = jnp.concatenate(m_parts, axis=1)
            ms = jnp.mean(m * m, axis=-1, keepdims=True)
            out = (load_res(e, s) + m * lax.rsqrt(ms + EPS) * g_out_ref[...]).reshape(nbatch, CHUNK, d)
            o_ref[:, e * CHUNK:(e + 1) * CHUNK, :] = jnp.where(jnp.logical_or(row >= PAD, s != 2), out, 0.0)

        return [functools.partial(piece, j) for j in range(d // tn)] + [finish]

    def in_pieces(e):
        def start():
            x = load_in(e, 2 * p + e)
            ms = jnp.mean(x * x, axis=-1, keepdims=True)
            hn_ref[...] = (x * lax.rsqrt(ms + EPS) * g_in_ref[...]).astype(BF16)

        def piece(j):
            cs = slice(j * tn, (j + 1) * tn)
            z_refs[e][:, :, cs] = _dot(hn_ref[...], win_ref[:, cs]).reshape(nbatch, CHUNK, tn)

        return [start] + [functools.partial(piece, j) for j in range(win_ref.shape[1] // tn)]

    def tick(e):
        fill = _Filler(out_pieces(e) + in_pieces(e))
        views = []
        off = 0
        for width in cols:
            views.append(z_refs[1 - e].at[:, :, off:off + width])
            off += width
        mixer(2 * p + e - 1, fill, *views, *params, y_refs[1 - e], *scratch)
        fill.drain()

    if not tail_is_padding:
        tick(0)
        tick(1)
    else:
        last = pl.num_programs(0) - 1

        @pl.when(p < last)
        def _():
            tick(0)
            tick(1)

        @pl.when(p == last)
        def _():
            for piece in out_pieces(0):
                piece()
            o_ref[:, CHUNK:, :] = jnp.zeros((nbatch, CHUNK, d), F32)


def _layer(h, g_in, w_in, w_out, g_out, params, *, mixer, init, cols, scratch, name, tn, lp, used, meta=None):
    batch, hlen, d = h.shape
    nin = w_in.shape[1]
    nout = w_out.shape[0]
    assert nin % tn == 0 and d % tn == 0
    nblk = lp // (2 * CHUNK)
    resident = pl.Buffered(1)

    def pspec(shape):
        return pl.BlockSpec(shape, lambda p: (0,) * len(shape))

    if meta is None:
        sources = [h, h]
        source_specs = [pl.BlockSpec((batch, 2 * CHUNK, d), lambda p: (0, jnp.minimum(p, nblk - 1), 0)),
                        pl.BlockSpec((batch, 2 * CHUNK, d), lambda p: (0, jnp.maximum(p - 1, 0), 0))]
    else:
        def window(b, lag):
            return pl.BlockSpec(
                (pl.Element(2 * CHUNK), pl.Element(d)),
                lambda p: (pl.multiple_of(b * hlen + _window_start(p - lag, hlen), CHUNK), 0))

        frames2 = h.reshape(batch * hlen, d)
        sources = [frames2] * (2 * batch) + [meta]
        source_specs = [window(b, lag) for lag in (0, 1) for b in range(batch)] + [pspec(meta.shape)]

    return pl.pallas_call(
        functools.partial(_layer_kernel, mixer=mixer, init=init, nparams=len(params), cols=cols, tn=tn,
                          nbatch=batch, frames_seq=None if meta is None else hlen,
                          tail_is_padding=lp - used >= CHUNK),
        grid=(nblk + 1,),
        in_specs=source_specs + [
            pspec((1, d)),
            pl.BlockSpec((d, nin), lambda p: (0, 0), pipeline_mode=resident),
            pl.BlockSpec((nout, d), lambda p: (0, 0), pipeline_mode=resident),
            pspec((1, d)),
        ] + [pspec(q.shape) for q in params],
        out_specs=pl.BlockSpec((batch, 2 * CHUNK, d), lambda p: (0, jnp.maximum(p - 1, 0), 0)),
        out_shape=jax.ShapeDtypeStruct((batch, lp, d), F32),
        scratch_shapes=[pltpu.VMEM((batch, CHUNK, nin), F32), pltpu.VMEM((batch, CHUNK, nin), F32),
                        pltpu.VMEM((batch, CHUNK, nout), BF16), pltpu.VMEM((batch, CHUNK, nout), BF16),
                        pltpu.VMEM((batch * CHUNK, d), BF16)] + scratch,
        compiler_params=pltpu.CompilerParams(
            dimension_semantics=("arbitrary",), vmem_limit_bytes=VMEM_LIMIT_BYTES),
        name=name,
    )(*sources, g_in.reshape(1, d), w_in, w_out, g_out.reshape(1, d), *params)


def _mixer_ab_init(gn_ref, wg_ref, ba_ref, bi_ref, lam_ref, cw_ref, cb_ref,
                   s_ref, xcar_ref, hcar_ref, a_slab, b_slab):
    s_ref[...] = jnp.zeros_like(s_ref)
    xcar_ref[...] = jnp.zeros_like(xcar_ref)
    hcar_ref[...] = jnp.zeros_like(hcar_ref)


def _mixer_ab(c, fill, q_ref, k_ref, v_ref, gate_ref, bx_ref, bg_ref, gn_ref, wg_ref, ba_ref, bi_ref,
              lam_ref, cw_ref, cb_ref, y_ref, s_ref, xcar_ref, hcar_ref, a_slab, b_slab):
    nbatch, _, width = bx_ref.shape
    ngrp = width // LANES
    scan_slots = CHUNK // SUBLANES
    fill.nslots = nbatch * RET_HEADS + ngrp + scan_slots

    row = lax.broadcasted_iota(jnp.int32, (CHUNK, 1), 0)
    idx = row.astype(F32)
    pos = (c * CHUNK + row - PAD).astype(F32)
    lane = lax.broadcasted_iota(jnp.int32, (1, RET_DK), 1)
    half = RET_DK // 2
    freq = jnp.exp((lane & (half - 1)).astype(F32) * (-math.log(ROPE_BASE) / half))
    ang = pos * freq
    cosv = jnp.cos(ang)
    sinv = jnp.where(lane < half, -1.0, 1.0) * jnp.sin(ang)
    ti = lax.broadcasted_iota(jnp.int32, (CHUNK, CHUNK), 0)
    si = lax.broadcasted_iota(jnp.int32, (CHUNK, CHUNK), 1)
    dist = jnp.abs(ti - si).astype(F32)

    def ret_body(b, carry):
        for h in range(RET_HEADS):
            log_g = math.log1p(-(2.0 ** (-5.0 - h)))
            qs = slice(h * RET_DK, (h + 1) * RET_DK)
            vs = slice(h * RET_DV, (h + 1) * RET_DV)
            qh = q_ref[b, :, qs]
            kh = k_ref[b, :, qs]
            qr = (qh * cosv + pltpu.roll(qh, half, 1) * sinv) * (RET_DK ** -0.5)
            kr = kh * cosv + pltpu.roll(kh, half, 1) * sinv
            vh = v_ref[b, :, vs].astype(BF16)
            scores = _dot_nt(qr.astype(BF16), kr.astype(BF16)) * jnp.exp(dist * log_g)
            o = _dot(scores.astype(BF16), vh)
            q_dec = qr * jnp.exp((idx + 1.0) * log_g)
            o = o + _dot(q_dec.astype(BF16), s_ref[b, h].astype(BF16))
            k_dec = kr * jnp.exp((CHUNK - 1.0 - idx) * log_g)
            kv = _dot(k_dec.T.astype(BF16), vh)
            s_ref[b, h] = math.exp(CHUNK * log_g) * s_ref[b, h] + kv
            oc = o - jnp.mean(o, axis=-1, keepdims=True)
            var = jnp.mean(oc * oc, axis=-1, keepdims=True)
            gt = gate_ref[b, :, vs]
            y_ref[b, :, vs] = (oc * lax.rsqrt(var + EPS) * gn_ref[:, vs]
                               * _silu(gt)).astype(y_ref.dtype)
            fill()
        return carry

    for b in range(nbatch):
        ret_body(b, 0)

    xb = bx_ref[...]
    xe = jnp.concatenate([xcar_ref[...], xb], axis=1)
    xc = cb_ref[...] + xb * cw_ref[CONV_WIDTH - 1:CONV_WIDTH, :]
    for s in range(1, CONV_WIDTH):
        xc = xc + pltpu.roll(xe, s, 1)[:, SUBLANES:, :] * cw_ref[CONV_WIDTH - 1 - s:CONV_WIDTH - s, :]
    xcar_ref[...] = xb[:, CHUNK - SUBLANES:, :]

    xc2 = xc.reshape(nbatch * CHUNK, width)
    xcb = xc2.astype(BF16)
    valid = jnp.logical_or(c > 0, jnp.logical_and(c == 0, row >= PAD))
    for p in range(ngrp):
        cs = slice(p * LANES, (p + 1) * LANES)
        g2 = _dot(xcb[:, cs], wg_ref[p])
        r = _sigmoid(g2[:, :LANES] + ba_ref[:, cs])
        i = _sigmoid(g2[:, LANES:] + bi_ref[:, cs])
        lam = lam_ref[:, cs]
        softplus_neg_lam = jnp.maximum(-lam, 0.0) + jnp.log1p(jnp.exp(-jnp.abs(lam)))
        a = jnp.exp(-LRU_C * r * softplus_neg_lam)
        bb = _sqrt_nonneg(1.0 - a * a) * (i * xc2[:, cs])
        for b in range(nbatch):
            rs = slice(b * CHUNK, (b + 1) * CHUNK)
            a_slab[b, p * SLAB_PITCH:p * SLAB_PITCH + CHUNK, :] = a[rs]
            b_slab[b, p * SLAB_PITCH:p * SLAB_PITCH + CHUNK, :] = jnp.where(valid, bb[rs], 0.0)
        fill()

    def step(t, hs):
        out = []
        for b in range(nbatch):
            ts = pl.ds(t, ngrp, stride=SLAB_PITCH)
            h = a_slab[b, ts, :] * hs[b] + b_slab[b, ts, :]
            b_slab[b, ts, :] = h
            out.append(h)
        return tuple(out)

    hs = tuple(hcar_ref[b] for b in range(nbatch))
    for t in range(CHUNK):
        hs = step(t, hs)
        if t % SUBLANES == SUBLANES - 1:
            fill()
    for b in range(nbatch):
        hcar_ref[b] = hs[b]
        hfull = jnp.concatenate(
            [b_slab[b, p * SLAB_PITCH:p * SLAB_PITCH + CHUNK, :] for p in range(ngrp)], axis=1)
        y_ref[b, :, RET_HEADS * RET_DV:] = (_gelu_tanh(bg_ref[b]) * hfull).astype(y_ref.dtype)


def _layer_ab(h3, g_in, w_in, w_out, g_out, ret_gn, wg, ba, bi, lam, conv_w, conv_b, *, lp, used, meta=None):
    batch = h3.shape[0]
    qk = RET_HEADS * RET_DK
    vw = RET_HEADS * RET_DV
    lw = lam.shape[-1]
    assert lw == SUBLANES * LANES
    params = [ret_gn.reshape(1, vw), wg, ba.reshape(1, lw), bi.reshape(1, lw), lam.reshape(1, lw),
              conv_w, conv_b.reshape(1, lw)]
    scratch = [
        pltpu.VMEM((batch, RET_HEADS, RET_DK, RET_DV), F32),
        pltpu.VMEM((batch, SUBLANES, lw), F32),
        pltpu.VMEM((batch, SUBLANES, LANES), F32),
        pltpu.VMEM((batch, SUBLANES * SLAB_PITCH, LANES), F32),
        pltpu.VMEM((batch, SUBLANES * SLAB_PITCH, LANES), F32),
    ]
    return _layer(h3, g_in, w_in, w_out, g_out, params, mixer=_mixer_ab, init=_mixer_ab_init,
                  cols=(qk, qk, vw, vw, lw, lw), scratch=scratch, name="layer_ab", tn=2 * MXU_COLS,
                  lp=lp, used=used, meta=meta)


_HG_LEVELS = (8, 16, 32)


def _mixer_cd_init(arl_ref, aim_ref, ldt_ref, arl_t_ref, aim_t_ref, ldt_t_ref,
                   bre_ref, bim_ref, cre_ref, cim_ref, d_ref, gw_ref, gb_ref, gn_ref, lbl_ref,
                   bm_ref, ab_ref, hcar_ref, bu_slab, sel_ref, st_ref):
    def disc(ldt, arl, aim):
        dt = jnp.exp(ldt)
        a_re = -jnp.exp(arl)
        mag = jnp.exp(dt * a_re)
        return a_re, aim, mag * jnp.cos(dt * aim), mag * jnp.sin(dt * aim)

    _, _, t_re, t_im = disc(ldt_t_ref[...], arl_t_ref[...], aim_t_ref[...])
    ab_ref[0] = t_re
    ab_ref[1] = t_im
    a_re, a_im, ab_re, ab_im = disc(ldt_ref[...], arl_ref[...], aim_ref[...])
    den = a_re * a_re + a_im * a_im
    z_re = ((ab_re - 1.0) * a_re + ab_im * a_im) / den
    z_im = (ab_im * a_re - (ab_re - 1.0) * a_im) / den
    for jb in range(bre_ref.shape[0]):
        ls = slice(jb * S5_LANES, (jb + 1) * S5_LANES)
        zr = z_re[:, ls]
        zi = z_im[:, ls]
        bb_re = zr * bre_ref[jb] - zi * bim_ref[jb]
        bb_im = zr * bim_ref[jb] + zi * bre_ref[jb]
        bm_ref[jb] = jnp.concatenate([bb_re, bb_im], axis=1).astype(BF16)
    ri = lax.broadcasted_iota(jnp.int32, (2 * HG_DK, 2 * CHUNK), 0)
    ci = lax.broadcasted_iota(jnp.int32, (2 * HG_DK, 2 * CHUNK), 1)
    same_head = (ri >= HG_DK) == (ci >= CHUNK)
    for s in range(SUBLANES):
        sel_ref[s] = jnp.logical_and(same_head, (ci & (SUBLANES - 1)) == s).astype(BF16)
    hcar_ref[...] = jnp.zeros_like(hcar_ref)
    st_ref[...] = jnp.zeros_like(st_ref)


def _mixer_cd(c, fill, u_ref, q_ref, f_ref, i_ref, g_ref,
              arl_ref, aim_ref, ldt_ref, arl_t_ref, aim_t_ref, ldt_t_ref,
              bre_ref, bim_ref, cre_ref, cim_ref, d_ref, gw_ref, gb_ref, gn_ref, lbl_ref, y_ref,
              bm_ref, ab_ref, hcar_ref, bu_slab, sel_ref, st_ref, *, layer):
    del c
    nbatch, _, s5w = u_ref.shape
    ncol = bre_ref.shape[0]
    cin = bre_ref.shape[1]
    nslab = ab_ref.shape[1]
    nhalf = nslab // SUBLANES
    gpc = S5_LANES // LANES
    hw = HG_HEADS * HG_DK
    pair = 2 * HG_DK
    scan_slots = CHUNK // SUBLANES
    hg_slots = len(_HG_LEVELS) + SUBLANES // 2
    fill.nslots = 2 * ncol + scan_slots + 1 + nbatch * hg_slots

    u2 = u_ref[...].reshape(nbatch * CHUNK, s5w)
    ub = u2.astype(BF16)
    for jb in range(ncol):
        bu = _dot(ub[:, jb * cin:(jb + 1) * cin], bm_ref[jb])
        for part in range(2):
            for gi in range(gpc):
                slab = part * nslab + jb * gpc + gi
                col = part * S5_LANES + gi * LANES
                for b in range(nbatch):
                    bu_slab[b, slab * SLAB_PITCH:slab * SLAB_PITCH + CHUNK, :] = (
                        bu[b * CHUNK:(b + 1) * CHUNK, col:col + LANES])
        fill()

    a_re = [ab_ref[0, hf * SUBLANES:(hf + 1) * SUBLANES, :] for hf in range(nhalf)]
    a_im = [ab_ref[1, hf * SUBLANES:(hf + 1) * SUBLANES, :] for hf in range(nhalf)]

    def step(t, hs):
        out = []
        for b in range(nbatch):
            for hf in range(nhalf):
                h_re, h_im = hs[2 * (b * nhalf + hf)], hs[2 * (b * nhalf + hf) + 1]
                ts_re = pl.ds(hf * SUBLANES * SLAB_PITCH + t, SUBLANES, stride=SLAB_PITCH)
                ts_im = pl.ds((nslab + hf * SUBLANES) * SLAB_PITCH + t, SUBLANES, stride=SLAB_PITCH)
                n_re = a_re[hf] * h_re - a_im[hf] * h_im + bu_slab[b, ts_re, :]
                n_im = a_re[hf] * h_im + a_im[hf] * h_re + bu_slab[b, ts_im, :]
                bu_slab[b, ts_re, :] = n_re
                bu_slab[b, ts_im, :] = n_im
                out += [n_re, n_im]
        return tuple(out)

    init = []
    for b in range(nbatch):
        for hf in range(nhalf):
            init += [hcar_ref[b, 0, hf], hcar_ref[b, 1, hf]]
    hs = tuple(init)
    for t in range(CHUNK):
        hs = step(t, hs)
        if t % SUBLANES == SUBLANES - 1:
            fill()
    for b in range(nbatch):
        for hf in range(nhalf):
            hcar_ref[b, 0, hf] = hs[2 * (b * nhalf + hf)]
            hcar_ref[b, 1, hf] = hs[2 * (b * nhalf + hf) + 1]

    def states(part, jb):
        return jnp.concatenate(
            [jnp.concatenate(
                [bu_slab[b, (part * nslab + jb * gpc + gi) * SLAB_PITCH:
                         (part * nslab + jb * gpc + gi) * SLAB_PITCH + CHUNK, :] for gi in range(gpc)], axis=1)
             for b in range(nbatch)], axis=0).astype(BF16)

    ys = []
    for jb in range(ncol):
        cs = slice(jb * cin, (jb + 1) * cin)
        y = _dot(states(0, jb), cre_ref[jb]) - _dot(states(1, jb), cim_ref[jb]) + d_ref[:, cs] * u2[:, cs]
        ys.append(_gelu_tanh(y))
        fill()
    yg = jnp.concatenate(ys, axis=1)
    yc = yg * _sigmoid(_dot(yg.astype(BF16), gw_ref[...]) + gb_ref[...])
    y_ref[:, :, :s5w] = yc.reshape(nbatch, CHUNK, s5w).astype(y_ref.dtype)
    fill()

    logits = lbl_ref[...]
    pexp = jnp.exp(logits - jnp.max(logits, axis=0, keepdims=True))
    psm = pexp / jnp.sum(pexp, axis=0, keepdims=True)
    lb = jnp.zeros_like(psm[0:1, :])
    for l in range(layer):
        lb = lb + psm[l:l + 1, :]

    ti = lax.broadcasted_iota(jnp.int32, (CHUNK, CHUNK), 0)
    si = lax.broadcasted_iota(jnp.int32, (CHUNK, CHUNK), 1)
    tril = (ti >= si).astype(BF16)
    tp = lax.broadcasted_iota(jnp.int32, (CHUNK, 2 * CHUNK), 0)
    sp = lax.broadcasted_iota(jnp.int32, (CHUNK, 2 * CHUNK), 1) & (CHUNK - 1)
    diag_mask = jnp.logical_and((tp >> 3) == (sp >> 3), sp <= tp)
    nvr = CHUNK // SUBLANES

    def both_heads(x, p):
        xa = x[:, p * pair:p * pair + HG_DK]
        xb_ = x[:, p * pair + HG_DK:(p + 1) * pair]
        zero = jnp.zeros_like(xa)
        return jnp.concatenate([jnp.concatenate([xa, zero], axis=1),
                                jnp.concatenate([zero, xb_], axis=1)], axis=0)

    def hg_body(b, carry):
        f = lb + (1.0 - lb) * jax.nn.sigmoid(f_ref[b])
        logf = jnp.log(f)
        kk = 1.0 - f
        lf_hi = logf.astype(BF16)
        lf_lo = (logf - lf_hi.astype(F32)).astype(BF16)
        cum = _dot(tril, lf_hi) + _dot(tril, lf_lo)
        total = cum[CHUNK - 1:CHUNK, :]
        q = q_ref[b]
        iv = i_ref[b]
        ivb = iv.astype(BF16)
        q_in = (q * jnp.exp(cum)).astype(BF16)
        k_dec = kk * jnp.exp(total - cum)
        dec = jnp.exp(total)

        att = [jnp.zeros((CHUNK, 2 * CHUNK), F32) for _ in range(HG_HEADS // 2)]
        ends = [cum[v * SUBLANES + SUBLANES - 1:(v + 1) * SUBLANES, :] for v in range(nvr)]
        zeros8 = jnp.zeros((SUBLANES, hw), F32)
        for n in _HG_LEVELS:
            per = n // SUBLANES
            qparts, kparts = [], []
            for v in range(nvr):
                blk = v // per
                vs = slice(v * SUBLANES, (v + 1) * SUBLANES)
                if blk % 2 == 1:
                    qparts.append(q[vs] * jnp.exp(cum[vs] - ends[blk * per - 1]))
                    kparts.append(zeros8)
                else:
                    qparts.append(zeros8)
                    kparts.append(kk[vs] * jnp.exp(ends[blk * per + per - 1] - cum[vs]))
            q_t = jnp.concatenate(qparts, axis=0).astype(BF16)
            k_t = jnp.concatenate(kparts, axis=0).astype(BF16)
            shift = n.bit_length()
            for p in range(HG_HEADS // 2):
                a_n = _dot_nt(q_t[:, p * pair:(p + 1) * pair], both_heads(k_t, p))
                if 2 * n < CHUNK:
                    a_n = jnp.where((tp >> shift) == (sp >> shift), a_n, 0.0)
                att[p] = att[p] + a_n
            fill()
        q3 = q.reshape(nvr, SUBLANES, hw)
        c3 = cum.reshape(nvr, SUBLANES, hw)
        e3 = (jnp.log(jnp.maximum(kk, 0.0)) - cum).reshape(nvr, SUBLANES, hw)
        dsum = [jnp.zeros((CHUNK, 2 * CHUNK), F32) for _ in range(HG_HEADS // 2)]
        for s in range(SUBLANES):
            w = q3 * jnp.exp(jnp.minimum(c3 + e3[:, s:s + 1, :], 0.0))
            wb = w.reshape(CHUNK, hw).astype(BF16)
            for p in range(HG_HEADS // 2):
                dsum[p] = dsum[p] + _dot(wb[:, p * pair:(p + 1) * pair], sel_ref[s])
            if s % 2 == 1:
                fill()
        for p in range(HG_HEADS // 2):
            a_all = (att[p] + jnp.where(diag_mask, dsum[p], 0.0)).astype(BF16)
            o_pair = _dot(a_all, both_heads(ivb, p))
            for hh in range(2):
                h = 2 * p + hh
                hs_ = slice(h * HG_DK, (h + 1) * HG_DK)
                st = st_ref[b, h]
                oh = o_pair[:, hh * HG_DK:(hh + 1) * HG_DK] + _dot_nt(q_in[:, hs_], st.astype(BF16))
                st_ref[b, h] = dec[:, hs_] * st + _dot(iv[:, hs_].T.astype(BF16), k_dec[:, hs_].astype(BF16))
                ms = jnp.mean(oh * oh, axis=-1, keepdims=True)
                gt = g_ref[b, :, hs_]
                y_ref[b, :, s5w + h * HG_DK:s5w + (h + 1) * HG_DK] = (
                    oh * lax.rsqrt(ms + EPS) * gn_ref[:, hs_] * _silu(gt)).astype(y_ref.dtype)
        return carry

    for b in range(nbatch):
        hg_body(b, 0)


def _block_diag(t):
    n, g, r, c = t.shape
    out = jnp.zeros((n, g, r, g, c), t.dtype)
    for gi in range(g):
        out = out.at[:, gi, :, gi, :].set(t[:, gi])
    return out.reshape(n, g * r, g * c)


def _layer_cd(h3, g_in, w_in, w_out, g_out, a_re_log, a_im, b_re, b_im, c_re, c_im, d, log_dt, glu_w, glu_b,
              hg_gn, lb_logits, *, layer, used):
    batch = h3.shape[0]
    groups, state = a_re_log.shape
    s5w = groups * S5_GROUP
    nstate = groups * state
    gpc = S5_LANES // state
    ncol = groups // gpc
    cin = gpc * S5_GROUP
    hw = HG_HEADS * HG_DK
    nslab = nstate // LANES
    assert s5w == hw and nslab % SUBLANES == 0

    bre = _block_diag(jnp.transpose(b_re.reshape(ncol, gpc, state, S5_GROUP), (0, 1, 3, 2)))
    bim = _block_diag(jnp.transpose(b_im.reshape(ncol, gpc, state, S5_GROUP), (0, 1, 3, 2)))
    cre = _block_diag(jnp.transpose(c_re.reshape(ncol, gpc, S5_GROUP, state), (0, 1, 3, 2))).astype(BF16)
    cim = _block_diag(jnp.transpose(c_im.reshape(ncol, gpc, S5_GROUP, state), (0, 1, 3, 2))).astype(BF16)
    ldt = jnp.repeat(log_dt, state)
    params = [a_re_log.reshape(1, nstate), a_im.reshape(1, nstate), ldt.reshape(1, nstate),
              a_re_log.reshape(nslab, LANES), a_im.reshape(nslab, LANES), ldt.reshape(nslab, LANES),
              bre, bim, cre, cim, d.reshape(1, s5w), glu_w.astype(BF16), glu_b.reshape(1, s5w),
              hg_gn.reshape(1, hw), lb_logits]
    scratch = [
        pltpu.VMEM((ncol, cin, 2 * S5_LANES), BF16),
        pltpu.VMEM((2, nslab, LANES), F32),
        pltpu.VMEM((batch, 2, nslab // SUBLANES, SUBLANES, LANES), F32),
        pltpu.VMEM((batch, 2 * nslab * SLAB_PITCH, LANES), F32),
        pltpu.VMEM((SUBLANES, 2 * HG_DK, 2 * CHUNK), BF16),
        pltpu.VMEM((batch, HG_HEADS, HG_DK, HG_DK), F32),
    ]
    return _layer(h3, g_in, w_in, w_out, g_out, params, mixer=functools.partial(_mixer_cd, layer=layer),
                  init=_mixer_cd_init, cols=(s5w,) * 5, scratch=scratch, name="layer_cd", tn=MXU_COLS,
                  lp=h3.shape[1], used=used)


def _pack_lru_gates(wa, wi):
    nblk, bd, _ = wa.shape
    per = LANES // bd
    wa_bd = _block_diag(wa.reshape(nblk // per, per, bd, bd))
    wi_bd = _block_diag(wi.reshape(nblk // per, per, bd, bd))
    return jnp.concatenate([wa_bd, wi_bd], axis=2).astype(BF16)


def kernel(x, meta, w_in_ab, w_out_ab, ret_gn, rg_wa, rg_ba, rg_wi, rg_bi, rg_lam, rg_conv_w, rg_conv_b,
           w_in_cd, w_out_cd, s5_a_re_log, s5_a_im, s5_b_re, s5_b_im, s5_c_re, s5_c_im, s5_d, s5_log_dt,
           s5_glu_w, s5_glu_b, hg_gn, hg_lb_logits, norm_g, mlp_w1, mlp_w2):
    batch, seq, d = x.shape
    depth = norm_g.shape[0]
    used = PAD + N_META + seq
    assert used % CHUNK == 0
    lp = -(-used // (2 * CHUNK)) * (2 * CHUNK)
    m = batch * lp

    h = x
    for l in range(depth):
        jdx = l // 2
        if l % 2 == 0:
            h = _layer_ab(h, norm_g[l, 0], w_in_ab[jdx].astype(BF16), w_out_ab[jdx].astype(BF16), norm_g[l, 1],
                          ret_gn[jdx], _pack_lru_gates(rg_wa[jdx], rg_wi[jdx]), rg_ba[jdx], rg_bi[jdx],
                          rg_lam[jdx], rg_conv_w[jdx], rg_conv_b[jdx], lp=lp, used=used,
                          meta=meta.astype(x.dtype) if l == 0 else None)
        else:
            h = _layer_cd(h, norm_g[l, 0], w_in_cd[jdx].astype(BF16), w_out_cd[jdx].astype(BF16), norm_g[l, 1],
                          s5_a_re_log[jdx], s5_a_im[jdx], s5_b_re[jdx], s5_b_im[jdx], s5_c_re[jdx], s5_c_im[jdx],
                          s5_d[jdx], s5_log_dt[jdx], s5_glu_w[jdx], s5_glu_b[jdx], hg_gn[jdx], hg_lb_logits,
                          layer=l, used=used)
        if l + 1 < depth:
            h = _mlp(h.reshape(m, d), norm_g[l, 2], mlp_w1, mlp_w2, norm_g[l, 3], layer=l,
                     tm=_largest_divisor(lp, 528), tn=1024, rows_per_batch=lp).reshape(batch, lp, d)
        else:
            h = _mlp_frames(h, norm_g[l, 2], mlp_w1, mlp_w2, norm_g[l, 3], layer=l, first=PAD + N_META,
                            count=seq, tm=_largest_divisor(seq, 512), tn=1024)
    return h
```

```python
import functools
import math

import jax
import jax.numpy as jnp
from jax import lax
from jax.experimental import pallas as pl
from jax.experimental.pallas import tpu as pltpu

F32 = jnp.float32
BF16 = jnp.bfloat16

CHUNK = 64
N_META = 16
PAD = CHUNK - N_META
EPS = 1e-6

RET_HEADS = 4
RET_DK = 128
RET_DV = 256
ROPE_BASE = 10000.0
LRU_C = 8.0
CONV_WIDTH = 4
S5_GROUP = 16
S5_STATE = 64
S5_LANES = 512
HG_HEADS = 4
HG_DK = 128
LANES = 128
SUBLANES = 8
MXU_COLS = 256
SLAB_PITCH = CHUNK + SUBLANES
VMEM_LIMIT_BYTES = 56 * 1024 * 1024


def _largest_divisor(n, cap):
    return max(d for d in range(1, cap + 1) if n % d == 0)


def _gelu_tanh(x):
    return 0.5 * x * (1.0 + jnp.tanh(0.7978845608028654 * (x + 0.044715 * x * x * x)))


def _sigmoid(x):
    return 0.5 * jnp.tanh(0.5 * x) + 0.5


def _silu(x):
    hx = 0.5 * x
    return hx + hx * jnp.tanh(hx)


def _sqrt_nonneg(x):
    return jnp.where(x > 0.0, x * lax.rsqrt(x), 0.0)


def _dot(a, b):
    return jnp.dot(a, b, preferred_element_type=F32)


def _dot_nt(a, b):
    return lax.dot_general(a, b, (((1,), (1,)), ((), ())), preferred_element_type=F32)


def _keep_rows(out, tiles_per_batch):
    first_tile = lax.rem(pl.program_id(0), tiles_per_batch) == 0
    row = lax.broadcasted_iota(jnp.int32, (out.shape[0], 1), 0)
    keep = jnp.logical_or(row >= PAD, jnp.logical_not(first_tile))
    return jnp.where(keep, out, 0.0)


def _mlp_rows(x, g_in_ref, w1_ref, w2_ref, g_out_ref, tn):
    ms = jnp.mean(x * x, axis=-1, keepdims=True)
    hn = (x * lax.rsqrt(ms + EPS) * g_in_ref[...]).astype(BF16)
    acc = jnp.zeros(x.shape, F32)
    for j in range(w1_ref.shape[1] // tn):
        a = jnp.maximum(_dot(hn, w1_ref[:, j * tn:(j + 1) * tn].astype(BF16)), 0.0)
        acc = acc + _dot((a * a).astype(BF16), w2_ref[j * tn:(j + 1) * tn, :].astype(BF16))
    ms = jnp.mean(acc * acc, axis=-1, keepdims=True)
    return x + acc * lax.rsqrt(ms + EPS) * g_out_ref[...]


def _mlp_kernel(h_ref, g_in_ref, w1_ref, w2_ref, g_out_ref, o_ref, *, tiles_per_batch, tn):
    o_ref[...] = _keep_rows(_mlp_rows(h_ref[...], g_in_ref, w1_ref, w2_ref, g_out_ref, tn), tiles_per_batch)


def _mlp_frames_kernel(h_ref, g_in_ref, w1_ref, w2_ref, g_out_ref, o_ref, *, tn):
    o_ref[...] = _mlp_rows(h_ref[...], g_in_ref, w1_ref, w2_ref, g_out_ref, tn)


def _mlp_frames(h3, g_in, w1, w2, g_out, *, layer, first, count, tm, tn):
    batch, lp, d = h3.shape
    f = w1.shape[2]
    resident = pl.Buffered(1)
    return pl.pallas_call(
        functools.partial(_mlp_frames_kernel, tn=tn),
        grid=(batch, count // tm),
        in_specs=[
            pl.BlockSpec((pl.Element(tm), pl.Element(d)),
                         lambda b, i: (pl.multiple_of(b * lp + first + i * tm, SUBLANES), 0)),
            pl.BlockSpec((1, d), lambda b, i: (0, 0)),
            pl.BlockSpec((None, d, f), lambda b, i: (layer, 0, 0), pipeline_mode=resident),
            pl.BlockSpec((None, f, d), lambda b, i: (layer, 0, 0), pipeline_mode=resident),
            pl.BlockSpec((1, d), lambda b, i: (0, 0)),
        ],
        out_specs=pl.BlockSpec((None, tm, d), lambda b, i: (b, i, 0)),
        out_shape=jax.ShapeDtypeStruct((batch, count, d), F32),
        compiler_params=pltpu.CompilerParams(
            dimension_semantics=("arbitrary", "arbitrary"), vmem_limit_bytes=VMEM_LIMIT_BYTES),
        name="mlp_frames",
    )(h3.reshape(batch * lp, d), g_in.reshape(1, d), w1, w2, g_out.reshape(1, d))


def _mlp(h, g_in, w1, w2, g_out, *, layer, tm, tn, rows_per_batch):
    m, d = h.shape
    f = w1.shape[2]
    resident = pl.Buffered(1)
    return pl.pallas_call(
        functools.partial(_mlp_kernel, tiles_per_batch=rows_per_batch // tm, tn=tn),
        grid=(m // tm,),
        in_specs=[
            pl.BlockSpec((tm, d), lambda i: (i, 0)),
            pl.BlockSpec((1, d), lambda i: (0, 0)),
            pl.BlockSpec((None, d, f), lambda i: (layer, 0, 0), pipeline_mode=resident),
            pl.BlockSpec((None, f, d), lambda i: (layer, 0, 0), pipeline_mode=resident),
            pl.BlockSpec((1, d), lambda i: (0, 0)),
        ],
        out_specs=pl.BlockSpec((tm, d), lambda i: (i, 0)),
        out_shape=jax.ShapeDtypeStruct((m, d), F32),
        compiler_params=pltpu.CompilerParams(
            dimension_semantics=("arbitrary",), vmem_limit_bytes=VMEM_LIMIT_BYTES),
        name="mlp",
    )(h, g_in.reshape(1, d), w1, w2, g_out.reshape(1, d))


def _window_start(p, seq):
    return jnp.clip(2 * CHUNK * p - CHUNK, 0, seq - 2 * CHUNK)


class _Filler:
    def __init__(self, pieces):
        self._pieces = list(pieces)
        self._total = len(self._pieces)
        self._done = 0
        self._slot = 0
        self.nslots = 1

    def __call__(self):
        self._slot += 1
        target = min(self._total, -(-self._total * self._slot // self.nslots))
        while self._done < target:
            self._pieces[self._done]()
            self._done += 1

    def drain(self):
        while self._done < self._total:
            self._pieces[self._done]()
            self._done += 1


def _layer_kernel(*refs, mixer, init, nparams, cols, tn, nbatch, frames_seq, tail_is_padding):
    it = iter(refs)
    if frames_seq:
        xa_refs = [next(it) for _ in range(nbatch)]
        xc_refs = [next(it) for _ in range(nbatch)]
        meta_ref = next(it)
    else:
        ha_ref, hc_ref = next(it), next(it)
    g_in_ref, win_ref, wout_ref, g_out_ref = next(it), next(it), next(it), next(it)
    params = [next(it) for _ in range(nparams)]
    o_ref = next(it)
    z_refs = (next(it), next(it))
    y_refs = (next(it), next(it))
    hn_ref = next(it)
    scratch = list(it)
    d = o_ref.shape[2]
    p = pl.program_id(0)
    rows = nbatch * CHUNK

    if frames_seq:
        meta = meta_ref[...]
        meta_chunk = jnp.concatenate([jnp.zeros((CHUNK - meta.shape[0], d), F32), meta], axis=0)
        meta_rows = jnp.concatenate([meta_chunk] * nbatch, axis=0)

        def frame_rows(window_refs, window_start, chunk):
            off = jnp.clip((chunk - 1) * CHUNK - window_start, 0, CHUNK)
            off = pl.multiple_of(off, CHUNK)
            x = jnp.concatenate([r[pl.ds(off, CHUNK), :] for r in window_refs], axis=0)
            return jnp.where(chunk == 0, meta_rows, x)

    def load_in(e, s):
        if frames_seq:
            return frame_rows(xa_refs, _window_start(p, frames_seq), s)
        return ha_ref[:, e * CHUNK:(e + 1) * CHUNK, :].reshape(rows, d)

    def load_res(e, s):
        if frames_seq:
            return frame_rows(xc_refs, _window_start(p - 1, frames_seq), s - 2)
        return hc_ref[:, e * CHUNK:(e + 1) * CHUNK, :].reshape(rows, d)

    row = lax.broadcasted_iota(jnp.int32, (1, CHUNK, 1), 1)

    def out_pieces(e):
        s = 2 * p + e
        m_parts = []

        def piece(j):
            cs = slice(j * tn, (j + 1) * tn)
            y_in = y_refs[e][...].reshape(rows, y_refs[e].shape[2])
            m_parts.append(_dot(y_in, wout_ref[:, cs]))

        def finish():
            m = jnp.concatenate(m_parts, axis=1)
            ms = jnp.mean(m * m, axis=-1, keepdims=True)
            out = (load_res(e, s) + m * lax.rsqrt(ms + EPS) * g_out_ref[...]).reshape(nbatch, CHUNK, d)
            o_ref[:, e * CHUNK:(e + 1) * CHUNK, :] = jnp.where(jnp.logical_or(row >= PAD, s != 2), out, 0.0)

        return [functools.partial(piece, j) for j in range(d // tn)] + [finish]

    def in_pieces(e):
        def start():
            x = load_in(e, 2 * p + e)
            ms = jnp.mean(x * x, axis=-1, keepdims=True)
            hn_ref[...] = (x * lax.rsqrt(ms + EPS) * g_in_ref[...]).astype(BF16)

        def piece(j):
            cs = slice(j * tn, (j + 1) * tn)
            z_refs[e][:, :, cs] = _dot(hn_ref[...], win_ref[:, cs]).reshape(nbatch, CHUNK, tn)

        return [start] + [functools.partial(piece, j) for j in range(win_ref.shape[1] // tn)]

    def tick(e, with_out=True):
        fill = _Filler((out_pieces(e) if with_out else []) + in_pieces(e))
        views = []
        off = 0
        for width in cols:
            views.append(z_refs[1 - e].at[:, :, off:off + width])
            off += width
        mixer(2 * p + e - 1, fill, *views, *params, y_refs[1 - e], *scratch)
        fill.drain()

    last = pl.num_programs(0) - 1

    @pl.when(p == 0)
    def _():
        init(*params, *scratch)
        for piece in in_pieces(0):
            piece()
        tick(1, with_out=False)

    @pl.when(jnp.logical_and(p > 0, p < last) if tail_is_padding else p > 0)
    def _():
        tick(0)
        tick(1)

    if tail_is_padding:
        @pl.when(p == last)
        def _():
            for piece in out_pieces(0):
                piece()
            o_ref[:, CHUNK:, :] = jnp.zeros((nbatch, CHUNK, d), F32)


def _layer(h, g_in, w_in, w_out, g_out, params, *, mixer, init, cols, scratch, name, tn, lp, used, meta=None):
    batch, hlen, d = h.shape
    nin = w_in.shape[1]
    nout = w_out.shape[0]
    assert nin % tn == 0 and d % tn == 0
    nblk = lp // (2 * CHUNK)
    resident = pl.Buffered(1)

    def pspec(shape):
        return pl.BlockSpec(shape, lambda p: (0,) * len(shape))

    if meta is None:
        sources = [h, h]
        source_specs = [pl.BlockSpec((batch, 2 * CHUNK, d), lambda p: (0, jnp.minimum(p, nblk - 1), 0)),
                        pl.BlockSpec((batch, 2 * CHUNK, d), lambda p: (0, jnp.maximum(p - 1, 0), 0))]
    else:
        def window(b, lag):
            return pl.BlockSpec(
                (pl.Element(2 * CHUNK), pl.Element(d)),
                lambda p: (pl.multiple_of(b * hlen + _window_start(p - lag, hlen), CHUNK), 0))

        frames2 = h.reshape(batch * hlen, d)
        sources = [frames2] * (2 * batch) + [meta]
        source_specs = [window(b, lag) for lag in (0, 1) for b in range(batch)] + [pspec(meta.shape)]

    return pl.pallas_call(
        functools.partial(_layer_kernel, mixer=mixer, init=init, nparams=len(params), cols=cols, tn=tn,
                          nbatch=batch, frames_seq=None if meta is None else hlen,
                          tail_is_padding=lp - used >= CHUNK),
        grid=(nblk + 1,),
        in_specs=source_specs + [
            pspec((1, d)),
            pl.BlockSpec((d, nin), lambda p: (0, 0), pipeline_mode=resident),
            pl.BlockSpec((nout, d), lambda p: (0, 0), pipeline_mode=resident),
            pspec((1, d)),
        ] + [pspec(q.shape) for q in params],
        out_specs=pl.BlockSpec((batch, 2 * CHUNK, d), lambda p: (0, jnp.maximum(p - 1, 0), 0)),
        out_shape=jax.ShapeDtypeStruct((batch, lp, d), F32),
        scratch_shapes=[pltpu.VMEM((batch, CHUNK, nin), F32), pltpu.VMEM((batch, CHUNK, nin), F32),
                        pltpu.VMEM((batch, CHUNK, nout), BF16), pltpu.VMEM((batch, CHUNK, nout), BF16),
                        pltpu.VMEM((batch * CHUNK, d), BF16)] + scratch,
        compiler_params=pltpu.CompilerParams(
            dimension_semantics=("arbitrary",), vmem_limit_bytes=VMEM_LIMIT_BYTES),
        name=name,
    )(*sources, g_in.reshape(1, d), w_in, w_out, g_out.reshape(1, d), *params)


def _mixer_ab_init(gn_ref, wg_ref, ba_ref, bi_ref, lam_ref, cw_ref, cb_ref,
                   s_ref, xcar_ref, hcar_ref, a_slab, b_slab):
    s_ref[...] = jnp.zeros_like(s_ref)
    xcar_ref[...] = jnp.zeros_like(xcar_ref)
    hcar_ref[...] = jnp.zeros_like(hcar_ref)


def _mixer_ab(c, fill, q_ref, k_ref, v_ref, gate_ref, bx_ref, bg_ref, gn_ref, wg_ref, ba_ref, bi_ref,
              lam_ref, cw_ref, cb_ref, y_ref, s_ref, xcar_ref, hcar_ref, a_slab, b_slab):
    nbatch, _, width = bx_ref.shape
    ngrp = width // LANES
    scan_slots = CHUNK // SUBLANES
    fill.nslots = nbatch * RET_HEADS + ngrp + scan_slots

    row = lax.broadcasted_iota(jnp.int32, (CHUNK, 1), 0)
    idx = row.astype(F32)
    pos = (c * CHUNK + row - PAD).astype(F32)
    lane = lax.broadcasted_iota(jnp.int32, (1, RET_DK), 1)
    half = RET_DK // 2
    freq = jnp.exp((lane & (half - 1)).astype(F32) * (-math.log(ROPE_BASE) / half))
    ang = pos * freq
    cosv = jnp.cos(ang)
    sinv = jnp.where(lane < half, -1.0, 1.0) * jnp.sin(ang)
    ti = lax.broadcasted_iota(jnp.int32, (CHUNK, CHUNK), 0)
    si = lax.broadcasted_iota(jnp.int32, (CHUNK, CHUNK), 1)
    dist = jnp.abs(ti - si).astype(F32)

    def ret_body(b, carry):
        for h in range(RET_HEADS):
            log_g = math.log1p(-(2.0 ** (-5.0 - h)))
            qs = slice(h * RET_DK, (h + 1) * RET_DK)
            vs = slice(h * RET_DV, (h + 1) * RET_DV)
            qh = q_ref[b, :, qs]
            kh = k_ref[b, :, qs]
            qr = (qh * cosv + pltpu.roll(qh, half, 1) * sinv) * (RET_DK ** -0.5)
            kr = kh * cosv + pltpu.roll(kh, half, 1) * sinv
            vh = v_ref[b, :, vs].astype(BF16)
            scores = _dot_nt(qr.astype(BF16), kr.astype(BF16)) * jnp.exp(dist * log_g)
            o = _dot(scores.astype(BF16), vh)
            q_dec = qr * jnp.exp((idx + 1.0) * log_g)
            o = o + _dot(q_dec.astype(BF16), s_ref[b, h].astype(BF16))
            k_dec = kr * jnp.exp((CHUNK - 1.0 - idx) * log_g)
            kv = _dot(k_dec.T.astype(BF16), vh)
            s_ref[b, h] = math.exp(CHUNK * log_g) * s_ref[b, h] + kv
            oc = o - jnp.mean(o, axis=-1, keepdims=True)
            var = jnp.mean(oc * oc, axis=-1, keepdims=True)
            gt = gate_ref[b, :, vs]
            y_ref[b, :, vs] = (oc * lax.rsqrt(var + EPS) * gn_ref[:, vs]
                               * _silu(gt)).astype(y_ref.dtype)
            fill()
        return carry

    for b in range(nbatch):
        ret_body(b, 0)

    xb = bx_ref[...]
    xe = jnp.concatenate([xcar_ref[...], xb], axis=1)
    xc = cb_ref[...] + xb * cw_ref[CONV_WIDTH - 1:CONV_WIDTH, :]
    for s in range(1, CONV_WIDTH):
        xc = xc + pltpu.roll(xe, s, 1)[:, SUBLANES:, :] * cw_ref[CONV_WIDTH - 1 - s:CONV_WIDTH - s, :]
    xcar_ref[...] = xb[:, CHUNK - SUBLANES:, :]

    xc2 = xc.reshape(nbatch * CHUNK, width)
    xcb = xc2.astype(BF16)
    valid = jnp.logical_or(c > 0, jnp.logical_and(c == 0, row >= PAD))
    for p in range(ngrp):
        cs = slice(p * LANES, (p + 1) * LANES)
        g2 = _dot(xcb[:, cs], wg_ref[p])
        r = _sigmoid(g2[:, :LANES] + ba_ref[:, cs])
        i = _sigmoid(g2[:, LANES:] + bi_ref[:, cs])
        lam = lam_ref[:, cs]
        softplus_neg_lam = jnp.maximum(-lam, 0.0) + jnp.log1p(jnp.exp(-jnp.abs(lam)))
        a = jnp.exp(-LRU_C * r * softplus_neg_lam)
        bb = _sqrt_nonneg(1.0 - a * a) * (i * xc2[:, cs])
        for b in range(nbatch):
            rs = slice(b * CHUNK, (b + 1) * CHUNK)
            a_slab[b, p * SLAB_PITCH:p * SLAB_PITCH + CHUNK, :] = a[rs]
            b_slab[b, p * SLAB_PITCH:p * SLAB_PITCH + CHUNK, :] = jnp.where(valid, bb[rs], 0.0)
        fill()

    def step(t, hs):
        out = []
        for b in range(nbatch):
            ts = pl.ds(t, ngrp, stride=SLAB_PITCH)
            h = a_slab[b, ts, :] * hs[b] + b_slab[b, ts, :]
            b_slab[b, ts, :] = h
            out.append(h)
        return tuple(out)

    hs = tuple(hcar_ref[b] for b in range(nbatch))
    for t in range(CHUNK):
        hs = step(t, hs)
        if t % SUBLANES == SUBLANES - 1:
            fill()
    for b in range(nbatch):
        hcar_ref[b] = hs[b]
        hfull = jnp.concatenate(
            [b_slab[b, p * SLAB_PITCH:p * SLAB_PITCH + CHUNK, :] for p in range(ngrp)], axis=1)
        y_ref[b, :, RET_HEADS * RET_DV:] = (_gelu_tanh(bg_ref[b]) * hfull).astype(y_ref.dtype)


def _layer_ab(h3, g_in, w_in, w_out, g_out, ret_gn, wg, ba, bi, lam, conv_w, conv_b, *, lp, used, meta=None):
    batch = h3.shape[0]
    qk = RET_HEADS * RET_DK
    vw = RET_HEADS * RET_DV
    lw = lam.shape[-1]
    assert lw == SUBLANES * LANES
    params = [ret_gn.reshape(1, vw), wg, ba.reshape(1, lw), bi.reshape(1, lw), lam.reshape(1, lw),
              conv_w, conv_b.reshape(1, lw)]
    scratch = [
        pltpu.VMEM((batch, RET_HEADS, RET_DK, RET_DV), F32),
        pltpu.VMEM((batch, SUBLANES, lw), F32),
        pltpu.VMEM((batch, SUBLANES, LANES), F32),
        pltpu.VMEM((batch, SUBLANES * SLAB_PITCH, LANES), F32),
        pltpu.VMEM((batch, SUBLANES * SLAB_PITCH, LANES), F32),
    ]
    return _layer(h3, g_in, w_in, w_out, g_out, params, mixer=_mixer_ab, init=_mixer_ab_init,
                  cols=(qk, qk, vw, vw, lw, lw), scratch=scratch, name="layer_ab", tn=2 * MXU_COLS,
                  lp=lp, used=used, meta=meta)


_HG_LEVELS = (8, 16, 32)


def _mixer_cd_init(arl_ref, aim_ref, ldt_ref, arl_t_ref, aim_t_ref, ldt_t_ref,
                   bre_ref, bim_ref, cre_ref, cim_ref, d_ref, gw_ref, gb_ref, gn_ref, lbl_ref,
                   bm_ref, ab_ref, hcar_ref, bu_slab, sel_ref, st_ref):
    def disc(ldt, arl, aim):
        dt = jnp.exp(ldt)
        a_re = -jnp.exp(arl)
        mag = jnp.exp(dt * a_re)
        return a_re, aim, mag * jnp.cos(dt * aim), mag * jnp.sin(dt * aim)

    _, _, t_re, t_im = disc(ldt_t_ref[...], arl_t_ref[...], aim_t_ref[...])
    ab_ref[0] = t_re
    ab_ref[1] = t_im
    a_re, a_im, ab_re, ab_im = disc(ldt_ref[...], arl_ref[...], aim_ref[...])
    den = a_re * a_re + a_im * a_im
    z_re = ((ab_re - 1.0) * a_re + ab_im * a_im) / den
    z_im = (ab_im * a_re - (ab_re - 1.0) * a_im) / den
    for jb in range(bre_ref.shape[0]):
        ls = slice(jb * S5_LANES, (jb + 1) * S5_LANES)
        zr = z_re[:, ls]
        zi = z_im[:, ls]
        bb_re = zr * bre_ref[jb] - zi * bim_ref[jb]
        bb_im = zr * bim_ref[jb] + zi * bre_ref[jb]
        bm_ref[jb] = jnp.concatenate([bb_re, bb_im], axis=1).astype(BF16)
    ri = lax.broadcasted_iota(jnp.int32, (2 * HG_DK, 2 * CHUNK), 0)
    ci = lax.broadcasted_iota(jnp.int32, (2 * HG_DK, 2 * CHUNK), 1)
    same_head = (ri >= HG_DK) == (ci >= CHUNK)
    for s in range(SUBLANES):
        sel_ref[s] = jnp.logical_and(same_head, (ci & (SUBLANES - 1)) == s).astype(BF16)
    hcar_ref[...] = jnp.zeros_like(hcar_ref)
    st_ref[...] = jnp.zeros_like(st_ref)


def _mixer_cd(c, fill, u_ref, q_ref, f_ref, i_ref, g_ref,
              arl_ref, aim_ref, ldt_ref, arl_t_ref, aim_t_ref, ldt_t_ref,
              bre_ref, bim_ref, cre_ref, cim_ref, d_ref, gw_ref, gb_ref, gn_ref, lbl_ref, y_ref,
              bm_ref, ab_ref, hcar_ref, bu_slab, sel_ref, st_ref, *, layer):
    del c
    nbatch, _, s5w = u_ref.shape
    ncol = bre_ref.shape[0]
    cin = bre_ref.shape[1]
    nslab = ab_ref.shape[1]
    nhalf = nslab // SUBLANES
    gpc = S5_LANES // LANES
    hw = HG_HEADS * HG_DK
    pair = 2 * HG_DK
    scan_slots = CHUNK // SUBLANES
    hg_slots = len(_HG_LEVELS) + SUBLANES // 2
    fill.nslots = 2 * ncol + scan_slots + 1 + nbatch * hg_slots

    u2 = u_ref[...].reshape(nbatch * CHUNK, s5w)
    ub = u2.astype(BF16)
    for jb in range(ncol):
        bu = _dot(ub[:, jb * cin:(jb + 1) * cin], bm_ref[jb])
        for part in range(2):
            for gi in range(gpc):
                slab = part * nslab + jb * gpc + gi
                col = part * S5_LANES + gi * LANES
                for b in range(nbatch):
                    bu_slab[b, slab * SLAB_PITCH:slab * SLAB_PITCH + CHUNK, :] = (
                        bu[b * CHUNK:(b + 1) * CHUNK, col:col + LANES])
        fill()

    a_re = [ab_ref[0, hf * SUBLANES:(hf + 1) * SUBLANES, :] for hf in range(nhalf)]
    a_im = [ab_ref[1, hf * SUBLANES:(hf + 1) * SUBLANES, :] for hf in range(nhalf)]

    def step(t, hs):
        out = []
        for b in range(nbatch):
            for hf in range(nhalf):
                h_re, h_im = hs[2 * (b * nhalf + hf)], hs[2 * (b * nhalf + hf) + 1]
                ts_re = pl.ds(hf * SUBLANES * SLAB_PITCH + t, SUBLANES, stride=SLAB_PITCH)
                ts_im = pl.ds((nslab + hf * SUBLANES) * SLAB_PITCH + t, SUBLANES, stride=SLAB_PITCH)
                n_re = a_re[hf] * h_re - a_im[hf] * h_im + bu_slab[b, ts_re, :]
                n_im = a_re[hf] * h_im + a_im[hf] * h_re + bu_slab[b, ts_im, :]
                bu_slab[b, ts_re, :] = n_re
                bu_slab[b, ts_im, :] = n_im
                out += [n_re, n_im]
        return tuple(out)

    init = []
    for b in range(nbatch):
        for hf in range(nhalf):
            init += [hcar_ref[b, 0, hf], hcar_ref[b, 1, hf]]
    hs = tuple(init)
    for t in range(CHUNK):
        hs = step(t, hs)
        if t % SUBLANES == SUBLANES - 1:
            fill()
    for b in range(nbatch):
        for hf in range(nhalf):
            hcar_ref[b, 0, hf] = hs[2 * (b * nhalf + hf)]
            hcar_ref[b, 1, hf] = hs[2 * (b * nhalf + hf) + 1]

    def states(part, jb):
        return jnp.concatenate(
            [jnp.concatenate(
                [bu_slab[b, (part * nslab + jb * gpc + gi) * SLAB_PITCH:
                         (part * nslab + jb * gpc + gi) * SLAB_PITCH + CHUNK, :] for gi in range(gpc)], axis=1)
             for b in range(nbatch)], axis=0).astype(BF16)

    ys = []
    for jb in range(ncol):
        cs = slice(jb * cin, (jb + 1) * cin)
        y = _dot(states(0, jb), cre_ref[jb]) - _dot(states(1, jb), cim_ref[jb]) + d_ref[:, cs] * u2[:, cs]
        ys.append(_gelu_tanh(y))
        fill()
    yg = jnp.concatenate(ys, axis=1)
    yc = yg * _sigmoid(_dot(yg.astype(BF16), gw_ref[...]) + gb_ref[...])
    y_ref[:, :, :s5w] = yc.reshape(nbatch, CHUNK, s5w).astype(y_ref.dtype)
    fill()

    logits = lbl_ref[...]
    pexp = jnp.exp(logits - jnp.max(logits, axis=0, keepdims=True))
    psm = pexp / jnp.sum(pexp, axis=0, keepdims=True)
    lb = jnp.zeros_like(psm[0:1, :])
    for l in range(layer):
        lb = lb + psm[l:l + 1, :]

    ti = lax.broadcasted_iota(jnp.int32, (CHUNK, CHUNK), 0)
    si = lax.broadcasted_iota(jnp.int32, (CHUNK, CHUNK), 1)
    tril = (ti >= si).astype(BF16)
    tp = lax.broadcasted_iota(jnp.int32, (CHUNK, 2 * CHUNK), 0)
    sp = lax.broadcasted_iota(jnp.int32, (CHUNK, 2 * CHUNK), 1) & (CHUNK - 1)
    diag_mask = jnp.logical_and((tp >> 3) == (sp >> 3), sp <= tp)
    nvr = CHUNK // SUBLANES

    def both_heads(x, p):
        xa = x[:, p * pair:p * pair + HG_DK]
        xb_ = x[:, p * pair + HG_DK:(p + 1) * pair]
        zero = jnp.zeros_like(xa)
        return jnp.concatenate([jnp.concatenate([xa, zero], axis=1),
                                jnp.concatenate([zero, xb_], axis=1)], axis=0)

    def hg_body(b, carry):
        f = lb + (1.0 - lb) * jax.nn.sigmoid(f_ref[b])
        logf = jnp.log(f)
        kk = 1.0 - f
        lf_hi = logf.astype(BF16)
        lf_lo = (logf - lf_hi.astype(F32)).astype(BF16)
        cum = _dot(tril, lf_hi) + _dot(tril, lf_lo)
        total = cum[CHUNK - 1:CHUNK, :]
        q = q_ref[b]
        iv = i_ref[b]
        ivb = iv.astype(BF16)
        q_in = (q * jnp.exp(cum)).astype(BF16)
        k_dec = kk * jnp.exp(total - cum)
        dec = jnp.exp(total)

        att = [jnp.zeros((CHUNK, 2 * CHUNK), F32) for _ in range(HG_HEADS // 2)]
        ends = [cum[v * SUBLANES + SUBLANES - 1:(v + 1) * SUBLANES, :] for v in range(nvr)]
        zeros8 = jnp.zeros((SUBLANES, hw), F32)
        for n in _HG_LEVELS:
            per = n // SUBLANES
            qparts, kparts = [], []
            for v in range(nvr):
                blk = v // per
                vs = slice(v * SUBLANES, (v + 1) * SUBLANES)
                if blk % 2 == 1:
                    qparts.append(q[vs] * jnp.exp(cum[vs] - ends[blk * per - 1]))
                    kparts.append(zeros8)
                else:
                    qparts.append(zeros8)
                    kparts.append(kk[vs] * jnp.exp(ends[blk * per + per - 1] - cum[vs]))
            q_t = jnp.concatenate(qparts, axis=0).astype(BF16)
            k_t = jnp.concatenate(kparts, axis=0).astype(BF16)
            shift = n.bit_length()
            for p in range(HG_HEADS // 2):
                a_n = _dot_nt(q_t[:, p * pair:(p + 1) * pair], both_heads(k_t, p))
                if 2 * n < CHUNK:
                    a_n = jnp.where((tp >> shift) == (sp >> shift), a_n, 0.0)
                att[p] = att[p] + a_n
            fill()
        q3 = q.reshape(nvr, SUBLANES, hw)
        c3 = cum.reshape(nvr, SUBLANES, hw)
        e3 = (jnp.log(jnp.maximum(kk, 0.0)) - cum).reshape(nvr, SUBLANES, hw)
        dsum = [jnp.zeros((CHUNK, 2 * CHUNK), F32) for _ in range(HG_HEADS // 2)]
        for s in range(SUBLANES):
            w = q3 * jnp.exp(jnp.minimum(c3 + e3[:, s:s + 1, :], 0.0))
            wb = w.reshape(CHUNK, hw).astype(BF16)
            for p in range(HG_HEADS // 2):
                dsum[p] = dsum[p] + _dot(wb[:, p * pair:(p + 1) * pair], sel_ref[s])
            if s % 2 == 1:
                fill()
        for p in range(HG_HEADS // 2):
            a_all = (att[p] + jnp.where(diag_mask, dsum[p], 0.0)).astype(BF16)
            o_pair = _dot(a_all, both_heads(ivb, p))
            for hh in range(2):
                h = 2 * p + hh
                hs_ = slice(h * HG_DK, (h + 1) * HG_DK)
                st = st_ref[b, h]
                oh = o_pair[:, hh * HG_DK:(hh + 1) * HG_DK] + _dot_nt(q_in[:, hs_], st.astype(BF16))
                st_ref[b, h] = dec[:, hs_] * st + _dot(iv[:, hs_].T.astype(BF16), k_dec[:, hs_].astype(BF16))
                ms = jnp.mean(oh * oh, axis=-1, keepdims=True)
                gt = g_ref[b, :, hs_]
                y_ref[b, :, s5w + h * HG_DK:s5w + (h + 1) * HG_DK] = (
                    oh * lax.rsqrt(ms + EPS) * gn_ref[:, hs_] * _silu(gt)).astype(y_ref.dtype)
        return carry

    for b in range(nbatch):
        hg_body(b, 0)


def _block_diag(t):
    n, g, r, c = t.shape
    out = jnp.zeros((n, g, r, g, c), t.dtype)
    for gi in range(g):
        out = out.at[:, gi, :, gi, :].set(t[:, gi])
    return out.reshape(n, g * r, g * c)


def _layer_cd(h3, g_in, w_in, w_out, g_out, a_re_log, a_im, b_re, b_im, c_re, c_im, d, log_dt, glu_w, glu_b,
              hg_gn, lb_logits, *, layer, used):
    batch = h3.shape[0]
    groups, state = a_re_log.shape
    s5w = groups * S5_GROUP
    nstate = groups * state
    gpc = S5_LANES // state
    ncol = groups // gpc
    cin = gpc * S5_GROUP
    hw = HG_HEADS * HG_DK
    nslab = nstate // LANES
    assert s5w == hw and nslab % SUBLANES == 0

    bre = _block_diag(jnp.transpose(b_re.reshape(ncol, gpc, state, S5_GROUP), (0, 1, 3, 2)))
    bim = _block_diag(jnp.transpose(b_im.reshape(ncol, gpc, state, S5_GROUP), (0, 1, 3, 2)))
    cre = _block_diag(jnp.transpose(c_re.reshape(ncol, gpc, S5_GROUP, state), (0, 1, 3, 2))).astype(BF16)
    cim = _block_diag(jnp.transpose(c_im.reshape(ncol, gpc, S5_GROUP, state), (0, 1, 3, 2))).astype(BF16)
    ldt = jnp.repeat(log_dt, state)
    params = [a_re_log.reshape(1, nstate), a_im.reshape(1, nstate), ldt.reshape(1, nstate),
              a_re_log.reshape(nslab, LANES), a_im.reshape(nslab, LANES), ldt.reshape(nslab, LANES),
              bre, bim, cre, cim, d.reshape(1, s5w), glu_w.astype(BF16), glu_b.reshape(1, s5w),
              hg_gn.reshape(1, hw), lb_logits]
    scratch = [
        pltpu.VMEM((ncol, cin, 2 * S5_LANES), BF16),
        pltpu.VMEM((2, nslab, LANES), F32),
        pltpu.VMEM((batch, 2, nslab // SUBLANES, SUBLANES, LANES), F32),
        pltpu.VMEM((batch, 2 * nslab * SLAB_PITCH, LANES), F32),
        pltpu.VMEM((SUBLANES, 2 * HG_DK, 2 * CHUNK), BF16),
        pltpu.VMEM((batch, HG_HEADS, HG_DK, HG_DK), F32),
    ]
    return _layer(h3, g_in, w_in, w_out, g_out, params, mixer=functools.partial(_mixer_cd, layer=layer),
                  init=_mixer_cd_init, cols=(s5w,) * 5, scratch=scratch, name="layer_cd", tn=MXU_COLS,
                  lp=h3.shape[1], used=used)


def _pack_lru_gates(wa, wi):
    nblk, bd, _ = wa.shape
    per = LANES // bd
    wa_bd = _block_diag(wa.reshape(nblk // per, per, bd, bd))
    wi_bd = _block_diag(wi.reshape(nblk // per, per, bd, bd))
    return jnp.concatenate([wa_bd, wi_bd], axis=2).astype(BF16)


def kernel(x, meta, w_in_ab, w_out_ab, ret_gn, rg_wa, rg_ba, rg_wi, rg_bi, rg_lam, rg_conv_w, rg_conv_b,
           w_in_cd, w_out_cd, s5_a_re_log, s5_a_im, s5_b_re, s5_b_im, s5_c_re, s5_c_im, s5_d, s5_log_dt,
           s5_glu_w, s5_glu_b, hg_gn, hg_lb_logits, norm_g, mlp_w1, mlp_w2):
    batch, seq, d = x.shape
    depth = norm_g.shape[0]
    used = PAD + N_META + seq
    assert used % CHUNK == 0
    lp = -(-used // (2 * CHUNK)) * (2 * CHUNK)
    m = batch * lp

    h = x
    for l in range(depth):
        jdx = l // 2
        if l % 2 == 0:
            h = _layer_ab(h, norm_g[l, 0], w_in_ab[jdx].astype(BF16), w_out_ab[jdx].astype(BF16), norm_g[l, 1],
                          ret_gn[jdx], _pack_lru_gates(rg_wa[jdx], rg_wi[jdx]), rg_ba[jdx], rg_bi[jdx],
                          rg_lam[jdx], rg_conv_w[jdx], rg_conv_b[jdx], lp=lp, used=used,
                          meta=meta.astype(x.dtype) if l == 0 else None)
        else:
            h = _layer_cd(h, norm_g[l, 0], w_in_cd[jdx].astype(BF16), w_out_cd[jdx].astype(BF16), norm_g[l, 1],
                          s5_a_re_log[jdx], s5_a_im[jdx], s5_b_re[jdx], s5_b_im[jdx], s5_c_re[jdx], s5_c_im[jdx],
                          s5_d[jdx], s5_log_dt[jdx], s5_glu_w[jdx], s5_glu_b[jdx], hg_gn[jdx], hg_lb_logits,
                          layer=l, used=used)
        if l + 1 < depth:
            h = _mlp(h.reshape(m, d), norm_g[l, 2], mlp_w1, mlp_w2, norm_g[l, 3], layer=l,
                     tm=_largest_divisor(lp, 528), tn=1024, rows_per_batch=lp).reshape(batch, lp, d)
        else:
            h = _mlp_frames(h, norm_g[l, 2], mlp_w1, mlp_w2, norm_g[l, 3], layer=l, first=PAD + N_META,
                            count=seq, tm=_largest_divisor(seq, 512), tn=1024)
    return h
```

```python
import functools
import math

import jax
import jax.numpy as jnp
from jax import lax
from jax.experimental import pallas as pl
from jax.experimental.pallas import tpu as pltpu

F32 = jnp.float32
BF16 = jnp.bfloat16

CHUNK = 64
N_META = 16
PAD = CHUNK - N_META
EPS = 1e-6

RET_HEADS = 4
RET_DK = 128
RET_DV = 256
ROPE_BASE = 10000.0
LRU_C = 8.0
CONV_WIDTH = 4
S5_GROUP = 16
S5_STATE = 64
S5_LANES = 512
HG_HEADS = 4
HG_DK = 128
LANES = 128
SUBLANES = 8
MXU_COLS = 256
SLAB_PITCH = CHUNK + SUBLANES
VMEM_LIMIT_BYTES = 56 * 1024 * 1024


MLP_ROWS = 704
MLP_ROW_ALIGN = 32


def _largest_divisor(n, cap, multiple_of=1):
    return max(d for d in range(multiple_of, cap + 1, multiple_of) if n % d == 0)


def _gelu_tanh(x):
    return 0.5 * x * (1.0 + jnp.tanh(0.7978845608028654 * (x + 0.044715 * x * x * x)))


def _sigmoid(x):
    return 0.5 * jnp.tanh(0.5 * x) + 0.5


def _silu(x):
    hx = 0.5 * x
    return hx + hx * jnp.tanh(hx)


def _sqrt_nonneg(x):
    return jnp.where(x > 0.0, x * lax.rsqrt(x), 0.0)


def _dot(a, b):
    return jnp.dot(a, b, preferred_element_type=F32)


def _dot_nt(a, b):
    return lax.dot_general(a, b, (((1,), (1,)), ((), ())), preferred_element_type=F32)


def _keep_rows(out, tiles_per_batch):
    first_tile = lax.rem(pl.program_id(0), tiles_per_batch) == 0
    row = lax.broadcasted_iota(jnp.int32, (out.shape[0], 1), 0)
    keep = jnp.logical_or(row >= PAD, jnp.logical_not(first_tile))
    return jnp.where(keep, out, 0.0)


def _mlp_rows(x, g_in_ref, w1_ref, w2_ref, g_out_ref, tn):
    nj = w1_ref.shape[1] // tn
    half = x.shape[0] // 2
    xs = (x[:half], x[half:])
    hn, acc, out = [None, None], [None, None], [None, None]

    def prologue(i):
        ms = jnp.mean(xs[i] * xs[i], axis=-1, keepdims=True)
        hn[i] = (xs[i] * lax.rsqrt(ms + EPS) * g_in_ref[...]).astype(BF16)

    def matmuls(i, j):
        a = jnp.maximum(_dot(hn[i], w1_ref[:, j * tn:(j + 1) * tn].astype(BF16)), 0.0)
        part = _dot((a * a).astype(BF16), w2_ref[j * tn:(j + 1) * tn, :].astype(BF16))
        acc[i] = part if j == 0 else acc[i] + part

    def epilogue(i):
        ms = jnp.mean(acc[i] * acc[i], axis=-1, keepdims=True)
        out[i] = xs[i] + acc[i] * lax.rsqrt(ms + EPS) * g_out_ref[...]

    prologue(0)
    matmuls(0, 0)
    prologue(1)
    for j in range(1, nj):
        matmuls(0, j)
        matmuls(1, j - 1)
    epilogue(0)
    matmuls(1, nj - 1)
    epilogue(1)
    return jnp.concatenate(out, axis=0)


def _mlp_kernel(h_ref, g_in_ref, w1_ref, w2_ref, g_out_ref, o_ref, *, tiles_per_batch, tn):
    o_ref[...] = _keep_rows(_mlp_rows(h_ref[...], g_in_ref, w1_ref, w2_ref, g_out_ref, tn), tiles_per_batch)


def _mlp_frames_kernel(h_ref, g_in_ref, w1_ref, w2_ref, g_out_ref, o_ref, *, tn):
    o_ref[...] = _mlp_rows(h_ref[...], g_in_ref, w1_ref, w2_ref, g_out_ref, tn)


def _mlp_frames(h3, g_in, w1, w2, g_out, *, layer, first, count, tm, tn):
    batch, lp, d = h3.shape
    f = w1.shape[2]
    resident = pl.Buffered(1)
    return pl.pallas_call(
        functools.partial(_mlp_frames_kernel, tn=tn),
        grid=(batch, count // tm),
        in_specs=[
            pl.BlockSpec((pl.Element(tm), pl.Element(d)),
                         lambda b, i: (pl.multiple_of(b * lp + first + i * tm, SUBLANES), 0)),
            pl.BlockSpec((1, d), lambda b, i: (0, 0)),
            pl.BlockSpec((None, d, f), lambda b, i: (layer, 0, 0), pipeline_mode=resident),
            pl.BlockSpec((None, f, d), lambda b, i: (layer, 0, 0), pipeline_mode=resident),
            pl.BlockSpec((1, d), lambda b, i: (0, 0)),
        ],
        out_specs=pl.BlockSpec((None, tm, d), lambda b, i: (b, i, 0)),
        out_shape=jax.ShapeDtypeStruct((batch, count, d), F32),
        compiler_params=pltpu.CompilerParams(
            dimension_semantics=("arbitrary", "arbitrary"), vmem_limit_bytes=VMEM_LIMIT_BYTES),
        name="mlp_frames",
    )(h3.reshape(batch * lp, d), g_in.reshape(1, d), w1, w2, g_out.reshape(1, d))


def _mlp(h, g_in, w1, w2, g_out, *, layer, tm, tn, rows_per_batch):
    m, d = h.shape
    f = w1.shape[2]
    resident = pl.Buffered(1)
    return pl.pallas_call(
        functools.partial(_mlp_kernel, tiles_per_batch=rows_per_batch // tm, tn=tn),
        grid=(m // tm,),
        in_specs=[
            pl.BlockSpec((tm, d), lambda i: (i, 0)),
            pl.BlockSpec((1, d), lambda i: (0, 0)),
            pl.BlockSpec((None, d, f), lambda i: (layer, 0, 0), pipeline_mode=resident),
            pl.BlockSpec((None, f, d), lambda i: (layer, 0, 0), pipeline_mode=resident),
            pl.BlockSpec((1, d), lambda i: (0, 0)),
        ],
        out_specs=pl.BlockSpec((tm, d), lambda i: (i, 0)),
        out_shape=jax.ShapeDtypeStruct((m, d), F32),
        compiler_params=pltpu.CompilerParams(
            dimension_semantics=("arbitrary",), vmem_limit_bytes=VMEM_LIMIT_BYTES),
        name="mlp",
    )(h, g_in.reshape(1, d), w1, w2, g_out.reshape(1, d))


def _window_start(p, seq):
    return jnp.clip(2 * CHUNK * p - CHUNK, 0, seq - 2 * CHUNK)


class _Filler:
    def __init__(self, pieces):
        self._pieces = list(pieces)
        self._total = len(self._pieces)
        self._done = 0
        self._slot = 0
        self.nslots = 1

    def __call__(self):
        self._slot += 1
        target = min(self._total, -(-self._total * self._slot // self.nslots))
        while self._done < target:
            self._pieces[self._done]()
            self._done += 1

    def drain(self):
        while self._done < self._total:
            self._pieces[self._done]()
            self._done += 1


def _layer_kernel(*refs, mixer, init, nparams, cols, tn, nbatch, frames_seq, tail_is_padding):
    it = iter(refs)
    if frames_seq:
        xa_refs = [next(it) for _ in range(nbatch)]
        xc_refs = [next(it) for _ in range(nbatch)]
        meta_ref = next(it)
    else:
        ha_ref, hc_ref = next(it), next(it)
    g_in_ref, win_ref, wout_ref, g_out_ref = next(it), next(it), next(it), next(it)
    params = [next(it) for _ in range(nparams)]
    o_ref = next(it)
    z_refs = (next(it), next(it))
    y_refs = (next(it), next(it))
    hn_ref = next(it)
    scratch = list(it)
    d = o_ref.shape[2]
    p = pl.program_id(0)
    rows = nbatch * CHUNK

    if frames_seq:
        meta = meta_ref[...]
        meta_chunk = jnp.concatenate([jnp.zeros((CHUNK - meta.shape[0], d), F32), meta], axis=0)
        meta_rows = jnp.concatenate([meta_chunk] * nbatch, axis=0)

        def frame_rows(window_refs, window_start, chunk):
            off = jnp.clip((chunk - 1) * CHUNK - window_start, 0, CHUNK)
            off = pl.multiple_of(off, CHUNK)
            x = jnp.concatenate([r[pl.ds(off, CHUNK), :] for r in window_refs], axis=0)
            return jnp.where(chunk == 0, meta_rows, x)

    def load_in(e, s):
        if frames_seq:
            return frame_rows(xa_refs, _window_start(p, frames_seq), s)
        return ha_ref[:, e * CHUNK:(e + 1) * CHUNK, :].reshape(rows, d)

    def load_res(e, s):
        if frames_seq:
            return frame_rows(xc_refs, _window_start(p - 1, frames_seq), s - 2)
        return hc_ref[:, e * CHUNK:(e + 1) * CHUNK, :].reshape(rows, d)

    row = lax.broadcasted_iota(jnp.int32, (1, CHUNK, 1), 1)

    def out_pieces(e):
        s = 2 * p + e
        m_parts = []

        def piece(j):
            cs = slice(j * tn, (j + 1) * tn)
            y_in = y_refs[e][...].reshape(rows, y_refs[e].shape[2])
            m_parts.append(_dot(y_in, wout_ref[:, cs]))

        def finish():
            m = jnp.concatenate(m_parts, axis=1)
            ms = jnp.mean(m * m, axis=-1, keepdims=True)
            out = (load_res(e, s) + m * lax.rsqrt(ms + EPS) * g_out_ref[...]).reshape(nbatch, CHUNK, d)
            o_ref[:, e * CHUNK:(e + 1) * CHUNK, :] = jnp.where(jnp.logical_or(row >= PAD, s != 2), out, 0.0)

        return [functools.partial(piece, j) for j in range(d // tn)] + [finish]

    def in_pieces(e):
        def start():
            x = load_in(e, 2 * p + e)
            ms = jnp.mean(x * x, axis=-1, keepdims=True)
            hn_ref[...] = (x * lax.rsqrt(ms + EPS) * g_in_ref[...]).astype(BF16)

        def piece(j):
            cs = slice(j * tn, (j + 1) * tn)
            z_refs[e][:, :, cs] = _dot(hn_ref[...], win_ref[:, cs]).reshape(nbatch, CHUNK, tn)

        return [start] + [functools.partial(piece, j) for j in range(win_ref.shape[1] // tn)]

    def tick(e, with_out=True):
        fill = _Filler((out_pieces(e) if with_out else []) + in_pieces(e))
        views = []
        off = 0
        for width in cols:
            views.append(z_refs[1 - e].at[:, :, off:off + width])
            off += width
        mixer(2 * p + e - 1, fill, *views, *params, y_refs[1 - e], *scratch)
        fill.drain()

    last = pl.num_programs(0) - 1

    @pl.when(p == 0)
    def _():
        init(*params, *scratch)
        for piece in in_pieces(0):
            piece()
        tick(1, with_out=False)

    @pl.when(jnp.logical_and(p > 0, p < last) if tail_is_padding else p > 0)
    def _():
        tick(0)
        tick(1)

    if tail_is_padding:
        @pl.when(p == last)
        def _():
            for piece in out_pieces(0):
                piece()
            o_ref[:, CHUNK:, :] = jnp.zeros((nbatch, CHUNK, d), F32)


def _layer(h, g_in, w_in, w_out, g_out, params, *, mixer, init, cols, scratch, name, tn, lp, used, meta=None):
    batch, hlen, d = h.shape
    nin = w_in.shape[1]
    nout = w_out.shape[0]
    assert nin % tn == 0 and d % tn == 0
    nblk = lp // (2 * CHUNK)
    resident = pl.Buffered(1)

    def pspec(shape):
        return pl.BlockSpec(shape, lambda p: (0,) * len(shape))

    if meta is None:
        sources = [h, h]
        source_specs = [pl.BlockSpec((batch, 2 * CHUNK, d), lambda p: (0, jnp.minimum(p, nblk - 1), 0)),
                        pl.BlockSpec((batch, 2 * CHUNK, d), lambda p: (0, jnp.maximum(p - 1, 0), 0))]
    else:
        def window(b, lag):
            return pl.BlockSpec(
                (pl.Element(2 * CHUNK), pl.Element(d)),
                lambda p: (pl.multiple_of(b * hlen + _window_start(p - lag, hlen), CHUNK), 0))

        frames2 = h.reshape(batch * hlen, d)
        sources = [frames2] * (2 * batch) + [meta]
        source_specs = [window(b, lag) for lag in (0, 1) for b in range(batch)] + [pspec(meta.shape)]

    return pl.pallas_call(
        functools.partial(_layer_kernel, mixer=mixer, init=init, nparams=len(params), cols=cols, tn=tn,
                          nbatch=batch, frames_seq=None if meta is None else hlen,
                          tail_is_padding=lp - used >= CHUNK),
        grid=(nblk + 1,),
        in_specs=source_specs + [
            pspec((1, d)),
            pl.BlockSpec((d, nin), lambda p: (0, 0), pipeline_mode=resident),
            pl.BlockSpec((nout, d), lambda p: (0, 0), pipeline_mode=resident),
            pspec((1, d)),
        ] + [pspec(q.shape) for q in params],
        out_specs=pl.BlockSpec((batch, 2 * CHUNK, d), lambda p: (0, jnp.maximum(p - 1, 0), 0)),
        out_shape=jax.ShapeDtypeStruct((batch, lp, d), F32),
        scratch_shapes=[pltpu.VMEM((batch, CHUNK, nin), F32), pltpu.VMEM((batch, CHUNK, nin), F32),
                        pltpu.VMEM((batch, CHUNK, nout), BF16), pltpu.VMEM((batch, CHUNK, nout), BF16),
                        pltpu.VMEM((batch * CHUNK, d), BF16)] + scratch,
        compiler_params=pltpu.CompilerParams(
            dimension_semantics=("arbitrary",), vmem_limit_bytes=VMEM_LIMIT_BYTES),
        name=name,
    )(*sources, g_in.reshape(1, d), w_in, w_out, g_out.reshape(1, d), *params)


def _mixer_ab_init(gn_ref, wg_ref, ba_ref, bi_ref, lam_ref, cw_ref, cb_ref,
                   s_ref, xcar_ref, hcar_ref, a_slab, b_slab):
    s_ref[...] = jnp.zeros_like(s_ref)
    xcar_ref[...] = jnp.zeros_like(xcar_ref)
    hcar_ref[...] = jnp.zeros_like(hcar_ref)


def _mixer_ab(c, fill, q_ref, k_ref, v_ref, gate_ref, bx_ref, bg_ref, gn_ref, wg_ref, ba_ref, bi_ref,
              lam_ref, cw_ref, cb_ref, y_ref, s_ref, xcar_ref, hcar_ref, a_slab, b_slab):
    nbatch, _, width = bx_ref.shape
    ngrp = width // LANES
    scan_slots = CHUNK // SUBLANES
    fill.nslots = nbatch * RET_HEADS + ngrp + scan_slots

    row = lax.broadcasted_iota(jnp.int32, (CHUNK, 1), 0)
    idx = row.astype(F32)
    pos = (c * CHUNK + row - PAD).astype(F32)
    lane = lax.broadcasted_iota(jnp.int32, (1, RET_DK), 1)
    half = RET_DK // 2
    freq = jnp.exp((lane & (half - 1)).astype(F32) * (-math.log(ROPE_BASE) / half))
    ang = pos * freq
    cosv = jnp.cos(ang)
    sinv = jnp.where(lane < half, -1.0, 1.0) * jnp.sin(ang)
    ti = lax.broadcasted_iota(jnp.int32, (CHUNK, CHUNK), 0)
    si = lax.broadcasted_iota(jnp.int32, (CHUNK, CHUNK), 1)
    dist = jnp.abs(ti - si).astype(F32)

    def ret_body(b, carry):
        for h in range(RET_HEADS):
            log_g = math.log1p(-(2.0 ** (-5.0 - h)))
            qs = slice(h * RET_DK, (h + 1) * RET_DK)
            vs = slice(h * RET_DV, (h + 1) * RET_DV)
            qh = q_ref[b, :, qs]
            kh = k_ref[b, :, qs]
            qr = (qh * cosv + pltpu.roll(qh, half, 1) * sinv) * (RET_DK ** -0.5)
            kr = kh * cosv + pltpu.roll(kh, half, 1) * sinv
            vh = v_ref[b, :, vs].astype(BF16)
            scores = _dot_nt(qr.astype(BF16), kr.astype(BF16)) * jnp.exp(dist * log_g)
            o = _dot(scores.astype(BF16), vh)
            q_dec = qr * jnp.exp((idx + 1.0) * log_g)
            o = o + _dot(q_dec.astype(BF16), s_ref[b, h].astype(BF16))
            k_dec = kr * jnp.exp((CHUNK - 1.0 - idx) * log_g)
            kv = _dot(k_dec.T.astype(BF16), vh)
            s_ref[b, h] = math.exp(CHUNK * log_g) * s_ref[b, h] + kv
            oc = o - jnp.mean(o, axis=-1, keepdims=True)
            var = jnp.mean(oc * oc, axis=-1, keepdims=True)
            gt = gate_ref[b, :, vs]
            y_ref[b, :, vs] = (oc * lax.rsqrt(var + EPS) * gn_ref[:, vs]
                               * _silu(gt)).astype(y_ref.dtype)
            fill()
        return carry

    for b in range(nbatch):
        ret_body(b, 0)

    xb = bx_ref[...]
    xe = jnp.concatenate([xcar_ref[...], xb], axis=1)
    xc = cb_ref[...] + xb * cw_ref[CONV_WIDTH - 1:CONV_WIDTH, :]
    for s in range(1, CONV_WIDTH):
        xc = xc + pltpu.roll(xe, s, 1)[:, SUBLANES:, :] * cw_ref[CONV_WIDTH - 1 - s:CONV_WIDTH - s, :]
    xcar_ref[...] = xb[:, CHUNK - SUBLANES:, :]

    xc2 = xc.reshape(nbatch * CHUNK, width)
    xcb = xc2.astype(BF16)
    valid = jnp.logical_or(c > 0, row >= PAD)
    for p in range(ngrp):
        cs = slice(p * LANES, (p + 1) * LANES)
        g2 = _dot(xcb[:, cs], wg_ref[p])
        r = _sigmoid(g2[:, :LANES] + ba_ref[:, cs])
        i = _sigmoid(g2[:, LANES:] + bi_ref[:, cs])
        lam = lam_ref[:, cs]
        softplus_neg_lam = jnp.maximum(-lam, 0.0) + jnp.log1p(jnp.exp(-jnp.abs(lam)))
        a = jnp.exp(-LRU_C * r * softplus_neg_lam)
        bb = _sqrt_nonneg(1.0 - a * a) * (i * xc2[:, cs])
        for b in range(nbatch):
            rs = slice(b * CHUNK, (b + 1) * CHUNK)
            a_slab[b, p * SLAB_PITCH:p * SLAB_PITCH + CHUNK, :] = a[rs]
            b_slab[b, p * SLAB_PITCH:p * SLAB_PITCH + CHUNK, :] = jnp.where(valid, bb[rs], 0.0)
        fill()

    def step(t, hs):
        out = []
        for b in range(nbatch):
            ts = pl.ds(t, ngrp, stride=SLAB_PITCH)
            h = a_slab[b, ts, :] * hs[b] + b_slab[b, ts, :]
            b_slab[b, ts, :] = h
            out.append(h)
        return tuple(out)

    hs = tuple(hcar_ref[b] for b in range(nbatch))
    for t in range(CHUNK):
        hs = step(t, hs)
        if t % SUBLANES == SUBLANES - 1:
            fill()
    for b in range(nbatch):
        hcar_ref[b] = hs[b]
        hfull = jnp.concatenate(
            [b_slab[b, p * SLAB_PITCH:p * SLAB_PITCH + CHUNK, :] for p in range(ngrp)], axis=1)
        y_ref[b, :, RET_HEADS * RET_DV:] = (_gelu_tanh(bg_ref[b]) * hfull).astype(y_ref.dtype)


def _layer_ab(h3, g_in, w_in, w_out, g_out, ret_gn, wg, ba, bi, lam, conv_w, conv_b, *, lp, used, meta=None):
    batch = h3.shape[0]
    qk = RET_HEADS * RET_DK
    vw = RET_HEADS * RET_DV
    lw = lam.shape[-1]
    assert lw == SUBLANES * LANES
    params = [ret_gn.reshape(1, vw), wg, ba.reshape(1, lw), bi.reshape(1, lw), lam.reshape(1, lw),
              conv_w, conv_b.reshape(1, lw)]
    scratch = [
        pltpu.VMEM((batch, RET_HEADS, RET_DK, RET_DV), F32),
        pltpu.VMEM((batch, SUBLANES, lw), F32),
        pltpu.VMEM((batch, SUBLANES, LANES), F32),
        pltpu.VMEM((batch, SUBLANES * SLAB_PITCH, LANES), F32),
        pltpu.VMEM((batch, SUBLANES * SLAB_PITCH, LANES), F32),
    ]
    return _layer(h3, g_in, w_in, w_out, g_out, params, mixer=_mixer_ab, init=_mixer_ab_init,
                  cols=(qk, qk, vw, vw, lw, lw), scratch=scratch, name="layer_ab", tn=2 * MXU_COLS,
                  lp=lp, used=used, meta=meta)


_HG_LEVELS = (8, 16, 32)


def _mixer_cd_init(arl_ref, aim_ref, ldt_ref, arl_t_ref, aim_t_ref, ldt_t_ref,
                   bre_ref, bim_ref, cre_ref, cim_ref, d_ref, gw_ref, gb_ref, gn_ref, lbl_ref,
                   bm_ref, ab_ref, hcar_ref, bu_slab, sel_ref, st_ref):
    def disc(ldt, arl, aim):
        dt = jnp.exp(ldt)
        a_re = -jnp.exp(arl)
        mag = jnp.exp(dt * a_re)
        return a_re, aim, mag * jnp.cos(dt * aim), mag * jnp.sin(dt * aim)

    _, _, t_re, t_im = disc(ldt_t_ref[...], arl_t_ref[...], aim_t_ref[...])
    ab_ref[0] = t_re
    ab_ref[1] = t_im
    a_re, a_im, ab_re, ab_im = disc(ldt_ref[...], arl_ref[...], aim_ref[...])
    den = a_re * a_re + a_im * a_im
    z_re = ((ab_re - 1.0) * a_re + ab_im * a_im) / den
    z_im = (ab_im * a_re - (ab_re - 1.0) * a_im) / den
    for jb in range(bre_ref.shape[0]):
        ls = slice(jb * S5_LANES, (jb + 1) * S5_LANES)
        zr = z_re[:, ls]
        zi = z_im[:, ls]
        bb_re = zr * bre_ref[jb] - zi * bim_ref[jb]
        bb_im = zr * bim_ref[jb] + zi * bre_ref[jb]
        bm_ref[jb] = jnp.concatenate([bb_re, bb_im], axis=1).astype(BF16)
    ri = lax.broadcasted_iota(jnp.int32, (2 * HG_DK, 2 * CHUNK), 0)
    ci = lax.broadcasted_iota(jnp.int32, (2 * HG_DK, 2 * CHUNK), 1)
    same_head = (ri >= HG_DK) == (ci >= CHUNK)
    for s in range(SUBLANES):
        sel_ref[s] = jnp.logical_and(same_head, (ci & (SUBLANES - 1)) == s).astype(BF16)
    hcar_ref[...] = jnp.zeros_like(hcar_ref)
    st_ref[...] = jnp.zeros_like(st_ref)


def _mixer_cd(c, fill, u_ref, q_ref, f_ref, i_ref, g_ref,
              arl_ref, aim_ref, ldt_ref, arl_t_ref, aim_t_ref, ldt_t_ref,
              bre_ref, bim_ref, cre_ref, cim_ref, d_ref, gw_ref, gb_ref, gn_ref, lbl_ref, y_ref,
              bm_ref, ab_ref, hcar_ref, bu_slab, sel_ref, st_ref, *, layer):
    del c
    nbatch, _, s5w = u_ref.shape
    ncol = bre_ref.shape[0]
    cin = bre_ref.shape[1]
    nslab = ab_ref.shape[1]
    nhalf = nslab // SUBLANES
    gpc = S5_LANES // LANES
    hw = HG_HEADS * HG_DK
    pair = 2 * HG_DK
    scan_slots = CHUNK // SUBLANES
    hg_slots = len(_HG_LEVELS) + SUBLANES // 2
    fill.nslots = 2 * ncol + scan_slots + 1 + nbatch * hg_slots

    u2 = u_ref[...].reshape(nbatch * CHUNK, s5w)
    ub = u2.astype(BF16)
    for jb in range(ncol):
        bu = _dot(ub[:, jb * cin:(jb + 1) * cin], bm_ref[jb])
        for part in range(2):
            for gi in range(gpc):
                slab = part * nslab + jb * gpc + gi
                col = part * S5_LANES + gi * LANES
                for b in range(nbatch):
                    bu_slab[b, slab * SLAB_PITCH:slab * SLAB_PITCH + CHUNK, :] = (
                        bu[b * CHUNK:(b + 1) * CHUNK, col:col + LANES])
        fill()

    a_re = [ab_ref[0, hf * SUBLANES:(hf + 1) * SUBLANES, :] for hf in range(nhalf)]
    a_im = [ab_ref[1, hf * SUBLANES:(hf + 1) * SUBLANES, :] for hf in range(nhalf)]

    def step(t, hs):
        out = []
        for b in range(nbatch):
            for hf in range(nhalf):
                h_re, h_im = hs[2 * (b * nhalf + hf)], hs[2 * (b * nhalf + hf) + 1]
                ts_re = pl.ds(hf * SUBLANES * SLAB_PITCH + t, SUBLANES, stride=SLAB_PITCH)
                ts_im = pl.ds((nslab + hf * SUBLANES) * SLAB_PITCH + t, SUBLANES, stride=SLAB_PITCH)
                n_re = a_re[hf] * h_re - a_im[hf] * h_im + bu_slab[b, ts_re, :]
                n_im = a_re[hf] * h_im + a_im[hf] * h_re + bu_slab[b, ts_im, :]
                bu_slab[b, ts_re, :] = n_re
                bu_slab[b, ts_im, :] = n_im
                out += [n_re, n_im]
        return tuple(out)

    init = []
    for b in range(nbatch):
        for hf in range(nhalf):
            init += [hcar_ref[b, 0, hf], hcar_ref[b, 1, hf]]
    hs = tuple(init)
    for t in range(CHUNK):
        hs = step(t, hs)
        if t % SUBLANES == SUBLANES - 1:
            fill()
    for b in range(nbatch):
        for hf in range(nhalf):
            hcar_ref[b, 0, hf] = hs[2 * (b * nhalf + hf)]
            hcar_ref[b, 1, hf] = hs[2 * (b * nhalf + hf) + 1]

    def states(part, jb):
        return jnp.concatenate(
            [jnp.concatenate(
                [bu_slab[b, (part * nslab + jb * gpc + gi) * SLAB_PITCH:
                         (part * nslab + jb * gpc + gi) * SLAB_PITCH + CHUNK, :] for gi in range(gpc)], axis=1)
             for b in range(nbatch)], axis=0).astype(BF16)

    ys = []
    for jb in range(ncol):
        cs = slice(jb * cin, (jb + 1) * cin)
        y = _dot(states(0, jb), cre_ref[jb]) - _dot(states(1, jb), cim_ref[jb]) + d_ref[:, cs] * u2[:, cs]
        ys.append(_gelu_tanh(y))
        fill()
    yg = jnp.concatenate(ys, axis=1)
    yc = yg * _sigmoid(_dot(yg.astype(BF16), gw_ref[...]) + gb_ref[...])
    y_ref[:, :, :s5w] = yc.reshape(nbatch, CHUNK, s5w).astype(y_ref.dtype)
    fill()

    logits = lbl_ref[...]
    pexp = jnp.exp(logits - jnp.max(logits, axis=0, keepdims=True))
    psm = pexp / jnp.sum(pexp, axis=0, keepdims=True)
    lb = jnp.zeros_like(psm[0:1, :])
    for l in range(layer):
        lb = lb + psm[l:l + 1, :]

    ti = lax.broadcasted_iota(jnp.int32, (CHUNK, CHUNK), 0)
    si = lax.broadcasted_iota(jnp.int32, (CHUNK, CHUNK), 1)
    tril = (ti >= si).astype(BF16)
    tp = lax.broadcasted_iota(jnp.int32, (CHUNK, 2 * CHUNK), 0)
    sp = lax.broadcasted_iota(jnp.int32, (CHUNK, 2 * CHUNK), 1) & (CHUNK - 1)
    diag_mask = jnp.logical_and((tp >> 3) == (sp >> 3), sp <= tp)
    nvr = CHUNK // SUBLANES

    def both_heads(x, p):
        xa = x[:, p * pair:p * pair + HG_DK]
        xb_ = x[:, p * pair + HG_DK:(p + 1) * pair]
        zero = jnp.zeros_like(xa)
        return jnp.concatenate([jnp.concatenate([xa, zero], axis=1),
                                jnp.concatenate([zero, xb_], axis=1)], axis=0)

    def hg_body(b, carry):
        f = lb + (1.0 - lb) * jax.nn.sigmoid(f_ref[b])
        logf = jnp.log(f)
        kk = 1.0 - f
        lf_hi = logf.astype(BF16)
        lf_lo = (logf - lf_hi.astype(F32)).astype(BF16)
        cum = _dot(tril, lf_hi) + _dot(tril, lf_lo)
        total = cum[CHUNK - 1:CHUNK, :]
        q = q_ref[b]
        iv = i_ref[b]
        ivb = iv.astype(BF16)
        q_in = (q * jnp.exp(cum)).astype(BF16)
        k_dec = kk * jnp.exp(total - cum)
        dec = jnp.exp(total)

        att = [jnp.zeros((CHUNK, 2 * CHUNK), F32) for _ in range(HG_HEADS // 2)]
        ends = [cum[v * SUBLANES + SUBLANES - 1:(v + 1) * SUBLANES, :] for v in range(nvr)]
        zeros8 = jnp.zeros((SUBLANES, hw), F32)
        for n in _HG_LEVELS:
            per = n // SUBLANES
            qparts, kparts = [], []
            for v in range(nvr):
                blk = v // per
                vs = slice(v * SUBLANES, (v + 1) * SUBLANES)
                if blk % 2 == 1:
                    qparts.append(q[vs] * jnp.exp(cum[vs] - ends[blk * per - 1]))
                    kparts.append(zeros8)
                else:
                    qparts.append(zeros8)
                    kparts.append(kk[vs] * jnp.exp(ends[blk * per + per - 1] - cum[vs]))
            q_t = jnp.concatenate(qparts, axis=0).astype(BF16)
            k_t = jnp.concatenate(kparts, axis=0).astype(BF16)
            shift = n.bit_length()
            for p in range(HG_HEADS // 2):
                a_n = _dot_nt(q_t[:, p * pair:(p + 1) * pair], both_heads(k_t, p))
                if 2 * n < CHUNK:
                    a_n = jnp.where((tp >> shift) == (sp >> shift), a_n, 0.0)
                att[p] = att[p] + a_n
            fill()
        q3 = q.reshape(nvr, SUBLANES, hw)
        c3 = cum.reshape(nvr, SUBLANES, hw)
        e3 = (jnp.log(jnp.maximum(kk, 0.0)) - cum).reshape(nvr, SUBLANES, hw)
        dsum = [jnp.zeros((CHUNK, 2 * CHUNK), F32) for _ in range(HG_HEADS // 2)]
        for s in range(SUBLANES):
            w = q3 * jnp.exp(jnp.minimum(c3 + e3[:, s:s + 1, :], 0.0))
            wb = w.reshape(CHUNK, hw).astype(BF16)
            for p in range(HG_HEADS // 2):
                dsum[p] = dsum[p] + _dot(wb[:, p * pair:(p + 1) * pair], sel_ref[s])
            if s % 2 == 1:
                fill()
        for p in range(HG_HEADS // 2):
            a_all = (att[p] + jnp.where(diag_mask, dsum[p], 0.0)).astype(BF16)
            o_pair = _dot(a_all, both_heads(ivb, p))
            for hh in range(2):
                h = 2 * p + hh
                hs_ = slice(h * HG_DK, (h + 1) * HG_DK)
                st = st_ref[b, h]
                oh = o_pair[:, hh * HG_DK:(hh + 1) * HG_DK] + _dot_nt(q_in[:, hs_], st.astype(BF16))
                st_ref[b, h] = dec[:, hs_] * st + _dot(iv[:, hs_].T.astype(BF16), k_dec[:, hs_].astype(BF16))
                ms = jnp.mean(oh * oh, axis=-1, keepdims=True)
                gt = g_ref[b, :, hs_]
                y_ref[b, :, s5w + h * HG_DK:s5w + (h + 1) * HG_DK] = (
                    oh * lax.rsqrt(ms + EPS) * gn_ref[:, hs_] * _silu(gt)).astype(y_ref.dtype)
        return carry

    for b in range(nbatch):
        hg_body(b, 0)


def _block_diag(t):
    n, g, r, c = t.shape
    on_diagonal = jnp.eye(g, dtype=bool)[None, :, None, :, None]
    return jnp.where(on_diagonal, t[:, :, :, None, :], 0).reshape(n, g * r, g * c)


def _layer_cd(h3, g_in, w_in, w_out, g_out, a_re_log, a_im, b_re, b_im, c_re, c_im, d, log_dt, glu_w, glu_b,
              hg_gn, lb_logits, *, layer, used):
    batch = h3.shape[0]
    groups, state = a_re_log.shape
    s5w = groups * S5_GROUP
    nstate = groups * state
    gpc = S5_LANES // state
    ncol = groups // gpc
    cin = gpc * S5_GROUP
    hw = HG_HEADS * HG_DK
    nslab = nstate // LANES
    assert s5w == hw and nslab % SUBLANES == 0

    bre = _block_diag(jnp.transpose(b_re.reshape(ncol, gpc, state, S5_GROUP), (0, 1, 3, 2)))
    bim = _block_diag(jnp.transpose(b_im.reshape(ncol, gpc, state, S5_GROUP), (0, 1, 3, 2)))
    cre = _block_diag(jnp.transpose(c_re.reshape(ncol, gpc, S5_GROUP, state), (0, 1, 3, 2))).astype(BF16)
    cim = _block_diag(jnp.transpose(c_im.reshape(ncol, gpc, S5_GROUP, state), (0, 1, 3, 2))).astype(BF16)
    ldt = jnp.repeat(log_dt, state)
    params = [a_re_log.reshape(1, nstate), a_im.reshape(1, nstate), ldt.reshape(1, nstate),
              a_re_log.reshape(nslab, LANES), a_im.reshape(nslab, LANES), ldt.reshape(nslab, LANES),
              bre, bim, cre, cim, d.reshape(1, s5w), glu_w.astype(BF16), glu_b.reshape(1, s5w),
              hg_gn.reshape(1, hw), lb_logits]
    scratch = [
        pltpu.VMEM((ncol, cin, 2 * S5_LANES), BF16),
        pltpu.VMEM((2, nslab, LANES), F32),
        pltpu.VMEM((batch, 2, nslab // SUBLANES, SUBLANES, LANES), F32),
        pltpu.VMEM((batch, 2 * nslab * SLAB_PITCH, LANES), F32),
        pltpu.VMEM((SUBLANES, 2 * HG_DK, 2 * CHUNK), BF16),
        pltpu.VMEM((batch, HG_HEADS, HG_DK, HG_DK), F32),
    ]
    return _layer(h3, g_in, w_in, w_out, g_out, params, mixer=functools.partial(_mixer_cd, layer=layer),
                  init=_mixer_cd_init, cols=(s5w,) * 5, scratch=scratch, name="layer_cd", tn=MXU_COLS,
                  lp=h3.shape[1], used=used)


def _pack_lru_gates(wa, wi):
    nblk, bd, _ = wa.shape
    per = LANES // bd
    wa_bd = _block_diag(wa.reshape(nblk // per, per, bd, bd))
    wi_bd = _block_diag(wi.reshape(nblk // per, per, bd, bd))
    return jnp.concatenate([wa_bd, wi_bd], axis=2).astype(BF16)


def kernel(x, meta, w_in_ab, w_out_ab, ret_gn, rg_wa, rg_ba, rg_wi, rg_bi, rg_lam, rg_conv_w, rg_conv_b,
           w_in_cd, w_out_cd, s5_a_re_log, s5_a_im, s5_b_re, s5_b_im, s5_c_re, s5_c_im, s5_d, s5_log_dt,
           s5_glu_w, s5_glu_b, hg_gn, hg_lb_logits, norm_g, mlp_w1, mlp_w2):
    batch, seq, d = x.shape
    depth = norm_g.shape[0]
    used = PAD + N_META + seq
    assert used % CHUNK == 0
    lp = -(-used // (2 * CHUNK)) * (2 * CHUNK)
    m = batch * lp

    h = x
    for l in range(depth):
        jdx = l // 2
        if l % 2 == 0:
            h = _layer_ab(h, norm_g[l, 0], w_in_ab[jdx].astype(BF16), w_out_ab[jdx].astype(BF16), norm_g[l, 1],
                          ret_gn[jdx], _pack_lru_gates(rg_wa[jdx], rg_wi[jdx]), rg_ba[jdx], rg_bi[jdx],
                          rg_lam[jdx], rg_conv_w[jdx], rg_conv_b[jdx], lp=lp, used=used,
                          meta=meta.astype(x.dtype) if l == 0 else None)
        else:
            h = _layer_cd(h, norm_g[l, 0], w_in_cd[jdx].astype(BF16), w_out_cd[jdx].astype(BF16), norm_g[l, 1],
                          s5_a_re_log[jdx], s5_a_im[jdx], s5_b_re[jdx], s5_b_im[jdx], s5_c_re[jdx], s5_c_im[jdx],
                          s5_d[jdx], s5_log_dt[jdx], s5_glu_w[jdx], s5_glu_b[jdx], hg_gn[jdx], hg_lb_logits,
                          layer=l, used=used)
        if l + 1 < depth:
            h = _mlp(h.reshape(m, d), norm_g[l, 2], mlp_w1, mlp_w2, norm_g[l, 3], layer=l,
                     tm=_largest_divisor(lp, MLP_ROWS, MLP_ROW_ALIGN), tn=1024,
                     rows_per_batch=lp).reshape(batch, lp, d)
        else:
            h = _mlp_frames(h, norm_g[l, 2], mlp_w1, mlp_w2, norm_g[l, 3], layer=l, first=PAD + N_META,
                            count=seq, tm=_largest_divisor(seq, MLP_ROWS, MLP_ROW_ALIGN), tn=1024)
    return h
```

```python
import functools
import math

import jax
import jax.numpy as jnp
from jax import lax
from jax.experimental import pallas as pl
from jax.experimental.pallas import tpu as pltpu

F32 = jnp.float32
BF16 = jnp.bfloat16

CHUNK = 64
N_META = 16
PAD = CHUNK - N_META
EPS = 1e-6

RET_HEADS = 4
RET_DK = 128
RET_DV = 256
ROPE_BASE = 10000.0
LRU_C = 8.0
CONV_WIDTH = 4
S5_GROUP = 16
S5_STATE = 64
S5_LANES = 512
HG_HEADS = 4
HG_DK = 128
LANES = 128
SUBLANES = 8
MXU_COLS = 256
SLAB_PITCH = CHUNK + SUBLANES
VMEM_LIMIT_BYTES = 56 * 1024 * 1024


MLP_ROWS = 704
MLP_ROW_ALIGN = 32


def _largest_divisor(n, cap, multiple_of=1):
    return max(d for d in range(multiple_of, cap + 1, multiple_of) if n % d == 0)


def _gelu_tanh(x):
    return 0.5 * x * (1.0 + jnp.tanh(0.7978845608028654 * (x + 0.044715 * x * x * x)))


def _sigmoid(x):
    return 0.5 * jnp.tanh(0.5 * x) + 0.5


def _silu(x):
    hx = 0.5 * x
    return hx + hx * jnp.tanh(hx)


def _sqrt_nonneg(x):
    return jnp.where(x > 0.0, x * lax.rsqrt(x), 0.0)


def _dot(a, b):
    return jnp.dot(a, b, preferred_element_type=F32)


def _dot_nt(a, b):
    return lax.dot_general(a, b, (((1,), (1,)), ((), ())), preferred_element_type=F32)


def _keep_rows(out, tiles_per_batch):
    first_tile = lax.rem(pl.program_id(0), tiles_per_batch) == 0
    row = lax.broadcasted_iota(jnp.int32, (out.shape[0], 1), 0)
    keep = jnp.logical_or(row >= PAD, jnp.logical_not(first_tile))
    return jnp.where(keep, out, 0.0)


def _mlp_rows(x, g_in_ref, w1_ref, w2_ref, g_out_ref, tn):
    nj = w1_ref.shape[1] // tn
    half = x.shape[0] // 2
    xs = (x[:half], x[half:])
    hn, acc, out = [None, None], [None, None], [None, None]

    def prologue(i):
        ms = jnp.mean(xs[i] * xs[i], axis=-1, keepdims=True)
        hn[i] = (xs[i] * lax.rsqrt(ms + EPS) * g_in_ref[...]).astype(BF16)

    def matmuls(i, j):
        a = jnp.maximum(_dot(hn[i], w1_ref[:, j * tn:(j + 1) * tn].astype(BF16)), 0.0)
        part = _dot((a * a).astype(BF16), w2_ref[j * tn:(j + 1) * tn, :].astype(BF16))
        acc[i] = part if j == 0 else acc[i] + part

    def epilogue(i):
        ms = jnp.mean(acc[i] * acc[i], axis=-1, keepdims=True)
        out[i] = xs[i] + acc[i] * lax.rsqrt(ms + EPS) * g_out_ref[...]

    prologue(0)
    matmuls(0, 0)
    prologue(1)
    for j in range(1, nj):
        matmuls(0, j)
        matmuls(1, j - 1)
    epilogue(0)
    matmuls(1, nj - 1)
    epilogue(1)
    return jnp.concatenate(out, axis=0)


def _mlp_kernel(h_ref, g_in_ref, w1_ref, w2_ref, g_out_ref, o_ref, *, tiles_per_batch, tn):
    o_ref[...] = _keep_rows(_mlp_rows(h_ref[...], g_in_ref, w1_ref, w2_ref, g_out_ref, tn), tiles_per_batch)


def _mlp_frames_kernel(h_ref, g_in_ref, w1_ref, w2_ref, g_out_ref, o_ref, *, tn):
    o_ref[...] = _mlp_rows(h_ref[...], g_in_ref, w1_ref, w2_ref, g_out_ref, tn)


def _mlp_frames(h3, g_in, w1, w2, g_out, *, layer, first, count, tm, tn):
    batch, lp, d = h3.shape
    f = w1.shape[2]
    resident = pl.Buffered(1)
    return pl.pallas_call(
        functools.partial(_mlp_frames_kernel, tn=tn),
        grid=(batch, count // tm),
        in_specs=[
            pl.BlockSpec((pl.Element(tm), pl.Element(d)),
                         lambda b, i: (pl.multiple_of(b * lp + first + i * tm, SUBLANES), 0)),
            pl.BlockSpec((1, d), lambda b, i: (0, 0)),
            pl.BlockSpec((None, d, f), lambda b, i: (layer, 0, 0), pipeline_mode=resident),
            pl.BlockSpec((None, f, d), lambda b, i: (layer, 0, 0), pipeline_mode=resident),
            pl.BlockSpec((1, d), lambda b, i: (0, 0)),
        ],
        out_specs=pl.BlockSpec((None, tm, d), lambda b, i: (b, i, 0)),
        out_shape=jax.ShapeDtypeStruct((batch, count, d), F32),
        compiler_params=pltpu.CompilerParams(
            dimension_semantics=("arbitrary", "arbitrary"), vmem_limit_bytes=VMEM_LIMIT_BYTES),
        name="mlp_frames",
    )(h3.reshape(batch * lp, d), g_in.reshape(1, d), w1, w2, g_out.reshape(1, d))


def _mlp(h, g_in, w1, w2, g_out, *, layer, tm, tn, rows_per_batch):
    m, d = h.shape
    f = w1.shape[2]
    resident = pl.Buffered(1)
    return pl.pallas_call(
        functools.partial(_mlp_kernel, tiles_per_batch=rows_per_batch // tm, tn=tn),
        grid=(m // tm,),
        in_specs=[
            pl.BlockSpec((tm, d), lambda i: (i, 0)),
            pl.BlockSpec((1, d), lambda i: (0, 0)),
            pl.BlockSpec((None, d, f), lambda i: (layer, 0, 0), pipeline_mode=resident),
            pl.BlockSpec((None, f, d), lambda i: (layer, 0, 0), pipeline_mode=resident),
            pl.BlockSpec((1, d), lambda i: (0, 0)),
        ],
        out_specs=pl.BlockSpec((tm, d), lambda i: (i, 0)),
        out_shape=jax.ShapeDtypeStruct((m, d), F32),
        compiler_params=pltpu.CompilerParams(
            dimension_semantics=("arbitrary",), vmem_limit_bytes=VMEM_LIMIT_BYTES),
        name="mlp",
    )(h, g_in.reshape(1, d), w1, w2, g_out.reshape(1, d))


def _window_start(p, seq):
    return jnp.clip(2 * CHUNK * p - CHUNK, 0, seq - 2 * CHUNK)


class _Filler:
    def __init__(self, pieces):
        self._pieces = list(pieces)
        self._total = len(self._pieces)
        self._done = 0
        self._slot = 0
        self.nslots = 1

    def __call__(self):
        self._slot += 1
        target = min(self._total, -(-self._total * self._slot // self.nslots))
        while self._done < target:
            self._pieces[self._done]()
            self._done += 1

    def drain(self):
        while self._done < self._total:
            self._pieces[self._done]()
            self._done += 1


def _layer_kernel(*refs, mixer, init, nparams, cols, tn, nbatch, frames_seq, tail_is_padding):
    it = iter(refs)
    if frames_seq:
        xa_refs = [next(it) for _ in range(nbatch)]
        xc_refs = [next(it) for _ in range(nbatch)]
        meta_ref = next(it)
    else:
        ha_ref, hc_ref = next(it), next(it)
    g_in_ref, win_ref, wout_ref, g_out_ref = next(it), next(it), next(it), next(it)
    params = [next(it) for _ in range(nparams)]
    o_ref = next(it)
    z_refs = (next(it), next(it))
    y_refs = (next(it), next(it))
    hn_ref = next(it)
    scratch = list(it)
    d = o_ref.shape[2]
    p = pl.program_id(0)
    rows = nbatch * CHUNK

    if frames_seq:
        meta = meta_ref[...]
        meta_chunk = jnp.concatenate([jnp.zeros((CHUNK - meta.shape[0], d), F32), meta], axis=0)
        meta_rows = jnp.concatenate([meta_chunk] * nbatch, axis=0)

        def frame_rows(window_refs, window_start, chunk):
            off = jnp.clip((chunk - 1) * CHUNK - window_start, 0, CHUNK)
            off = pl.multiple_of(off, CHUNK)
            x = jnp.concatenate([r[pl.ds(off, CHUNK), :] for r in window_refs], axis=0)
            return jnp.where(chunk == 0, meta_rows, x)

    def load_in(e, s):
        if frames_seq:
            return frame_rows(xa_refs, _window_start(p, frames_seq), s)
        return ha_ref[:, e * CHUNK:(e + 1) * CHUNK, :].reshape(rows, d)

    def load_res(e, s):
        if frames_seq:
            return frame_rows(xc_refs, _window_start(p - 1, frames_seq), s - 2)
        return hc_ref[:, e * CHUNK:(e + 1) * CHUNK, :].reshape(rows, d)

    row = lax.broadcasted_iota(jnp.int32, (1, CHUNK, 1), 1)

    def out_pieces(e):
        s = 2 * p + e
        m_parts = []

        def piece(j):
            cs = slice(j * tn, (j + 1) * tn)
            y_in = y_refs[e][...].reshape(rows, y_refs[e].shape[2])
            m_parts.append(_dot(y_in, wout_ref[:, cs]))

        def finish():
            m = jnp.concatenate(m_parts, axis=1)
            ms = jnp.mean(m * m, axis=-1, keepdims=True)
            out = (load_res(e, s) + m * lax.rsqrt(ms + EPS) * g_out_ref[...]).reshape(nbatch, CHUNK, d)
            o_ref[:, e * CHUNK:(e + 1) * CHUNK, :] = jnp.where(jnp.logical_or(row >= PAD, s != 2), out, 0.0)

        return [functools.partial(piece, j) for j in range(d // tn)] + [finish]

    def in_pieces(e):
        def start():
            x = load_in(e, 2 * p + e)
            ms = jnp.mean(x * x, axis=-1, keepdims=True)
            hn_ref[...] = (x * lax.rsqrt(ms + EPS) * g_in_ref[...]).astype(BF16)

        def piece(j):
            cs = slice(j * tn, (j + 1) * tn)
            z_refs[e][:, :, cs] = _dot(hn_ref[...], win_ref[:, cs]).reshape(nbatch, CHUNK, tn)

        return [start] + [functools.partial(piece, j) for j in range(win_ref.shape[1] // tn)]

    def tick(e, with_out=True):
        fill = _Filler((out_pieces(e) if with_out else []) + in_pieces(e))
        views = []
        off = 0
        for width in cols:
            views.append(z_refs[1 - e].at[:, :, off:off + width])
            off += width
        mixer(2 * p + e - 1, fill, *views, *params, y_refs[1 - e], *scratch)
        fill.drain()

    last = pl.num_programs(0) - 1

    @pl.when(p == 0)
    def _():
        init(*params, *scratch)
        for piece in in_pieces(0):
            piece()
        tick(1, with_out=False)

    @pl.when(jnp.logical_and(p > 0, p < last) if tail_is_padding else p > 0)
    def _():
        tick(0)
        tick(1)

    if tail_is_padding:
        @pl.when(p == last)
        def _():
            for piece in out_pieces(0):
                piece()
            o_ref[:, CHUNK:, :] = jnp.zeros((nbatch, CHUNK, d), F32)


def _layer(h, g_in, w_in, w_out, g_out, params, *, mixer, init, cols, scratch, name, tn, lp, used, meta=None):
    batch, hlen, d = h.shape
    nin = w_in.shape[1]
    nout = w_out.shape[0]
    assert nin % tn == 0 and d % tn == 0
    nblk = lp // (2 * CHUNK)
    resident = pl.Buffered(1)

    def pspec(shape):
        return pl.BlockSpec(shape, lambda p: (0,) * len(shape))

    if meta is None:
        sources = [h, h]
        source_specs = [pl.BlockSpec((batch, 2 * CHUNK, d), lambda p: (0, jnp.minimum(p, nblk - 1), 0)),
                        pl.BlockSpec((batch, 2 * CHUNK, d), lambda p: (0, jnp.maximum(p - 1, 0), 0))]
    else:
        def window(b, lag):
            return pl.BlockSpec(
                (pl.Element(2 * CHUNK), pl.Element(d)),
                lambda p: (pl.multiple_of(b * hlen + _window_start(p - lag, hlen), CHUNK), 0))

        frames2 = h.reshape(batch * hlen, d)
        sources = [frames2] * (2 * batch) + [meta]
        source_specs = [window(b, lag) for lag in (0, 1) for b in range(batch)] + [pspec(meta.shape)]

    return pl.pallas_call(
        functools.partial(_layer_kernel, mixer=mixer, init=init, nparams=len(params), cols=cols, tn=tn,
                          nbatch=batch, frames_seq=None if meta is None else hlen,
                          tail_is_padding=lp - used >= CHUNK),
        grid=(nblk + 1,),
        in_specs=source_specs + [
            pspec((1, d)),
            pl.BlockSpec((d, nin), lambda p: (0, 0), pipeline_mode=resident),
            pl.BlockSpec((nout, d), lambda p: (0, 0), pipeline_mode=resident),
            pspec((1, d)),
        ] + [pspec(q.shape) for q in params],
        out_specs=pl.BlockSpec((batch, 2 * CHUNK, d), lambda p: (0, jnp.maximum(p - 1, 0), 0)),
        out_shape=jax.ShapeDtypeStruct((batch, lp, d), F32),
        scratch_shapes=[pltpu.VMEM((batch, CHUNK, nin), F32), pltpu.VMEM((batch, CHUNK, nin), F32),
                        pltpu.VMEM((batch, CHUNK, nout), BF16), pltpu.VMEM((batch, CHUNK, nout), BF16),
                        pltpu.VMEM((batch * CHUNK, d), BF16)] + scratch,
        compiler_params=pltpu.CompilerParams(
            dimension_semantics=("arbitrary",), vmem_limit_bytes=VMEM_LIMIT_BYTES),
        name=name,
    )(*sources, g_in.reshape(1, d), w_in, w_out, g_out.reshape(1, d), *params)


def _mixer_ab_init(gn_ref, wg_ref, ba_ref, bi_ref, lam_ref, cw_ref, cb_ref,
                   s_ref, xcar_ref, hcar_ref, a_slab, b_slab):
    s_ref[...] = jnp.zeros_like(s_ref)
    xcar_ref[...] = jnp.zeros_like(xcar_ref)
    hcar_ref[...] = jnp.zeros_like(hcar_ref)


def _mixer_ab(c, fill, q_ref, k_ref, v_ref, gate_ref, bx_ref, bg_ref, gn_ref, wg_ref, ba_ref, bi_ref,
              lam_ref, cw_ref, cb_ref, y_ref, s_ref, xcar_ref, hcar_ref, a_slab, b_slab):
    nbatch, _, width = bx_ref.shape
    ngrp = width // LANES
    scan_slots = CHUNK // SUBLANES
    fill.nslots = nbatch * RET_HEADS + ngrp + scan_slots

    row = lax.broadcasted_iota(jnp.int32, (CHUNK, 1), 0)
    idx = row.astype(F32)
    pos = (c * CHUNK + row - PAD).astype(F32)
    lane = lax.broadcasted_iota(jnp.int32, (1, RET_DK), 1)
    half = RET_DK // 2
    freq = jnp.exp((lane & (half - 1)).astype(F32) * (-math.log(ROPE_BASE) / half))
    ang = pos * freq
    cosv = jnp.cos(ang)
    sinv = jnp.where(lane < half, -1.0, 1.0) * jnp.sin(ang)
    ti = lax.broadcasted_iota(jnp.int32, (CHUNK, CHUNK), 0)
    si = lax.broadcasted_iota(jnp.int32, (CHUNK, CHUNK), 1)
    dist = jnp.abs(ti - si).astype(F32)

    def ret_body(b, carry):
        for h in range(RET_HEADS):
            log_g = math.log1p(-(2.0 ** (-5.0 - h)))
            qs = slice(h * RET_DK, (h + 1) * RET_DK)
            vs = slice(h * RET_DV, (h + 1) * RET_DV)
            qh = q_ref[b, :, qs]
            kh = k_ref[b, :, qs]
            qr = (qh * cosv + pltpu.roll(qh, half, 1) * sinv) * (RET_DK ** -0.5)
            kr = kh * cosv + pltpu.roll(kh, half, 1) * sinv
            vh = v_ref[b, :, vs].astype(BF16)
            scores = _dot_nt(qr.astype(BF16), kr.astype(BF16)) * jnp.exp(dist * log_g)
            o = _dot(scores.astype(BF16), vh)
            q_dec = qr * jnp.exp((idx + 1.0) * log_g)
            o = o + _dot(q_dec.astype(BF16), s_ref[b, h].astype(BF16))
            k_dec = kr * jnp.exp((CHUNK - 1.0 - idx) * log_g)
            kv = _dot(k_dec.T.astype(BF16), vh)
            s_ref[b, h] = math.exp(CHUNK * log_g) * s_ref[b, h] + kv
            oc = o - jnp.mean(o, axis=-1, keepdims=True)
            var = jnp.mean(oc * oc, axis=-1, keepdims=True)
            gt = gate_ref[b, :, vs]
            y_ref[b, :, vs] = (oc * lax.rsqrt(var + EPS) * gn_ref[:, vs]
                               * _silu(gt)).astype(y_ref.dtype)
            fill()
        return carry

    for b in range(nbatch):
        ret_body(b, 0)

    xb = bx_ref[...]
    xe = jnp.concatenate([xcar_ref[...], xb], axis=1)
    xc = cb_ref[...] + xb * cw_ref[CONV_WIDTH - 1:CONV_WIDTH, :]
    for s in range(1, CONV_WIDTH):
        xc = xc + pltpu.roll(xe, s, 1)[:, SUBLANES:, :] * cw_ref[CONV_WIDTH - 1 - s:CONV_WIDTH - s, :]
    xcar_ref[...] = xb[:, CHUNK - SUBLANES:, :]

    xc2 = xc.reshape(nbatch * CHUNK, width)
    xcb = xc2.astype(BF16)
    valid = jnp.logical_or(c > 0, row >= PAD)
    for p in range(ngrp):
        cs = slice(p * LANES, (p + 1) * LANES)
        g2 = _dot(xcb[:, cs], wg_ref[p])
        r = _sigmoid(g2[:, :LANES] + ba_ref[:, cs])
        i = _sigmoid(g2[:, LANES:] + bi_ref[:, cs])
        lam = lam_ref[:, cs]
        softplus_neg_lam = jnp.maximum(-lam, 0.0) + jnp.log1p(jnp.exp(-jnp.abs(lam)))
        a = jnp.exp(-LRU_C * r * softplus_neg_lam)
        bb = _sqrt_nonneg(1.0 - a * a) * (i * xc2[:, cs])
        for b in range(nbatch):
            rs = slice(b * CHUNK, (b + 1) * CHUNK)
            a_slab[b, p * SLAB_PITCH:p * SLAB_PITCH + CHUNK, :] = a[rs]
            b_slab[b, p * SLAB_PITCH:p * SLAB_PITCH + CHUNK, :] = jnp.where(valid, bb[rs], 0.0)
        fill()

    def step(t, hs):
        out = []
        for b in range(nbatch):
            ts = pl.ds(t, ngrp, stride=SLAB_PITCH)
            h = a_slab[b, ts, :] * hs[b] + b_slab[b, ts, :]
            b_slab[b, ts, :] = h
            out.append(h)
        return tuple(out)

    hs = tuple(hcar_ref[b] for b in range(nbatch))
    for t in range(CHUNK):
        hs = step(t, hs)
        if t % SUBLANES == SUBLANES - 1:
            fill()
    for b in range(nbatch):
        hcar_ref[b] = hs[b]
        hfull = jnp.concatenate(
            [b_slab[b, p * SLAB_PITCH:p * SLAB_PITCH + CHUNK, :] for p in range(ngrp)], axis=1)
        y_ref[b, :, RET_HEADS * RET_DV:] = (_gelu_tanh(bg_ref[b]) * hfull).astype(y_ref.dtype)


def _layer_ab(h3, g_in, w_in, w_out, g_out, ret_gn, wg, ba, bi, lam, conv_w, conv_b, *, lp, used, meta=None):
    batch = h3.shape[0]
    qk = RET_HEADS * RET_DK
    vw = RET_HEADS * RET_DV
    lw = lam.shape[-1]
    assert lw == SUBLANES * LANES
    params = [ret_gn.reshape(1, vw), wg, ba.reshape(1, lw), bi.reshape(1, lw), lam.reshape(1, lw),
              conv_w, conv_b.reshape(1, lw)]
    scratch = [
        pltpu.VMEM((batch, RET_HEADS, RET_DK, RET_DV), F32),
        pltpu.VMEM((batch, SUBLANES, lw), F32),
        pltpu.VMEM((batch, SUBLANES, LANES), F32),
        pltpu.VMEM((batch, SUBLANES * SLAB_PITCH, LANES), F32),
        pltpu.VMEM((batch, SUBLANES * SLAB_PITCH, LANES), F32),
    ]
    return _layer(h3, g_in, w_in, w_out, g_out, params, mixer=_mixer_ab, init=_mixer_ab_init,
                  cols=(qk, qk, vw, vw, lw, lw), scratch=scratch, name="layer_ab", tn=2 * MXU_COLS,
                  lp=lp, used=used, meta=meta)


_HG_LEVELS = (8, 16, 32)


def _mixer_cd_init(arl_ref, aim_ref, ldt_ref, arl_t_ref, aim_t_ref, ldt_t_ref,
                   bre_ref, bim_ref, cre_ref, cim_ref, d_ref, gw_ref, gb_ref, gn_ref, lbl_ref,
                   bm_ref, ab_ref, hcar_ref, bu_slab, sel_ref, st_ref):
    def disc(ldt, arl, aim):
        dt = jnp.exp(ldt)
        a_re = -jnp.exp(arl)
        mag = jnp.exp(dt * a_re)
        return a_re, aim, mag * jnp.cos(dt * aim), mag * jnp.sin(dt * aim)

    _, _, t_re, t_im = disc(ldt_t_ref[...], arl_t_ref[...], aim_t_ref[...])
    ab_ref[0] = t_re
    ab_ref[1] = t_im
    a_re, a_im, ab_re, ab_im = disc(ldt_ref[...], arl_ref[...], aim_ref[...])
    den = a_re * a_re + a_im * a_im
    z_re = ((ab_re - 1.0) * a_re + ab_im * a_im) / den
    z_im = (ab_im * a_re - (ab_re - 1.0) * a_im) / den
    for jb in range(bre_ref.shape[0]):
        ls = slice(jb * S5_LANES, (jb + 1) * S5_LANES)
        zr = z_re[:, ls]
        zi = z_im[:, ls]
        bb_re = zr * bre_ref[jb] - zi * bim_ref[jb]
        bb_im = zr * bim_ref[jb] + zi * bre_ref[jb]
        bm_ref[jb] = jnp.concatenate([bb_re, bb_im], axis=1).astype(BF16)
    ri = lax.broadcasted_iota(jnp.int32, (2 * HG_DK, 2 * CHUNK), 0)
    ci = lax.broadcasted_iota(jnp.int32, (2 * HG_DK, 2 * CHUNK), 1)
    same_head = (ri >= HG_DK) == (ci >= CHUNK)
    for s in range(SUBLANES):
        sel_ref[s] = jnp.logical_and(same_head, (ci & (SUBLANES - 1)) == s).astype(BF16)
    hcar_ref[...] = jnp.zeros_like(hcar_ref)
    st_ref[...] = jnp.zeros_like(st_ref)


def _mixer_cd(c, fill, u_ref, q_ref, f_ref, i_ref, g_ref,
              arl_ref, aim_ref, ldt_ref, arl_t_ref, aim_t_ref, ldt_t_ref,
              bre_ref, bim_ref, cre_ref, cim_ref, d_ref, gw_ref, gb_ref, gn_ref, lbl_ref, y_ref,
              bm_ref, ab_ref, hcar_ref, bu_slab, sel_ref, st_ref, *, layer):
    del c
    nbatch, _, s5w = u_ref.shape
    ncol = bre_ref.shape[0]
    cin = bre_ref.shape[1]
    nslab = ab_ref.shape[1]
    nhalf = nslab // SUBLANES
    gpc = S5_LANES // LANES
    hw = HG_HEADS * HG_DK
    pair = 2 * HG_DK
    scan_slots = CHUNK // SUBLANES
    fill.nslots = 2 * ncol + scan_slots + 1 + SUBLANES // 2 + nbatch * len(_HG_LEVELS)

    u2 = u_ref[...].reshape(nbatch * CHUNK, s5w)
    ub = u2.astype(BF16)
    for jb in range(ncol):
        bu = _dot(ub[:, jb * cin:(jb + 1) * cin], bm_ref[jb])
        for part in range(2):
            for gi in range(gpc):
                slab = part * nslab + jb * gpc + gi
                col = part * S5_LANES + gi * LANES
                for b in range(nbatch):
                    bu_slab[b, slab * SLAB_PITCH:slab * SLAB_PITCH + CHUNK, :] = (
                        bu[b * CHUNK:(b + 1) * CHUNK, col:col + LANES])
        fill()

    a_re = [ab_ref[0, hf * SUBLANES:(hf + 1) * SUBLANES, :] for hf in range(nhalf)]
    a_im = [ab_ref[1, hf * SUBLANES:(hf + 1) * SUBLANES, :] for hf in range(nhalf)]

    def step(t, hs):
        out = []
        for b in range(nbatch):
            for hf in range(nhalf):
                h_re, h_im = hs[2 * (b * nhalf + hf)], hs[2 * (b * nhalf + hf) + 1]
                ts_re = pl.ds(hf * SUBLANES * SLAB_PITCH + t, SUBLANES, stride=SLAB_PITCH)
                ts_im = pl.ds((nslab + hf * SUBLANES) * SLAB_PITCH + t, SUBLANES, stride=SLAB_PITCH)
                n_re = a_re[hf] * h_re - a_im[hf] * h_im + bu_slab[b, ts_re, :]
                n_im = a_re[hf] * h_im + a_im[hf] * h_re + bu_slab[b, ts_im, :]
                bu_slab[b, ts_re, :] = n_re
                bu_slab[b, ts_im, :] = n_im
                out += [n_re, n_im]
        return tuple(out)

    init = []
    for b in range(nbatch):
        for hf in range(nhalf):
            init += [hcar_ref[b, 0, hf], hcar_ref[b, 1, hf]]
    hs = tuple(init)
    for t in range(CHUNK):
        hs = step(t, hs)
        if t % SUBLANES == SUBLANES - 1:
            fill()
    for b in range(nbatch):
        for hf in range(nhalf):
            hcar_ref[b, 0, hf] = hs[2 * (b * nhalf + hf)]
            hcar_ref[b, 1, hf] = hs[2 * (b * nhalf + hf) + 1]

    def states(part, jb):
        return jnp.concatenate(
            [jnp.concatenate(
                [bu_slab[b, (part * nslab + jb * gpc + gi) * SLAB_PITCH:
                         (part * nslab + jb * gpc + gi) * SLAB_PITCH + CHUNK, :] for gi in range(gpc)], axis=1)
             for b in range(nbatch)], axis=0).astype(BF16)

    ys = []
    for jb in range(ncol):
        cs = slice(jb * cin, (jb + 1) * cin)
        y = _dot(states(0, jb), cre_ref[jb]) - _dot(states(1, jb), cim_ref[jb]) + d_ref[:, cs] * u2[:, cs]
        ys.append(_gelu_tanh(y))
        fill()
    yg = jnp.concatenate(ys, axis=1)
    yc = yg * _sigmoid(_dot(yg.astype(BF16), gw_ref[...]) + gb_ref[...])
    y_ref[:, :, :s5w] = yc.reshape(nbatch, CHUNK, s5w).astype(y_ref.dtype)
    fill()

    logits = lbl_ref[...]
    pexp = jnp.exp(logits - jnp.max(logits, axis=0, keepdims=True))
    psm = pexp / jnp.sum(pexp, axis=0, keepdims=True)
    lb = jnp.zeros_like(psm[0:1, :])
    for l in range(layer):
        lb = lb + psm[l:l + 1, :]

    ti = lax.broadcasted_iota(jnp.int32, (CHUNK, CHUNK), 0)
    si = lax.broadcasted_iota(jnp.int32, (CHUNK, CHUNK), 1)
    tril = (ti >= si).astype(BF16)
    tp = lax.broadcasted_iota(jnp.int32, (CHUNK, 2 * CHUNK), 0)
    sp = lax.broadcasted_iota(jnp.int32, (CHUNK, 2 * CHUNK), 1) & (CHUNK - 1)
    diag_mask = jnp.logical_and((tp >> 3) == (sp >> 3), sp <= tp)
    nvr = CHUNK // SUBLANES

    def both_heads(x, p):
        xa = x[:, p * pair:p * pair + HG_DK]
        xb_ = x[:, p * pair + HG_DK:(p + 1) * pair]
        zero = jnp.zeros_like(xa)
        return jnp.concatenate([jnp.concatenate([xa, zero], axis=1),
                                jnp.concatenate([zero, xb_], axis=1)], axis=0)

    def gates(b):
        f = lb + (1.0 - lb) * jax.nn.sigmoid(f_ref[b])
        logf = jnp.log(f)
        kk = 1.0 - f
        lf_hi = logf.astype(BF16)
        lf_lo = (logf - lf_hi.astype(F32)).astype(BF16)
        cum = _dot(tril, lf_hi) + _dot(tril, lf_lo)
        return kk, cum

    gated = [gates(b) for b in range(nbatch)]

    q3 = [q_ref[b].reshape(nvr, SUBLANES, hw) for b in range(nbatch)]
    c3 = [cum.reshape(nvr, SUBLANES, hw) for _, cum in gated]
    e3 = [(jnp.log(jnp.maximum(kk, 0.0)) - cum).reshape(nvr, SUBLANES, hw) for kk, cum in gated]
    dsum = [jnp.zeros((nbatch * CHUNK, 2 * CHUNK), F32) for _ in range(HG_HEADS // 2)]
    for s in range(SUBLANES):
        wb = jnp.concatenate(
            [(q3[b] * jnp.exp(jnp.minimum(c3[b] + e3[b][:, s:s + 1, :], 0.0))).reshape(CHUNK, hw)
             for b in range(nbatch)], axis=0).astype(BF16)
        for p in range(HG_HEADS // 2):
            dsum[p] = dsum[p] + _dot(wb[:, p * pair:(p + 1) * pair], sel_ref[s])
        if s % 2 == 1:
            fill()

    def hg_body(b, carry):
        kk, cum = gated[b]
        total = cum[CHUNK - 1:CHUNK, :]
        q = q_ref[b]
        iv = i_ref[b]
        ivb = iv.astype(BF16)
        q_in = (q * jnp.exp(cum)).astype(BF16)
        k_dec = kk * jnp.exp(total - cum)
        dec = jnp.exp(total)

        att = [jnp.zeros((CHUNK, 2 * CHUNK), F32) for _ in range(HG_HEADS // 2)]
        ends = [cum[v * SUBLANES + SUBLANES - 1:(v + 1) * SUBLANES, :] for v in range(nvr)]
        zeros8 = jnp.zeros((SUBLANES, hw), F32)
        for n in _HG_LEVELS:
            per = n // SUBLANES
            qparts, kparts = [], []
            for v in range(nvr):
                blk = v // per
                vs = slice(v * SUBLANES, (v + 1) * SUBLANES)
                if blk % 2 == 1:
                    qparts.append(q[vs] * jnp.exp(cum[vs] - ends[blk * per - 1]))
                    kparts.append(zeros8)
                else:
                    qparts.append(zeros8)
                    kparts.append(kk[vs] * jnp.exp(ends[blk * per + per - 1] - cum[vs]))
            q_t = jnp.concatenate(qparts, axis=0).astype(BF16)
            k_t = jnp.concatenate(kparts, axis=0).astype(BF16)
            shift = n.bit_length()
            for p in range(HG_HEADS // 2):
                a_n = _dot_nt(q_t[:, p * pair:(p + 1) * pair], both_heads(k_t, p))
                if 2 * n < CHUNK:
                    a_n = jnp.where((tp >> shift) == (sp >> shift), a_n, 0.0)
                att[p] = att[p] + a_n
            fill()
        for p in range(HG_HEADS // 2):
            diag = dsum[p][b * CHUNK:(b + 1) * CHUNK, :]
            a_all = (att[p] + jnp.where(diag_mask, diag, 0.0)).astype(BF16)
            o_pair = _dot(a_all, both_heads(ivb, p))
            for hh in range(2):
                h = 2 * p + hh
                hs_ = slice(h * HG_DK, (h + 1) * HG_DK)
                st = st_ref[b, h]
                oh = o_pair[:, hh * HG_DK:(hh + 1) * HG_DK] + _dot_nt(q_in[:, hs_], st.astype(BF16))
                st_ref[b, h] = dec[:, hs_] * st + _dot(iv[:, hs_].T.astype(BF16), k_dec[:, hs_].astype(BF16))
                ms = jnp.mean(oh * oh, axis=-1, keepdims=True)
                gt = g_ref[b, :, hs_]
                y_ref[b, :, s5w + h * HG_DK:s5w + (h + 1) * HG_DK] = (
                    oh * lax.rsqrt(ms + EPS) * gn_ref[:, hs_] * _silu(gt)).astype(y_ref.dtype)
        return carry

    for b in range(nbatch):
        hg_body(b, 0)


def _block_diag(t):
    n, g, r, c = t.shape
    on_diagonal = jnp.eye(g, dtype=bool)[None, :, None, :, None]
    return jnp.where(on_diagonal, t[:, :, :, None, :], 0).reshape(n, g * r, g * c)


def _layer_cd(h3, g_in, w_in, w_out, g_out, a_re_log, a_im, b_re, b_im, c_re, c_im, d, log_dt, glu_w, glu_b,
              hg_gn, lb_logits, *, layer, used):
    batch = h3.shape[0]
    groups, state = a_re_log.shape
    s5w = groups * S5_GROUP
    nstate = groups * state
    gpc = S5_LANES // state
    ncol = groups // gpc
    cin = gpc * S5_GROUP
    hw = HG_HEADS * HG_DK
    nslab = nstate // LANES
    assert s5w == hw and nslab % SUBLANES == 0

    bre = _block_diag(jnp.transpose(b_re.reshape(ncol, gpc, state, S5_GROUP), (0, 1, 3, 2)))
    bim = _block_diag(jnp.transpose(b_im.reshape(ncol, gpc, state, S5_GROUP), (0, 1, 3, 2)))
    cre = _block_diag(jnp.transpose(c_re.reshape(ncol, gpc, S5_GROUP, state), (0, 1, 3, 2))).astype(BF16)
    cim = _block_diag(jnp.transpose(c_im.reshape(ncol, gpc, S5_GROUP, state), (0, 1, 3, 2))).astype(BF16)
    ldt = jnp.repeat(log_dt, state)
    params = [a_re_log.reshape(1, nstate), a_im.reshape(1, nstate), ldt.reshape(1, nstate),
              a_re_log.reshape(nslab, LANES), a_im.reshape(nslab, LANES), ldt.reshape(nslab, LANES),
              bre, bim, cre, cim, d.reshape(1, s5w), glu_w.astype(BF16), glu_b.reshape(1, s5w),
              hg_gn.reshape(1, hw), lb_logits]
    scratch = [
        pltpu.VMEM((ncol, cin, 2 * S5_LANES), BF16),
        pltpu.VMEM((2, nslab, LANES), F32),
        pltpu.VMEM((batch, 2, nslab // SUBLANES, SUBLANES, LANES), F32),
        pltpu.VMEM((batch, 2 * nslab * SLAB_PITCH, LANES), F32),
        pltpu.VMEM((SUBLANES, 2 * HG_DK, 2 * CHUNK), BF16),
        pltpu.VMEM((batch, HG_HEADS, HG_DK, HG_DK), F32),
    ]
    return _layer(h3, g_in, w_in, w_out, g_out, params, mixer=functools.partial(_mixer_cd, layer=layer),
                  init=_mixer_cd_init, cols=(s5w,) * 5, scratch=scratch, name="layer_cd", tn=MXU_COLS,
                  lp=h3.shape[1], used=used)


def _pack_lru_gates(wa, wi):
    nblk, bd, _ = wa.shape
    per = LANES // bd
    wa_bd = _block_diag(wa.reshape(nblk // per, per, bd, bd))
    wi_bd = _block_diag(wi.reshape(nblk // per, per, bd, bd))
    return jnp.concatenate([wa_bd, wi_bd], axis=2).astype(BF16)


def kernel(x, meta, w_in_ab, w_out_ab, ret_gn, rg_wa, rg_ba, rg_wi, rg_bi, rg_lam, rg_conv_w, rg_conv_b,
           w_in_cd, w_out_cd, s5_a_re_log, s5_a_im, s5_b_re, s5_b_im, s5_c_re, s5_c_im, s5_d, s5_log_dt,
           s5_glu_w, s5_glu_b, hg_gn, hg_lb_logits, norm_g, mlp_w1, mlp_w2):
    batch, seq, d = x.shape
    depth = norm_g.shape[0]
    used = PAD + N_META + seq
    assert used % CHUNK == 0
    lp = -(-used // (2 * CHUNK)) * (2 * CHUNK)
    m = batch * lp

    h = x
    for l in range(depth):
        jdx = l // 2
        if l % 2 == 0:
            h = _layer_ab(h, norm_g[l, 0], w_in_ab[jdx].astype(BF16), w_out_ab[jdx].astype(BF16), norm_g[l, 1],
                          ret_gn[jdx], _pack_lru_gates(rg_wa[jdx], rg_wi[jdx]), rg_ba[jdx], rg_bi[jdx],
                          rg_lam[jdx], rg_conv_w[jdx], rg_conv_b[jdx], lp=lp, used=used,
                          meta=meta.astype(x.dtype) if l == 0 else None)
        else:
            h = _layer_cd(h, norm_g[l, 0], w_in_cd[jdx].astype(BF16), w_out_cd[jdx].astype(BF16), norm_g[l, 1],
                          s5_a_re_log[jdx], s5_a_im[jdx], s5_b_re[jdx], s5_b_im[jdx], s5_c_re[jdx], s5_c_im[jdx],
                          s5_d[jdx], s5_log_dt[jdx], s5_glu_w[jdx], s5_glu_b[jdx], hg_gn[jdx], hg_lb_logits,
                          layer=l, used=used)
        if l + 1 < depth:
            h = _mlp(h.reshape(m, d), norm_g[l, 2], mlp_w1, mlp_w2, norm_g[l, 3], layer=l,
                     tm=_largest_divisor(lp, MLP_ROWS, MLP_ROW_ALIGN), tn=1024,
                     rows_per_batch=lp).reshape(batch, lp, d)
        else:
            h = _mlp_frames(h, norm_g[l, 2], mlp_w1, mlp_w2, norm_g[l, 3], layer=l, first=PAD + N_META,
                            count=seq, tm=_largest_divisor(seq, MLP_ROWS, MLP_ROW_ALIGN), tn=1024)
    return h
```

```python
import functools
import math

import jax
import jax.numpy as jnp
from jax import lax
from jax.experimental import pallas as pl
from jax.experimental.pallas import tpu as pltpu

F32 = jnp.float32
BF16 = jnp.bfloat16

CHUNK = 64
N_META = 16
PAD = CHUNK - N_META
EPS = 1e-6

RET_HEADS = 4
RET_DK = 128
RET_DV = 256
ROPE_BASE = 10000.0
LRU_C = 8.0
CONV_WIDTH = 4
S5_GROUP = 16
S5_LANES = 512
HG_HEADS = 4
HG_DK = 128
LANES = 128
SUBLANES = 8
MXU_COLS = 256
SLAB_PITCH = CHUNK + SUBLANES
VMEM_LIMIT_BYTES = 56 * 1024 * 1024
MLP_ROWS = 704
MLP_ROW_ALIGN = 32


def _largest_divisor(n, cap, multiple_of=1):
    return max(d for d in range(multiple_of, cap + 1, multiple_of) if n % d == 0)


def _gelu_tanh(x):
    return 0.5 * x * (1.0 + jnp.tanh(0.7978845608028654 * (x + 0.044715 * x * x * x)))


def _sigmoid(x):
    return 0.5 * jnp.tanh(0.5 * x) + 0.5


def _silu(x):
    hx = 0.5 * x
    return hx + hx * jnp.tanh(hx)


def _sqrt_nonneg(x):
    return jnp.where(x > 0.0, x * lax.rsqrt(x), 0.0)


def _dot(a, b):
    return jnp.dot(a, b, preferred_element_type=F32)


def _dot_nt(a, b):
    return lax.dot_general(a, b, (((1,), (1,)), ((), ())), preferred_element_type=F32)


def _keep_rows(out, tiles_per_batch):
    first_tile = lax.rem(pl.program_id(0), tiles_per_batch) == 0
    row = lax.broadcasted_iota(jnp.int32, (out.shape[0], 1), 0)
    keep = jnp.logical_or(row >= PAD, jnp.logical_not(first_tile))
    return jnp.where(keep, out, 0.0)


def _mlp_rows(x, g_in_ref, w1_ref, w2_ref, g_out_ref, tn):
    nj = w1_ref.shape[1] // tn
    half = x.shape[0] // 2
    xs = (x[:half], x[half:])
    hn, acc, out = [None, None], [None, None], [None, None]

    def prologue(i):
        ms = jnp.mean(xs[i] * xs[i], axis=-1, keepdims=True)
        hn[i] = (xs[i] * lax.rsqrt(ms + EPS) * g_in_ref[...]).astype(BF16)

    def matmuls(i, j):
        a = jnp.maximum(_dot(hn[i], w1_ref[:, j * tn:(j + 1) * tn].astype(BF16)), 0.0)
        part = _dot((a * a).astype(BF16), w2_ref[j * tn:(j + 1) * tn, :].astype(BF16))
        acc[i] = part if j == 0 else acc[i] + part

    def epilogue(i):
        ms = jnp.mean(acc[i] * acc[i], axis=-1, keepdims=True)
        out[i] = xs[i] + acc[i] * lax.rsqrt(ms + EPS) * g_out_ref[...]

    prologue(0)
    matmuls(0, 0)
    prologue(1)
    for j in range(1, nj):
        matmuls(0, j)
        matmuls(1, j - 1)
    epilogue(0)
    matmuls(1, nj - 1)
    epilogue(1)
    return jnp.concatenate(out, axis=0)


def _mlp_kernel(h_ref, g_in_ref, w1_ref, w2_ref, g_out_ref, o_ref, *, tiles_per_batch, tn):
    o_ref[...] = _keep_rows(_mlp_rows(h_ref[...], g_in_ref, w1_ref, w2_ref, g_out_ref, tn), tiles_per_batch)


def _mlp_frames_kernel(h_ref, g_in_ref, w1_ref, w2_ref, g_out_ref, o_ref, *, tn):
    o_ref[...] = _mlp_rows(h_ref[...], g_in_ref, w1_ref, w2_ref, g_out_ref, tn)


def _mlp_frames(h3, g_in, w1, w2, g_out, *, layer, first, count, tm, tn):
    batch, lp, d = h3.shape
    f = w1.shape[2]
    resident = pl.Buffered(1)
    return pl.pallas_call(
        functools.partial(_mlp_frames_kernel, tn=tn),
        grid=(batch, count // tm),
        in_specs=[
            pl.BlockSpec((pl.Element(tm), pl.Element(d)),
                         lambda b, i: (pl.multiple_of(b * lp + first + i * tm, SUBLANES), 0)),
            pl.BlockSpec((1, d), lambda b, i: (0, 0)),
            pl.BlockSpec((None, d, f), lambda b, i: (layer, 0, 0), pipeline_mode=resident),
            pl.BlockSpec((None, f, d), lambda b, i: (layer, 0, 0), pipeline_mode=resident),
            pl.BlockSpec((1, d), lambda b, i: (0, 0)),
        ],
        out_specs=pl.BlockSpec((None, tm, d), lambda b, i: (b, i, 0)),
        out_shape=jax.ShapeDtypeStruct((batch, count, d), F32),
        compiler_params=pltpu.CompilerParams(
            dimension_semantics=("arbitrary", "arbitrary"), vmem_limit_bytes=VMEM_LIMIT_BYTES),
        name="mlp_frames",
    )(h3.reshape(batch * lp, d), g_in.reshape(1, d), w1, w2, g_out.reshape(1, d))


def _mlp(h, g_in, w1, w2, g_out, *, layer, tm, tn, rows_per_batch):
    m, d = h.shape
    f = w1.shape[2]
    resident = pl.Buffered(1)
    return pl.pallas_call(
        functools.partial(_mlp_kernel, tiles_per_batch=rows_per_batch // tm, tn=tn),
        grid=(m // tm,),
        in_specs=[
            pl.BlockSpec((tm, d), lambda i: (i, 0)),
            pl.BlockSpec((1, d), lambda i: (0, 0)),
            pl.BlockSpec((None, d, f), lambda i: (layer, 0, 0), pipeline_mode=resident),
            pl.BlockSpec((None, f, d), lambda i: (layer, 0, 0), pipeline_mode=resident),
            pl.BlockSpec((1, d), lambda i: (0, 0)),
        ],
        out_specs=pl.BlockSpec((tm, d), lambda i: (i, 0)),
        out_shape=jax.ShapeDtypeStruct((m, d), F32),
        compiler_params=pltpu.CompilerParams(
            dimension_semantics=("arbitrary",), vmem_limit_bytes=VMEM_LIMIT_BYTES),
        name="mlp",
    )(h, g_in.reshape(1, d), w1, w2, g_out.reshape(1, d))


def _window_start(p, seq):
    return jnp.clip(2 * CHUNK * p - CHUNK, 0, seq - 2 * CHUNK)


class _Filler:
    def __init__(self, pieces):
        self._pieces = list(pieces)
        self._total = len(self._pieces)
        self._done = 0
        self._slot = 0
        self.nslots = 1

    def __call__(self):
        self._slot += 1
        target = min(self._total, -(-self._total * self._slot // self.nslots))
        while self._done < target:
            self._pieces[self._done]()
            self._done += 1

    def drain(self):
        while self._done < self._total:
            self._pieces[self._done]()
            self._done += 1


def _layer_kernel(*refs, mixer, init, nparams, cols, tn, nbatch, frames_seq, tail_is_padding):
    it = iter(refs)
    if frames_seq:
        xa_refs = [next(it) for _ in range(nbatch)]
        xc_refs = [next(it) for _ in range(nbatch)]
        meta_ref = next(it)
    else:
        ha_ref, hc_ref = next(it), next(it)
    g_in_ref, win_ref, wout_ref, g_out_ref = next(it), next(it), next(it), next(it)
    params = [next(it) for _ in range(nparams)]
    o_ref = next(it)
    z_refs = (next(it), next(it))
    y_refs = (next(it), next(it))
    hn_ref = next(it)
    scratch = list(it)
    d = o_ref.shape[2]
    p = pl.program_id(0)
    rows = nbatch * CHUNK

    if frames_seq:
        meta = meta_ref[...]
        meta_chunk = jnp.concatenate([jnp.zeros((CHUNK - meta.shape[0], d), F32), meta], axis=0)
        meta_rows = jnp.concatenate([meta_chunk] * nbatch, axis=0)

        def frame_rows(window_refs, window_start, chunk):
            off = jnp.clip((chunk - 1) * CHUNK - window_start, 0, CHUNK)
            off = pl.multiple_of(off, CHUNK)
            x = jnp.concatenate([r[pl.ds(off, CHUNK), :] for r in window_refs], axis=0)
            return jnp.where(chunk == 0, meta_rows, x)

    def load_in(e, s):
        if frames_seq:
            return frame_rows(xa_refs, _window_start(p, frames_seq), s)
        return ha_ref[:, e * CHUNK:(e + 1) * CHUNK, :].reshape(rows, d)

    def load_res(e, s):
        if frames_seq:
            return frame_rows(xc_refs, _window_start(p - 1, frames_seq), s - 2)
        return hc_ref[:, e * CHUNK:(e + 1) * CHUNK, :].reshape(rows, d)

    row = lax.broadcasted_iota(jnp.int32, (1, CHUNK, 1), 1)

    def out_pieces(e):
        s = 2 * p + e
        m_parts = []

        def piece(j):
            cs = slice(j * tn, (j + 1) * tn)
            y_in = y_refs[e][...].reshape(rows, y_refs[e].shape[2])
            m_parts.append(_dot(y_in, wout_ref[:, cs]))

        def finish():
            m = jnp.concatenate(m_parts, axis=1)
            ms = jnp.mean(m * m, axis=-1, keepdims=True)
            out = (load_res(e, s) + m * lax.rsqrt(ms + EPS) * g_out_ref[...]).reshape(nbatch, CHUNK, d)
            o_ref[:, e * CHUNK:(e + 1) * CHUNK, :] = jnp.where(jnp.logical_or(row >= PAD, s != 2), out, 0.0)

        return [functools.partial(piece, j) for j in range(d // tn)] + [finish]

    def in_pieces(e):
        def start():
            x = load_in(e, 2 * p + e)
            ms = jnp.mean(x * x, axis=-1, keepdims=True)
            hn_ref[...] = (x * lax.rsqrt(ms + EPS) * g_in_ref[...]).astype(BF16)

        def piece(j):
            cs = slice(j * tn, (j + 1) * tn)
            z_refs[e][:, :, cs] = _dot(hn_ref[...], win_ref[:, cs]).reshape(nbatch, CHUNK, tn)

        return [start] + [functools.partial(piece, j) for j in range(win_ref.shape[1] // tn)]

    def tick(e, with_out=True):
        fill = _Filler((out_pieces(e) if with_out else []) + in_pieces(e))
        views = []
        off = 0
        for width in cols:
            views.append(z_refs[1 - e].at[:, :, off:off + width])
            off += width
        mixer(2 * p + e - 1, fill, *views, *params, y_refs[1 - e], *scratch)
        fill.drain()

    last = pl.num_programs(0) - 1

    @pl.when(p == 0)
    def _():
        init(*params, *scratch)
        for piece in in_pieces(0):
            piece()
        tick(1, with_out=False)

    @pl.when(jnp.logical_and(p > 0, p < last) if tail_is_padding else p > 0)
    def _():
        tick(0)
        tick(1)

    if tail_is_padding:
        @pl.when(p == last)
        def _():
            for piece in out_pieces(0):
                piece()
            o_ref[:, CHUNK:, :] = jnp.zeros((nbatch, CHUNK, d), F32)


def _layer(h, g_in, w_in, w_out, g_out, params, *, mixer, init, cols, scratch, name, tn, lp, used, meta=None):
    batch, hlen, d = h.shape
    nin = w_in.shape[1]
    nout = w_out.shape[0]
    assert nin % tn == 0 and d % tn == 0
    nblk = lp // (2 * CHUNK)
    resident = pl.Buffered(1)

    def pspec(shape):
        return pl.BlockSpec(shape, lambda p: (0,) * len(shape))

    if meta is None:
        sources = [h, h]
        source_specs = [pl.BlockSpec((batch, 2 * CHUNK, d), lambda p: (0, jnp.minimum(p, nblk - 1), 0)),
                        pl.BlockSpec((batch, 2 * CHUNK, d), lambda p: (0, jnp.maximum(p - 1, 0), 0))]
    else:
        def window(b, lag):
            return pl.BlockSpec(
                (pl.Element(2 * CHUNK), pl.Element(d)),
                lambda p: (pl.multiple_of(b * hlen + _window_start(p - lag, hlen), CHUNK), 0))

        frames2 = h.reshape(batch * hlen, d)
        sources = [frames2] * (2 * batch) + [meta]
        source_specs = [window(b, lag) for lag in (0, 1) for b in range(batch)] + [pspec(meta.shape)]

    return pl.pallas_call(
        functools.partial(_layer_kernel, mixer=mixer, init=init, nparams=len(params), cols=cols, tn=tn,
                          nbatch=batch, frames_seq=None if meta is None else hlen,
                          tail_is_padding=lp - used >= CHUNK),
        grid=(nblk + 1,),
        in_specs=source_specs + [
            pspec((1, d)),
            pl.BlockSpec((d, nin), lambda p: (0, 0), pipeline_mode=resident),
            pl.BlockSpec((nout, d), lambda p: (0, 0), pipeline_mode=resident),
            pspec((1, d)),
        ] + [pspec(q.shape) for q in params],
        out_specs=pl.BlockSpec((batch, 2 * CHUNK, d), lambda p: (0, jnp.maximum(p - 1, 0), 0)),
        out_shape=jax.ShapeDtypeStruct((batch, lp, d), F32),
        scratch_shapes=[pltpu.VMEM((batch, CHUNK, nin), F32), pltpu.VMEM((batch, CHUNK, nin), F32),
                        pltpu.VMEM((batch, CHUNK, nout), BF16), pltpu.VMEM((batch, CHUNK, nout), BF16),
                        pltpu.VMEM((batch * CHUNK, d), BF16)] + scratch,
        compiler_params=pltpu.CompilerParams(
            dimension_semantics=("arbitrary",), vmem_limit_bytes=VMEM_LIMIT_BYTES),
        name=name,
    )(*sources, g_in.reshape(1, d), w_in, w_out, g_out.reshape(1, d), *params)


def _mixer_ab_init(gn_ref, wg_ref, ba_ref, bi_ref, lam_ref, cw_ref, cb_ref,
                   s_ref, xcar_ref, hcar_ref, a_slab, b_slab):
    s_ref[...] = jnp.zeros_like(s_ref)
    xcar_ref[...] = jnp.zeros_like(xcar_ref)
    hcar_ref[...] = jnp.zeros_like(hcar_ref)


def _mixer_ab(c, fill, q_ref, k_ref, v_ref, gate_ref, bx_ref, bg_ref, gn_ref, wg_ref, ba_ref, bi_ref,
              lam_ref, cw_ref, cb_ref, y_ref, s_ref, xcar_ref, hcar_ref, a_slab, b_slab):
    nbatch, _, width = bx_ref.shape
    ngrp = width // LANES
    scan_slots = CHUNK // SUBLANES
    fill.nslots = nbatch * RET_HEADS + ngrp + scan_slots

    row = lax.broadcasted_iota(jnp.int32, (CHUNK, 1), 0)
    idx = row.astype(F32)
    pos = (c * CHUNK + row - PAD).astype(F32)
    lane = lax.broadcasted_iota(jnp.int32, (1, RET_DK), 1)
    half = RET_DK // 2
    freq = jnp.exp((lane & (half - 1)).astype(F32) * (-math.log(ROPE_BASE) / half))
    ang = pos * freq
    cosv = jnp.cos(ang)
    sinv = jnp.where(lane < half, -1.0, 1.0) * jnp.sin(ang)
    ti = lax.broadcasted_iota(jnp.int32, (CHUNK, CHUNK), 0)
    si = lax.broadcasted_iota(jnp.int32, (CHUNK, CHUNK), 1)
    dist = jnp.abs(ti - si).astype(F32)

    def ret_body(b, carry):
        for h in range(RET_HEADS):
            log_g = math.log1p(-(2.0 ** (-5.0 - h)))
            qs = slice(h * RET_DK, (h + 1) * RET_DK)
            vs = slice(h * RET_DV, (h + 1) * RET_DV)
            qh = q_ref[b, :, qs]
            kh = k_ref[b, :, qs]
            qr = (qh * cosv + pltpu.roll(qh, half, 1) * sinv) * (RET_DK ** -0.5)
            kr = kh * cosv + pltpu.roll(kh, half, 1) * sinv
            vh = v_ref[b, :, vs].astype(BF16)
            scores = _dot_nt(qr.astype(BF16), kr.astype(BF16)) * jnp.exp(dist * log_g)
            o = _dot(scores.astype(BF16), vh)
            q_dec = qr * jnp.exp((idx + 1.0) * log_g)
            o = o + _dot(q_dec.astype(BF16), s_ref[b, h].astype(BF16))
            k_dec = kr * jnp.exp((CHUNK - 1.0 - idx) * log_g)
            kv = _dot(k_dec.T.astype(BF16), vh)
            s_ref[b, h] = math.exp(CHUNK * log_g) * s_ref[b, h] + kv
            oc = o - jnp.mean(o, axis=-1, keepdims=True)
            var = jnp.mean(oc * oc, axis=-1, keepdims=True)
            gt = gate_ref[b, :, vs]
            y_ref[b, :, vs] = (oc * lax.rsqrt(var + EPS) * gn_ref[:, vs]
                               * _silu(gt)).astype(y_ref.dtype)
            fill()
        return carry

    for b in range(nbatch):
        ret_body(b, 0)

    xb = bx_ref[...]
    xe = jnp.concatenate([xcar_ref[...], xb], axis=1)
    xc = cb_ref[...] + xb * cw_ref[CONV_WIDTH - 1:CONV_WIDTH, :]
    for s in range(1, CONV_WIDTH):
        xc = xc + pltpu.roll(xe, s, 1)[:, SUBLANES:, :] * cw_ref[CONV_WIDTH - 1 - s:CONV_WIDTH - s, :]
    xcar_ref[...] = xb[:, CHUNK - SUBLANES:, :]

    xc2 = xc.reshape(nbatch * CHUNK, width)
    xcb = xc2.astype(BF16)
    valid = jnp.logical_or(c > 0, row >= PAD)
    for p in range(ngrp):
        cs = slice(p * LANES, (p + 1) * LANES)
        g2 = _dot(xcb[:, cs], wg_ref[p])
        r = _sigmoid(g2[:, :LANES] + ba_ref[:, cs])
        i = _sigmoid(g2[:, LANES:] + bi_ref[:, cs])
        lam = lam_ref[:, cs]
        softplus_neg_lam = jnp.maximum(-lam, 0.0) + jnp.log1p(jnp.exp(-jnp.abs(lam)))
        a = jnp.exp(-LRU_C * r * softplus_neg_lam)
        bb = _sqrt_nonneg(1.0 - a * a) * (i * xc2[:, cs])
        for b in range(nbatch):
            rs = slice(b * CHUNK, (b + 1) * CHUNK)
            a_slab[b, p * SLAB_PITCH:p * SLAB_PITCH + CHUNK, :] = a[rs]
            b_slab[b, p * SLAB_PITCH:p * SLAB_PITCH + CHUNK, :] = jnp.where(valid, bb[rs], 0.0)
        fill()

    def step(t, hs):
        out = []
        for b in range(nbatch):
            ts = pl.ds(t, ngrp, stride=SLAB_PITCH)
            h = a_slab[b, ts, :] * hs[b] + b_slab[b, ts, :]
            b_slab[b, ts, :] = h
            out.append(h)
        return tuple(out)

    hs = tuple(hcar_ref[b] for b in range(nbatch))
    for t in range(CHUNK):
        hs = step(t, hs)
        if t % SUBLANES == SUBLANES - 1:
            fill()
    for b in range(nbatch):
        hcar_ref[b] = hs[b]
        hfull = jnp.concatenate(
            [b_slab[b, p * SLAB_PITCH:p * SLAB_PITCH + CHUNK, :] for p in range(ngrp)], axis=1)
        y_ref[b, :, RET_HEADS * RET_DV:] = (_gelu_tanh(bg_ref[b]) * hfull).astype(y_ref.dtype)


def _layer_ab(h3, g_in, w_in, w_out, g_out, ret_gn, wg, ba, bi, lam, conv_w, conv_b, *, lp, used, meta=None):
    batch = h3.shape[0]
    qk = RET_HEADS * RET_DK
    vw = RET_HEADS * RET_DV
    lw = lam.shape[-1]
    assert lw == SUBLANES * LANES
    params = [ret_gn.reshape(1, vw), wg, ba.reshape(1, lw), bi.reshape(1, lw), lam.reshape(1, lw),
              conv_w, conv_b.reshape(1, lw)]
    scratch = [
        pltpu.VMEM((batch, RET_HEADS, RET_DK, RET_DV), F32),
        pltpu.VMEM((batch, SUBLANES, lw), F32),
        pltpu.VMEM((batch, SUBLANES, LANES), F32),
        pltpu.VMEM((batch, SUBLANES * SLAB_PITCH, LANES), F32),
        pltpu.VMEM((batch, SUBLANES * SLAB_PITCH, LANES), F32),
    ]
    return _layer(h3, g_in, w_in, w_out, g_out, params, mixer=_mixer_ab, init=_mixer_ab_init,
                  cols=(qk, qk, vw, vw, lw, lw), scratch=scratch, name="layer_ab", tn=2 * MXU_COLS,
                  lp=lp, used=used, meta=meta)


_HG_LEVELS = (8, 16, 32)


def _mixer_cd_init(arl_ref, aim_ref, ldt_ref, arl_t_ref, aim_t_ref, ldt_t_ref,
                   bre_ref, bim_ref, cre_ref, cim_ref, d_ref, gw_ref, gb_ref, gn_ref, lbl_ref,
                   bm_ref, ab_ref, hcar_ref, bu_slab, sel_ref, st_ref):
    def disc(ldt, arl, aim):
        dt = jnp.exp(ldt)
        a_re = -jnp.exp(arl)
        mag = jnp.exp(dt * a_re)
        return a_re, aim, mag * jnp.cos(dt * aim), mag * jnp.sin(dt * aim)

    _, _, t_re, t_im = disc(ldt_t_ref[...], arl_t_ref[...], aim_t_ref[...])
    ab_ref[0] = t_re
    ab_ref[1] = t_im
    a_re, a_im, ab_re, ab_im = disc(ldt_ref[...], arl_ref[...], aim_ref[...])
    den = a_re * a_re + a_im * a_im
    z_re = ((ab_re - 1.0) * a_re + ab_im * a_im) / den
    z_im = (ab_im * a_re - (ab_re - 1.0) * a_im) / den
    for jb in range(bre_ref.shape[0]):
        ls = slice(jb * S5_LANES, (jb + 1) * S5_LANES)
        zr = z_re[:, ls]
        zi = z_im[:, ls]
        bb_re = zr * bre_ref[jb] - zi * bim_ref[jb]
        bb_im = zr * bim_ref[jb] + zi * bre_ref[jb]
        bm_ref[jb] = jnp.concatenate([bb_re, bb_im], axis=1).astype(BF16)
    ri = lax.broadcasted_iota(jnp.int32, (2 * HG_DK, 2 * CHUNK), 0)
    ci = lax.broadcasted_iota(jnp.int32, (2 * HG_DK, 2 * CHUNK), 1)
    same_head = (ri >= HG_DK) == (ci >= CHUNK)
    for s in range(SUBLANES):
        sel_ref[s] = jnp.logical_and(same_head, (ci & (SUBLANES - 1)) == s).astype(BF16)
    hcar_ref[...] = jnp.zeros_like(hcar_ref)
    st_ref[...] = jnp.zeros_like(st_ref)


def _mixer_cd(c, fill, u_ref, q_ref, f_ref, i_ref, g_ref,
              arl_ref, aim_ref, ldt_ref, arl_t_ref, aim_t_ref, ldt_t_ref,
              bre_ref, bim_ref, cre_ref, cim_ref, d_ref, gw_ref, gb_ref, gn_ref, lbl_ref, y_ref,
              bm_ref, ab_ref, hcar_ref, bu_slab, sel_ref, st_ref, *, layer):
    del c
    nbatch, _, s5w = u_ref.shape
    ncol = bre_ref.shape[0]
    cin = bre_ref.shape[1]
    nslab = ab_ref.shape[1]
    nhalf = nslab // SUBLANES
    gpc = S5_LANES // LANES
    hw = HG_HEADS * HG_DK
    pair = 2 * HG_DK
    scan_slots = CHUNK // SUBLANES
    fill.nslots = 2 * ncol + scan_slots + 1 + SUBLANES // 2 + nbatch * len(_HG_LEVELS)

    u2 = u_ref[...].reshape(nbatch * CHUNK, s5w)
    ub = u2.astype(BF16)
    for jb in range(ncol):
        bu = _dot(ub[:, jb * cin:(jb + 1) * cin], bm_ref[jb])
        for part in range(2):
            for gi in range(gpc):
                slab = part * nslab + jb * gpc + gi
                col = part * S5_LANES + gi * LANES
                for b in range(nbatch):
                    bu_slab[b, slab * SLAB_PITCH:slab * SLAB_PITCH + CHUNK, :] = (
                        bu[b * CHUNK:(b + 1) * CHUNK, col:col + LANES])
        fill()

    a_re = [ab_ref[0, hf * SUBLANES:(hf + 1) * SUBLANES, :] for hf in range(nhalf)]
    a_im = [ab_ref[1, hf * SUBLANES:(hf + 1) * SUBLANES, :] for hf in range(nhalf)]

    def step(t, hs):
        out = []
        for b in range(nbatch):
            for hf in range(nhalf):
                h_re, h_im = hs[2 * (b * nhalf + hf)], hs[2 * (b * nhalf + hf) + 1]
                ts_re = pl.ds(hf * SUBLANES * SLAB_PITCH + t, SUBLANES, stride=SLAB_PITCH)
                ts_im = pl.ds((nslab + hf * SUBLANES) * SLAB_PITCH + t, SUBLANES, stride=SLAB_PITCH)
                n_re = a_re[hf] * h_re - a_im[hf] * h_im + bu_slab[b, ts_re, :]
                n_im = a_re[hf] * h_im + a_im[hf] * h_re + bu_slab[b, ts_im, :]
                bu_slab[b, ts_re, :] = n_re
                bu_slab[b, ts_im, :] = n_im
                out += [n_re, n_im]
        return tuple(out)

    init = []
    for b in range(nbatch):
        for hf in range(nhalf):
            init += [hcar_ref[b, 0, hf], hcar_ref[b, 1, hf]]
    hs = tuple(init)
    for t in range(CHUNK):
        hs = step(t, hs)
        if t % SUBLANES == SUBLANES - 1:
            fill()
    for b in range(nbatch):
        for hf in range(nhalf):
            hcar_ref[b, 0, hf] = hs[2 * (b * nhalf + hf)]
            hcar_ref[b, 1, hf] = hs[2 * (b * nhalf + hf) + 1]

    def states(part, jb):
        return jnp.concatenate(
            [jnp.concatenate(
                [bu_slab[b, (part * nslab + jb * gpc + gi) * SLAB_PITCH:
                         (part * nslab + jb * gpc + gi) * SLAB_PITCH + CHUNK, :] for gi in range(gpc)], axis=1)
             for b in range(nbatch)], axis=0).astype(BF16)

    ys = []
    for jb in range(ncol):
        cs = slice(jb * cin, (jb + 1) * cin)
        y = _dot(states(0, jb), cre_ref[jb]) - _dot(states(1, jb), cim_ref[jb]) + d_ref[:, cs] * u2[:, cs]
        ys.append(_gelu_tanh(y))
        fill()
    yg = jnp.concatenate(ys, axis=1)
    yc = yg * _sigmoid(_dot(yg.astype(BF16), gw_ref[...]) + gb_ref[...])
    y_ref[:, :, :s5w] = yc.reshape(nbatch, CHUNK, s5w).astype(y_ref.dtype)
    fill()

    logits = lbl_ref[...]
    pexp = jnp.exp(logits - jnp.max(logits, axis=0, keepdims=True))
    psm = pexp / jnp.sum(pexp, axis=0, keepdims=True)
    lb = jnp.zeros_like(psm[0:1, :])
    for l in range(layer):
        lb = lb + psm[l:l + 1, :]

    ti = lax.broadcasted_iota(jnp.int32, (CHUNK, CHUNK), 0)
    si = lax.broadcasted_iota(jnp.int32, (CHUNK, CHUNK), 1)
    tril = (ti >= si).astype(BF16)
    tp = lax.broadcasted_iota(jnp.int32, (CHUNK, 2 * CHUNK), 0)
    sp = lax.broadcasted_iota(jnp.int32, (CHUNK, 2 * CHUNK), 1) & (CHUNK - 1)
    diag_mask = jnp.logical_and((tp >> 3) == (sp >> 3), sp <= tp)
    nvr = CHUNK // SUBLANES

    def both_heads(x, p):
        xa = x[:, p * pair:p * pair + HG_DK]
        xb_ = x[:, p * pair + HG_DK:(p + 1) * pair]
        zero = jnp.zeros_like(xa)
        return jnp.concatenate([jnp.concatenate([xa, zero], axis=1),
                                jnp.concatenate([zero, xb_], axis=1)], axis=0)

    def gates(b):
        f = lb + (1.0 - lb) * jax.nn.sigmoid(f_ref[b])
        logf = jnp.log(f)
        kk = 1.0 - f
        lf_hi = logf.astype(BF16)
        lf_lo = (logf - lf_hi.astype(F32)).astype(BF16)
        cum = _dot(tril, lf_hi) + _dot(tril, lf_lo)
        return kk, cum

    gated = [gates(b) for b in range(nbatch)]

    q3 = [q_ref[b].reshape(nvr, SUBLANES, hw) for b in range(nbatch)]
    c3 = [cum.reshape(nvr, SUBLANES, hw) for _, cum in gated]
    e3 = [(jnp.log(jnp.maximum(kk, 0.0)) - cum).reshape(nvr, SUBLANES, hw) for kk, cum in gated]
    dsum = [jnp.zeros((nbatch * CHUNK, 2 * CHUNK), F32) for _ in range(HG_HEADS // 2)]
    for s in range(SUBLANES):
        wb = jnp.concatenate(
            [(q3[b] * jnp.exp(jnp.minimum(c3[b] + e3[b][:, s:s + 1, :], 0.0))).reshape(CHUNK, hw)
             for b in range(nbatch)], axis=0).astype(BF16)
        for p in range(HG_HEADS // 2):
            dsum[p] = dsum[p] + _dot(wb[:, p * pair:(p + 1) * pair], sel_ref[s])
        if s % 2 == 1:
            fill()

    def hg_body(b, carry):
        kk, cum = gated[b]
        total = cum[CHUNK - 1:CHUNK, :]
        q = q_ref[b]
        iv = i_ref[b]
        ivb = iv.astype(BF16)
        q_in = (q * jnp.exp(cum)).astype(BF16)
        k_dec = kk * jnp.exp(total - cum)
        dec = jnp.exp(total)

        att = [jnp.zeros((CHUNK, 2 * CHUNK), F32) for _ in range(HG_HEADS // 2)]
        ends = [cum[v * SUBLANES + SUBLANES - 1:(v + 1) * SUBLANES, :] for v in range(nvr)]
        zeros8 = jnp.zeros((SUBLANES, hw), F32)
        for n in _HG_LEVELS:
            per = n // SUBLANES
            qparts, kparts = [], []
            for v in range(nvr):
                blk = v // per
                vs = slice(v * SUBLANES, (v + 1) * SUBLANES)
                if blk % 2 == 1:
                    qparts.append(q[vs] * jnp.exp(cum[vs] - ends[blk * per - 1]))
                    kparts.append(zeros8)
                else:
                    qparts.append(zeros8)
                    kparts.append(kk[vs] * jnp.exp(ends[blk * per + per - 1] - cum[vs]))
            q_t = jnp.concatenate(qparts, axis=0).astype(BF16)
            k_t = jnp.concatenate(kparts, axis=0).astype(BF16)
            shift = n.bit_length()
            for p in range(HG_HEADS // 2):
                a_n = _dot_nt(q_t[:, p * pair:(p + 1) * pair], both_heads(k_t, p))
                if 2 * n < CHUNK:
                    a_n = jnp.where((tp >> shift) == (sp >> shift), a_n, 0.0)
                att[p] = att[p] + a_n
            fill()
        for p in range(HG_HEADS // 2):
            diag = dsum[p][b * CHUNK:(b + 1) * CHUNK, :]
            a_all = (att[p] + jnp.where(diag_mask, diag, 0.0)).astype(BF16)
            o_pair = _dot(a_all, both_heads(ivb, p))
            for hh in range(2):
                h = 2 * p + hh
                hs_ = slice(h * HG_DK, (h + 1) * HG_DK)
                st = st_ref[b, h]
                oh = o_pair[:, hh * HG_DK:(hh + 1) * HG_DK] + _dot_nt(q_in[:, hs_], st.astype(BF16))
                st_ref[b, h] = dec[:, hs_] * st + _dot(iv[:, hs_].T.astype(BF16), k_dec[:, hs_].astype(BF16))
                ms = jnp.mean(oh * oh, axis=-1, keepdims=True)
                gt = g_ref[b, :, hs_]
                y_ref[b, :, s5w + h * HG_DK:s5w + (h + 1) * HG_DK] = (
                    oh * lax.rsqrt(ms + EPS) * gn_ref[:, hs_] * _silu(gt)).astype(y_ref.dtype)
        return carry

    for b in range(nbatch):
        hg_body(b, 0)


def _block_diag(t):
    n, g, r, c = t.shape
    on_diagonal = jnp.eye(g, dtype=bool)[None, :, None, :, None]
    return jnp.where(on_diagonal, t[:, :, :, None, :], 0).reshape(n, g * r, g * c)


def _layer_cd(h3, g_in, w_in, w_out, g_out, a_re_log, a_im, b_re, b_im, c_re, c_im, d, log_dt, glu_w, glu_b,
              hg_gn, lb_logits, *, layer, used):
    batch = h3.shape[0]
    groups, state = a_re_log.shape
    s5w = groups * S5_GROUP
    nstate = groups * state
    gpc = S5_LANES // state
    ncol = groups // gpc
    cin = gpc * S5_GROUP
    hw = HG_HEADS * HG_DK
    nslab = nstate // LANES
    assert s5w == hw and nslab % SUBLANES == 0

    bre = _block_diag(jnp.transpose(b_re.reshape(ncol, gpc, state, S5_GROUP), (0, 1, 3, 2)))
    bim = _block_diag(jnp.transpose(b_im.reshape(ncol, gpc, state, S5_GROUP), (0, 1, 3, 2)))
    cre = _block_diag(jnp.transpose(c_re.reshape(ncol, gpc, S5_GROUP, state), (0, 1, 3, 2))).astype(BF16)
    cim = _block_diag(jnp.transpose(c_im.reshape(ncol, gpc, S5_GROUP, state), (0, 1, 3, 2))).astype(BF16)
    ldt = jnp.repeat(log_dt, state)
    params = [a_re_log.reshape(1, nstate), a_im.reshape(1, nstate), ldt.reshape(1, nstate),
              a_re_log.reshape(nslab, LANES), a_im.reshape(nslab, LANES), ldt.reshape(nslab, LANES),
              bre, bim, cre, cim, d.reshape(1, s5w), glu_w.astype(BF16), glu_b.reshape(1, s5w),
              hg_gn.reshape(1, hw), lb_logits]
    scratch = [
        pltpu.VMEM((ncol, cin, 2 * S5_LANES), BF16),
        pltpu.VMEM((2, nslab, LANES), F32),
        pltpu.VMEM((batch, 2, nslab // SUBLANES, SUBLANES, LANES), F32),
        pltpu.VMEM((batch, 2 * nslab * SLAB_PITCH, LANES), F32),
        pltpu.VMEM((SUBLANES, 2 * HG_DK, 2 * CHUNK), BF16),
        pltpu.VMEM((batch, HG_HEADS, HG_DK, HG_DK), F32),
    ]
    return _layer(h3, g_in, w_in, w_out, g_out, params, mixer=functools.partial(_mixer_cd, layer=layer),
                  init=_mixer_cd_init, cols=(s5w,) * 5, scratch=scratch, name="layer_cd", tn=MXU_COLS,
                  lp=h3.shape[1], used=used)


def _pack_lru_gates(wa, wi):
    nblk, bd, _ = wa.shape
    per = LANES // bd
    wa_bd = _block_diag(wa.reshape(nblk // per, per, bd, bd))
    wi_bd = _block_diag(wi.reshape(nblk // per, per, bd, bd))
    return jnp.concatenate([wa_bd, wi_bd], axis=2).astype(BF16)


def kernel(x, meta, w_in_ab, w_out_ab, ret_gn, rg_wa, rg_ba, rg_wi, rg_bi, rg_lam, rg_conv_w, rg_conv_b,
           w_in_cd, w_out_cd, s5_a_re_log, s5_a_im, s5_b_re, s5_b_im, s5_c_re, s5_c_im, s5_d, s5_log_dt,
           s5_glu_w, s5_glu_b, hg_gn, hg_lb_logits, norm_g, mlp_w1, mlp_w2):
    batch, seq, d = x.shape
    depth = norm_g.shape[0]
    used = PAD + N_META + seq
    assert used % CHUNK == 0
    lp = -(-used // (2 * CHUNK)) * (2 * CHUNK)
    m = batch * lp

    h = x
    for l in range(depth):
        jdx = l // 2
        if l % 2 == 0:
            h = _layer_ab(h, norm_g[l, 0], w_in_ab[jdx].astype(BF16), w_out_ab[jdx].astype(BF16), norm_g[l, 1],
                          ret_gn[jdx], _pack_lru_gates(rg_wa[jdx], rg_wi[jdx]), rg_ba[jdx], rg_bi[jdx],
                          rg_lam[jdx], rg_conv_w[jdx], rg_conv_b[jdx], lp=lp, used=used,
                          meta=meta.astype(x.dtype) if l == 0 else None)
        else:
            h = _layer_cd(h, norm_g[l, 0], w_in_cd[jdx].astype(BF16), w_out_cd[jdx].astype(BF16), norm_g[l, 1],
                          s5_a_re_log[jdx], s5_a_im[jdx], s5_b_re[jdx], s5_b_im[jdx], s5_c_re[jdx], s5_c_im[jdx],
                          s5_d[jdx], s5_log_dt[jdx], s5_glu_w[jdx], s5_glu_b[jdx], hg_gn[jdx], hg_lb_logits,
                          layer=l, used=used)
        if l + 1 < depth:
            h = _mlp(h.reshape(m, d), norm_g[l, 2], mlp_w1, mlp_w2, norm_g[l, 3], layer=l,
                     tm=_largest_divisor(lp, MLP_ROWS, MLP_ROW_ALIGN), tn=1024,
                     rows_per_batch=lp).reshape(batch, lp, d)
        else:
            h = _mlp_frames(h, norm_g[l, 2], mlp_w1, mlp_w2, norm_g[l, 3], layer=l, first=PAD + N_META,
                            count=seq, tm=_largest_divisor(seq, MLP_ROWS, MLP_ROW_ALIGN), tn=1024)
    return h
```
